```python
import math
import jax, jax.numpy as jnp
from jax import lax
import numpy as np

D_MODEL = 1024
BATCH = 16
SEQ = 256
DEPTH = 2
DEC_BATCH = 8
DEC_SEQ = 2048
PAST_LEN = 512

GRID_W = 64
MIX_WIDTH = D_MODEL
ML_WIDTH = MIX_WIDTH // 4
ML_HEADS = 4
ML_DK = ML_WIDTH // ML_HEADS
ML_DV = ML_DK
DN_WIDTH = MIX_WIDTH // 4
DN_HEADS = 4
DN_DK = DN_WIDTH // DN_HEADS
DN_DV = DN_DK
CONV_K = 3
FT_WIDTH = MIX_WIDTH // 4
FT_GROUPS = 4
FT_GC = FT_WIDTH // FT_GROUPS
DA_WIDTH = MIX_WIDTH - ML_WIDTH - DN_WIDTH - FT_WIDTH
DA_HEADS = 4
DA_V = DA_WIDTH // DA_HEADS
DA_QK = DA_V // 2
ROPE_AXIS_PAIRS = DA_QK // 4
ROPE_BASE = 10000.0
D_FF = ((8 * D_MODEL // 3 + 127) // 128) * 128
HALF = 0.5
IN_WIDTH = 4 * ML_WIDTH + 4 * ML_HEADS + 4 * DN_WIDTH + 4 * DN_HEADS + FT_WIDTH + 3 * DA_WIDTH
N_MOD = 9
CHUNK = 64
Q_BLOCK = 128
EPS = 1e-6
F32 = jnp.float32

kernel_name = 'hybrid_diffusion_prefix_trunk_step'


def rms_norm(x, g):
    xf = x.astype(F32)
    y = xf * lax.rsqrt(jnp.mean(xf * xf, axis=-1, keepdims=True) + EPS)
    return y.astype(x.dtype) * g


def l2_normalize(x):
    return x * lax.rsqrt(jnp.sum(x * x, axis=-1, keepdims=True) + EPS)


def swiglu(h, w_in, w_out):
    a, b = jnp.split(h @ w_in, 2, axis=-1)
    return (jax.nn.silu(a) * b) @ w_out


def centred_conv(x, w):
    T = x.shape[1]
    pad = CONV_K // 2
    xp = jnp.pad(x, ((0, 0), (pad, pad), (0, 0)))
    return sum(xp[:, j:j + T] * w[j] for j in range(CONV_K))


def to_chunks(a):
    B, H, T = a.shape[:3]
    a = a.reshape((B, H, T // CHUNK, CHUNK) + a.shape[3:])
    return jnp.moveaxis(a, 2, 0)


def from_chunks(a):
    a = jnp.moveaxis(a, 0, 2)
    return a.reshape(a.shape[:2] + (a.shape[2] * a.shape[3],) + a.shape[4:])


def mlstm_scan(q, k, v, ig, lf, C0, n0, m0):
    mask = jnp.tril(jnp.ones((CHUNK, CHUNK), dtype=bool))

    def step(carry, inp):
        C, n, m = carry
        qc, kc, vc, ic, fc = inp
        b = jnp.cumsum(fc, axis=-1)
        inter = b + m[..., None]
        D = jnp.where(mask, b[..., :, None] - b[..., None, :] + ic[..., None, :], -jnp.inf)
        mt = jnp.maximum(inter, jnp.max(D, axis=-1))
        w_inter = jnp.exp(inter - mt)
        s = jnp.einsum('bhtd,bhsd->bhts', qc, kc) * jnp.exp(D - mt[..., None])
        num = w_inter[..., None] * jnp.einsum('bhtd,bhde->bhte', qc, C) + jnp.einsum('bhts,bhse->bhte', s, vc)
        den = w_inter * jnp.einsum('bhtd,bhd->bht', qc, n) + jnp.sum(s, axis=-1)
        h = num / jnp.maximum(jnp.abs(den), jnp.exp(-mt))[..., None]
        g = b[..., -1:] - b + ic
        m_new = mt[..., -1]
        a_prev = jnp.exp(b[..., -1] + m - m_new)
        a_s = jnp.exp(g - m_new[..., None])
        C_new = a_prev[..., None, None] * C + jnp.einsum('bhs,bhsd,bhse->bhde', a_s, kc, vc)
        n_new = a_prev[..., None] * n + jnp.einsum('bhs,bhsd->bhd', a_s, kc)
        return (C_new, n_new, m_new), h

    xs = tuple(to_chunks(a) for a in (q, k, v, ig, lf))
    (C, n, m), h = lax.scan(step, (C0, n0, m0), xs)
    return from_chunks(h), (C, n, m)


def gated_delta_scan(q, k, v, beta, lg, S0):
    incl = jnp.tril(jnp.ones((CHUNK, CHUNK), dtype=bool))
    strict = jnp.tril(jnp.ones((CHUNK, CHUNK), dtype=bool), -1)
    eye = jnp.eye(CHUNK, dtype=F32)
    dv = v.shape[-1]

    def step(S, inp):
        qc, kc, vc, bc, gc = inp
        g = jnp.cumsum(gc, axis=-1)
        decay = jnp.exp(jnp.where(incl, g[..., :, None] - g[..., None, :], -jnp.inf))
        kb = kc * bc[..., None]
        a = jnp.where(strict, jnp.einsum('bhtd,bhsd->bhts', kb, kc) * decay, 0.0)
        rhs = jnp.concatenate([vc * bc[..., None], kb * jnp.exp(g)[..., None]], axis=-1)
        sol = lax.linalg.triangular_solve(eye + a, rhs, left_side=True, lower=True, unit_diagonal=True)
        u, w = sol[..., :dv], sol[..., dv:]
        v_new = u - jnp.einsum('bhtd,bhde->bhte', w, S)
        attn = jnp.einsum('bhtd,bhsd->bhts', qc, kc) * decay
        o = jnp.einsum('bhtd,bhde->bhte', qc * jnp.exp(g)[..., None], S) + jnp.einsum('bhts,bhse->bhte', attn, v_new)
        g_last = g[..., -1:]
        S_new = jnp.exp(g_last)[..., None] * S + jnp.einsum('bhsd,bhse->bhde', kc * jnp.exp(g_last - g)[..., None], v_new)
        return S_new, o

    xs = tuple(to_chunks(a) for a in (q, k, v, beta, lg))
    S, o = lax.scan(step, S0, xs)
    return from_chunks(o), (S,)


def maybe_flip(a, d):
    return jnp.flip(a, axis=2) if d == 1 else a


def run_bidir(scan_fn, shared, per_dir, init):
    outs, finals = [], []
    for d in range(2):
        args = [maybe_flip(a, d) for a in shared] + [maybe_flip(a[:, d], d) for a in per_dir] + [s[:, d] for s in init]
        o, fin = scan_fn(*args)
        outs.append(maybe_flip(o, d))
        finals.append(fin)
    states = tuple(jnp.stack([finals[0][j], finals[1][j]], axis=1) for j in range(len(init)))
    return outs[0] + outs[1], states


def axial_rope(T, dtype):
    rows = T // GRID_W
    row = jnp.repeat(jnp.arange(rows), GRID_W).astype(F32)
    col = jnp.tile(jnp.arange(GRID_W), rows).astype(F32)
    inv = ROPE_BASE ** (-jnp.arange(ROPE_AXIS_PAIRS, dtype=F32) / ROPE_AXIS_PAIRS)
    ang = jnp.concatenate([row[:, None] * inv, col[:, None] * inv], axis=-1)
    return jnp.cos(ang).astype(dtype), jnp.sin(ang).astype(dtype)


def apply_rope(x, cos, sin):
    xr = x.reshape(x.shape[:-1] + (DA_QK // 2, 2))
    x1, x2 = xr[..., 0], xr[..., 1]
    c = cos[None, :, None, None, :]
    s = sin[None, :, None, None, :]
    return jnp.stack([x1 * c - x2 * s, x1 * s + x2 * c], axis=-1).reshape(x.shape)


def diff_attention(q, k, v, lam):
    B, Tq = q.shape[:2]
    qb = jnp.moveaxis(q.reshape((B, Tq // Q_BLOCK, Q_BLOCK) + q.shape[2:]), 1, 0)

    def block(qblk):
        s = jnp.einsum('bqhmd,bkhmd->bhmqk', qblk, k).astype(F32) * (DA_QK ** -0.5)
        p = jax.nn.softmax(s, axis=-1)
        w = p[:, :, 0] - lam * p[:, :, 1]
        return jnp.einsum('bhqk,bkhe->bqhe', w.astype(v.dtype), v)

    out = lax.map(block, qb)
    return jnp.moveaxis(out, 0, 1).reshape(B, Tq, q.shape[2], v.shape[-1])


def in_proj_split_points():
    sizes = (ML_WIDTH,) * 4 + (2 * ML_HEADS,) * 2 + (DN_WIDTH,) * 4 + (2 * DN_HEADS,) * 2 + (FT_WIDTH,) + (DA_WIDTH,) * 3
    points, acc = [], 0
    for s in sizes[:-1]:
        acc += s
        points.append(acc)
    return points


def token_mixing(h, lp, lam_init, rope, init_ml, init_dn, ctx_kv):
    B, T, _ = h.shape
    (ml_q, ml_k, ml_v, ml_o, ml_i, ml_f, dn_q, dn_k, dn_v, dn_g, dn_b, dn_a,
     ft_x, da_q, da_k, da_v) = jnp.split(h @ lp['w_in'], in_proj_split_points(), axis=-1)

    def heads(a, H):
        return a.astype(F32).reshape(B, T, H, -1).transpose(0, 2, 1, 3)

    def dir_split(a, H):
        return a.astype(F32).reshape(B, T, 2, H).transpose(0, 2, 3, 1)

    init_ml = tuple(s.astype(F32) for s in init_ml)
    init_dn = tuple(s.astype(F32) for s in init_dn)

    ig = dir_split(ml_i, ML_HEADS) + lp['ml_i_bias'].astype(F32)[None, :, :, None]
    lf = jax.nn.log_sigmoid(dir_split(ml_f, ML_HEADS) + lp['ml_f_bias'].astype(F32)[None, :, :, None])
    h_ml, ml_state = run_bidir(mlstm_scan, (heads(ml_q, ML_HEADS), heads(ml_k, ML_HEADS) * ML_DK ** -0.5,
                                            heads(ml_v, ML_HEADS)), (ig, lf), init_ml)
    h_ml = h_ml.transpose(0, 2, 1, 3) * jax.nn.sigmoid(ml_o.astype(F32)).reshape(B, T, ML_HEADS, ML_DV)
    y_ml = rms_norm(h_ml, lp['ml_norm']).reshape(B, T, ML_WIDTH).astype(h.dtype)

    qkv = jax.nn.silu(centred_conv(jnp.concatenate([dn_q, dn_k, dn_v], axis=-1), lp['dn_conv']))
    cq, ck, cv = jnp.split(qkv, 3, axis=-1)
    dq = l2_normalize(heads(cq, DN_HEADS)) * DN_DK ** -0.5
    dk = l2_normalize(heads(ck, DN_HEADS))
    dvv = heads(cv, DN_HEADS)
    beta = jax.nn.sigmoid(dir_split(dn_b, DN_HEADS))
    log_alpha = -jnp.exp(lp['dn_A_log'].astype(F32))[None, :, :, None] * jax.nn.softplus(
        dir_split(dn_a, DN_HEADS) + lp['dn_dt_bias'].astype(F32)[None, :, :, None])
    o_dn, dn_state = run_bidir(gated_delta_scan, (dq, dk, dvv), (beta, log_alpha), init_dn)
    o_dn = o_dn.transpose(0, 2, 1, 3)
    y_dn = (rms_norm(o_dn, lp['dn_norm']) * jax.nn.silu(dn_g.astype(F32).reshape(B, T, DN_HEADS, DN_DV)))
    y_dn = y_dn.reshape(B, T, DN_WIDTH).astype(h.dtype)

    ft = ft_x.astype(F32).reshape(B, T, FT_GROUPS, FT_GC)
    y_ft = jnp.fft.fft2(ft, axes=(1, 3), norm='ortho').real.reshape(B, T, FT_WIDTH).astype(h.dtype)

    q = da_q.reshape(B, T, DA_HEADS, 2, DA_QK)
    k = da_k.reshape(B, T, DA_HEADS, 2, DA_QK)
    v = da_v.reshape(B, T, DA_HEADS, DA_V)
    if rope is None:
        keys, vals = k, v
    else:
        q = apply_rope(q, *rope)
        keys = jnp.concatenate([apply_rope(k, *rope), ctx_kv[0]], axis=1)
        vals = jnp.concatenate([v, ctx_kv[1]], axis=1)
    lq1, lk1, lq2, lk2 = lp['da_lambda'].astype(F32)
    lam = jnp.exp(jnp.sum(lq1 * lk1)) - jnp.exp(jnp.sum(lq2 * lk2)) + lam_init
    o_da = diff_attention(q, keys, vals, lam)
    y_da = (rms_norm(o_da, lp['da_norm']) * (1.0 - lam_init)).reshape(B, T, DA_WIDTH).astype(h.dtype)

    y = jnp.concatenate([y_ml, y_dn, y_ft, y_da], axis=-1) @ lp['w_out']
    return y, (k, v, ml_state, dn_state)


def modulation(cond, w_mod, b_mod):
    return (jax.nn.silu(cond) @ w_mod + b_mod).reshape(cond.shape[0], N_MOD, 1, D_MODEL)


def modulate(x, g, shift, scale):
    return rms_norm(x, g) * (1.0 + scale) + shift


def trunk_layer(x, mod, lp, lam_init, rope, init_ml, init_dn, ctx_kv):
    y = swiglu(modulate(x, lp['norm_pre'][0], mod[:, 0], mod[:, 1]), lp['ffn_w_in'][0], lp['ffn_w_out'][0])
    x = x + HALF * mod[:, 2] * rms_norm(y, lp['norm_post'][0])
    y, aux = token_mixing(modulate(x, lp['norm_pre'][1], mod[:, 3], mod[:, 4]), lp, lam_init, rope,
                          init_ml, init_dn, ctx_kv)
    x = x + mod[:, 5] * rms_norm(y, lp['norm_post'][1])
    y = swiglu(modulate(x, lp['norm_pre'][2], mod[:, 6], mod[:, 7]), lp['ffn_w_in'][1], lp['ffn_w_out'][1])
    x = x + HALF * mod[:, 8] * rms_norm(y, lp['norm_post'][2])
    return x, aux


def setup_inputs(seed: int = 0) -> dict:
    key = jax.random.key(seed)
    keys = iter(jax.random.split(key, 32))

    def nrm(shape, scale=1.0):
        return scale * jax.random.normal(next(keys), shape, F32)

    def gain(shape):
        return 1.0 + nrm(shape, 0.02)

    x_prompt = nrm((BATCH, SEQ, D_MODEL))
    x_sample = nrm((DEC_BATCH, DEC_SEQ, D_MODEL))
    cache_k = nrm((DEC_BATCH, DEPTH, PAST_LEN, DA_HEADS, 2, DA_QK))
    cache_v = nrm((DEC_BATCH, DEPTH, PAST_LEN, DA_HEADS, DA_V))
    state_mlstm_C = nrm((DEC_BATCH, DEPTH, 2, ML_HEADS, ML_DK, ML_DV), 0.3)
    state_mlstm_n = nrm((DEC_BATCH, DEPTH, 2, ML_HEADS, ML_DK), 0.3)
    state_mlstm_m = nrm((DEC_BATCH, DEPTH, 2, ML_HEADS), 0.5)
    state_delta_S = nrm((DEC_BATCH, DEPTH, 2, DN_HEADS, DN_DK, DN_DV), 0.3)
    c = nrm((DEC_BATCH, D_MODEL))
    c_ctx = nrm((D_MODEL,))
    w_mod = nrm((DEPTH, D_MODEL, N_MOD * D_MODEL), 0.5 * D_MODEL ** -0.5)
    b_mod = nrm((DEPTH, N_MOD * D_MODEL), 0.02)
    norm_pre = gain((DEPTH, 3, D_MODEL))
    norm_post = gain((DEPTH, 3, D_MODEL))
    ffn_w_in = nrm((DEPTH, 2, D_MODEL, 2 * D_FF), D_MODEL ** -0.5)
    ffn_w_out = nrm((DEPTH, 2, D_FF, D_MODEL), D_FF ** -0.5)
    w_in = nrm((DEPTH, D_MODEL, IN_WIDTH), D_MODEL ** -0.5)
    w_out = nrm((DEPTH, MIX_WIDTH, D_MODEL), MIX_WIDTH ** -0.5)
    ml_i_bias = nrm((DEPTH, 2, ML_HEADS), 0.1)
    ml_f_bias = 3.0 + 3.0 * jax.random.uniform(next(keys), (DEPTH, 2, ML_HEADS), F32)
    ml_norm = gain((DEPTH, ML_HEADS, ML_DV))
    dn_conv = nrm((DEPTH, CONV_K, 3 * DN_WIDTH), CONV_K ** -0.5)
    dn_A_log = jnp.log(jax.random.uniform(next(keys), (DEPTH, 2, DN_HEADS), F32, 1.0, 16.0))
    dt = jnp.exp(jax.random.uniform(next(keys), (DEPTH, 2, DN_HEADS), F32, math.log(1e-3), math.log(1e-1)))
    dn_dt_bias = dt + jnp.log(-jnp.expm1(-dt))
    dn_norm = gain((DEPTH, DN_DV))
    da_lambda = nrm((DEPTH, 4, DA_QK), 0.1)
    da_norm = gain((DEPTH, DA_V))
    return {'x_prompt': x_prompt, 'x_sample': x_sample, 'cache_k': cache_k, 'cache_v': cache_v,
            'state_mlstm_C': state_mlstm_C, 'state_mlstm_n': state_mlstm_n, 'state_mlstm_m': state_mlstm_m,
            'state_delta_S': state_delta_S, 'c': c, 'c_ctx': c_ctx, 'w_mod': w_mod, 'b_mod': b_mod,
            'norm_pre': norm_pre, 'norm_post': norm_post, 'ffn_w_in': ffn_w_in, 'ffn_w_out': ffn_w_out,
            'w_in': w_in, 'w_out': w_out, 'ml_i_bias': ml_i_bias, 'ml_f_bias': ml_f_bias, 'ml_norm': ml_norm,
            'dn_conv': dn_conv, 'dn_A_log': dn_A_log, 'dn_dt_bias': dn_dt_bias, 'dn_norm': dn_norm,
            'da_lambda': da_lambda, 'da_norm': da_norm}


def reference(x_prompt, x_sample, cache_k, cache_v, state_mlstm_C, state_mlstm_n, state_mlstm_m, state_delta_S,
              c, c_ctx, w_mod, b_mod, norm_pre, norm_post, ffn_w_in, ffn_w_out, w_in, w_out,
              ml_i_bias, ml_f_bias, ml_norm, dn_conv, dn_A_log, dn_dt_bias, dn_norm, da_lambda, da_norm):
    Bp = x_prompt.shape[0]
    Ts = x_sample.shape[1]
    rope = axial_rope(Ts, x_sample.dtype)
    zero_ml = (jnp.zeros((Bp, 2, ML_HEADS, ML_DK, ML_DV), F32), jnp.zeros((Bp, 2, ML_HEADS, ML_DK), F32),
               jnp.zeros((Bp, 2, ML_HEADS), F32))
    zero_dn = (jnp.zeros((Bp, 2, DN_HEADS, DN_DK, DN_DV), F32),)
    xp, xs = x_prompt, x_sample
    ks, vs, Cs, ns, ms, Ss = [], [], [], [], [], []
    for li in range(DEPTH):
        lp = {'w_in': w_in[li], 'w_out': w_out[li], 'norm_pre': norm_pre[li], 'norm_post': norm_post[li],
              'ffn_w_in': ffn_w_in[li], 'ffn_w_out': ffn_w_out[li], 'ml_i_bias': ml_i_bias[li],
              'ml_f_bias': ml_f_bias[li], 'ml_norm': ml_norm[li], 'dn_conv': dn_conv[li], 'dn_A_log': dn_A_log[li],
              'dn_dt_bias': dn_dt_bias[li], 'dn_norm': dn_norm[li], 'da_lambda': da_lambda[li], 'da_norm': da_norm[li]}
        lam_init = 0.8 - 0.6 * math.exp(-0.3 * li)
        mod_ctx = modulation(c_ctx[None], w_mod[li], b_mod[li])
        xp, (k_c, v_c, (C_c, n_c, m_c), (S_c,)) = trunk_layer(xp, mod_ctx, lp, lam_init, None, zero_ml, zero_dn, None)
        ks.append(k_c); vs.append(v_c)
        Cs.append(C_c.astype(x_prompt.dtype)); ns.append(n_c.astype(x_prompt.dtype))
        ms.append(m_c.astype(x_prompt.dtype)); Ss.append(S_c.astype(x_prompt.dtype))
        mod_lat = modulation(c, w_mod[li], b_mod[li])
        xs, _ = trunk_layer(xs, mod_lat, lp, lam_init, rope,
                            (state_mlstm_C[:, li], state_mlstm_n[:, li], state_mlstm_m[:, li]),
                            (state_delta_S[:, li],), (cache_k[:, li], cache_v[:, li]))
    return (xp, xs, jnp.stack(ks, axis=1), jnp.stack(vs, axis=1), jnp.stack(Cs, axis=1), jnp.stack(ns, axis=1),
            jnp.stack(ms, axis=1), jnp.stack(Ss, axis=1))
```

```python
import functools
import math

import numpy as np
import jax
import jax.numpy as jnp
from jax import lax
from jax.experimental import pallas as pl
from jax.experimental.pallas import tpu as pltpu

D_MODEL = 1024
DEPTH = 2
GRID_W = 64
HEADS = 4
HEAD_DIM = 64
MIX_SEC = HEADS * HEAD_DIM
DA_QK = 32
CONV_K = 3
D_FF = ((8 * D_MODEL // 3 + 127) // 128) * 128
N_MOD = 9
CHUNK = 64
N_GATE = 32
ROPE_AXIS_PAIRS = DA_QK // 4
ROPE_BASE = 10000.0
EPS = 1e-6
HALF = 0.5
F32 = jnp.float32
BF16 = jnp.bfloat16
LANES = 128
VMEM_LIMIT = 56 * 1024 * 1024

_OFF_ML = 0
_OFF_MLG = 4 * MIX_SEC
_OFF_DN = _OFF_MLG + 16
_OFF_DNG = _OFF_DN + 4 * MIX_SEC
_OFF_FT = _OFF_DNG + 16
_OFF_DA = _OFF_FT + MIX_SEC
IN_WIDTH = _OFF_DA + 3 * MIX_SEC
MAIN_WIDTH = 12 * MIX_SEC


def _cparams(sem):
    return pltpu.CompilerParams(dimension_semantics=sem, vmem_limit_bytes=VMEM_LIMIT)


def _mm(a, b):
    return jnp.dot(a.astype(BF16), b.astype(BF16), preferred_element_type=F32)


def _mm_nt(a, b):
    return lax.dot_general(a.astype(BF16), b.astype(BF16), (((1,), (1,)), ((), ())), preferred_element_type=F32)


def _mm_tn(a, b):
    return lax.dot_general(a.astype(BF16), b.astype(BF16), (((0,), (0,)), ((), ())), preferred_element_type=F32)


def _split(a):
    hi = a.astype(BF16)
    lo = (a - hi.astype(F32)).astype(BF16)
    return hi, lo


def _mm3(a, b):
    ah, al = _split(a)
    bh, bl = _split(b)
    dot = functools.partial(jnp.dot, preferred_element_type=F32)
    return dot(ah, bh) + (dot(ah, bl) + dot(al, bh))


def _mm_exact_rhs(a, b_bf16):
    ah, al = _split(a)
    dot = functools.partial(jnp.dot, preferred_element_type=F32)
    return dot(ah, b_bf16) + dot(al, b_bf16)


def _mm_exact_lhs(a_bf16, b):
    bh, bl = _split(b)
    dot = functools.partial(jnp.dot, preferred_element_type=F32)
    return dot(a_bf16, bh) + dot(a_bf16, bl)


def _rms(x):
    return x * lax.rsqrt(jnp.mean(x * x, axis=-1, keepdims=True) + EPS)


def _silu(x):
    return x * jax.nn.sigmoid(x)


def _log_sigmoid(x):
    return jnp.minimum(x, 0.0) - jnp.log1p(jnp.exp(-jnp.abs(x)))


def _softplus(x):
    return jnp.maximum(x, 0.0) + jnp.log1p(jnp.exp(-jnp.abs(x)))


def _head_mean_sq(x, seg_ref):
    return _mm_exact_rhs(x * x, seg_ref[...]) * (1.0 / HEAD_DIM)


def _tri_masks():
    row = lax.broadcasted_iota(jnp.int32, (CHUNK, CHUNK), 0)
    col = lax.broadcasted_iota(jnp.int32, (CHUNK, CHUNK), 1)
    return row >= col, row <= col, row > col, row < col


def _mod_kernel(c_ref, w_ref, b_ref, o_ref):
    c = c_ref[...]
    o_ref[...] = _mm(_silu(c), w_ref[...]) + b_ref[...]


def _modulation(cond, w_mod, b_mod):
    R = cond.shape[0]
    return pl.pallas_call(
        _mod_kernel,
        grid=(DEPTH, N_MOD),
        in_specs=[pl.BlockSpec((R, D_MODEL), lambda l, j: (0, 0)),
                  pl.BlockSpec((None, D_MODEL, D_MODEL), lambda l, j: (l, 0, j)),
                  pl.BlockSpec((None, None, 1, D_MODEL), lambda l, j: (l, j, 0, 0))],
        out_specs=pl.BlockSpec((None, None, R, D_MODEL), lambda l, j: (l, j, 0, 0)),
        out_shape=jax.ShapeDtypeStruct((DEPTH, N_MOD, R, D_MODEL), F32),
        compiler_params=_cparams(("arbitrary", "arbitrary")),
        name="modulation",
    )(cond, w_mod, b_mod.reshape(DEPTH, N_MOD, 1, D_MODEL))


def _ffn_kernel(x_ref, mod_ref, gpre_ref, gpost_ref, wa_ref, wb_ref, wo_ref, o_ref, h_scr, acc_scr, *, m0, n_ff):
    j = pl.program_id(1)

    @pl.when(j == 0)
    def _():
        h = _rms(x_ref[...]) * gpre_ref[...] * (1.0 + mod_ref[m0 + 1:m0 + 2, :]) + mod_ref[m0:m0 + 1, :]
        h_scr[...] = h.astype(BF16)
        acc_scr[...] = jnp.zeros_like(acc_scr)

    h = h_scr[...]
    a = jnp.dot(h, wa_ref[...], preferred_element_type=F32)
    b = jnp.dot(h, wb_ref[...], preferred_element_type=F32)
    acc_scr[...] += _mm(_silu(a) * b, wo_ref[...])

    @pl.when(j == n_ff - 1)
    def _():
        y = _rms(acc_scr[...]) * gpost_ref[...]
        o_ref[...] = x_ref[...] + HALF * mod_ref[m0 + 2:m0 + 3, :] * y


def _ffn_tiles(T):
    tm = min(T, 1024)
    return tm, 256


def _ffn_block(x, mod, g_pre, g_post, w_in, w_out, m0):
    B, T, _ = x.shape
    tm, tf = _ffn_tiles(T)
    tpb, n_ff = T // tm, D_FF // tf
    return pl.pallas_call(
        functools.partial(_ffn_kernel, m0=m0, n_ff=n_ff),
        grid=(B * tpb, n_ff),
        in_specs=[pl.BlockSpec((None, tm, D_MODEL), lambda i, j: (i // tpb, i % tpb, 0)),
                  pl.BlockSpec((None, N_MOD, D_MODEL), lambda i, j: (i // tpb, 0, 0)),
                  pl.BlockSpec((1, D_MODEL), lambda i, j: (0, 0)),
                  pl.BlockSpec((1, D_MODEL), lambda i, j: (0, 0)),
                  pl.BlockSpec((D_MODEL, tf), lambda i, j: (0, j)),
                  pl.BlockSpec((D_MODEL, tf), lambda i, j: (0, j + n_ff)),
                  pl.BlockSpec((tf, D_MODEL), lambda i, j: (j, 0))],
        out_specs=pl.BlockSpec((None, tm, D_MODEL), lambda i, j: (i // tpb, i % tpb, 0)),
        out_shape=jax.ShapeDtypeStruct(x.shape, F32),
        scratch_shapes=[pltpu.VMEM((tm, D_MODEL), BF16), pltpu.VMEM((tm, D_MODEL), F32)],
        compiler_params=_cparams(("parallel", "arbitrary")),
        name="ffn_block",
    )(x, mod, g_pre, g_post, w_in, w_in, w_out)


def _rope(x, cos, sin_signed):
    lane = lax.broadcasted_iota(jnp.int32, x.shape, 1)
    nxt = pltpu.roll(x, x.shape[1] - 1, 1)
    prv = pltpu.roll(x, 1, 1)
    swapped = jnp.where(lane % 2 == 0, nxt, prv)
    return x * cos + swapped * sin_signed


def _inproj_kernel(*refs, rope):
    if rope:
        (x_ref, mod_ref, g_ref, w_ref, wg_ref, wgt_ref, cos_ref, sin_ref,
         ml_ref, dn_ref, dng_ref, ft_ref, q_ref, k_ref, v_ref, gc_ref, gr_ref) = refs
    else:
        (x_ref, mod_ref, g_ref, w_ref, wg_ref, wgt_ref,
         ml_ref, dn_ref, dng_ref, ft_ref, q_ref, k_ref, v_ref, gc_ref, gr_ref) = refs
    h = (_rms(x_ref[...]) * g_ref[...] * (1.0 + mod_ref[4:5, :]) + mod_ref[3:4, :]).astype(BF16)

    def proj(lo, width):
        return jnp.dot(h, w_ref[:, lo:lo + width], preferred_element_type=F32)

    ml_ref[...] = proj(0, 4 * MIX_SEC)
    dn_ref[...] = proj(4 * MIX_SEC, 3 * MIX_SEC)
    dng_ref[...] = proj(7 * MIX_SEC, MIX_SEC)
    ft_ref[...] = proj(8 * MIX_SEC, MIX_SEC)
    q = proj(9 * MIX_SEC, MIX_SEC)
    k = proj(10 * MIX_SEC, MIX_SEC)
    if rope:
        q = _rope(q, cos_ref[...], sin_ref[...])
        k = _rope(k, cos_ref[...], sin_ref[...])
    q_ref[...] = q
    k_ref[...] = k
    v_ref[...] = proj(11 * MIX_SEC, MIX_SEC)
    gc_ref[...] = jnp.dot(h, wg_ref[...], preferred_element_type=F32)
    gr_ref[...] = lax.dot_general(wgt_ref[...], h, (((1,), (1,)), ((), ())), preferred_element_type=F32)


def _inproj(x, mod, g_pre, w_main, w_gate, w_gate_t, rope_tabs):
    B, T, _ = x.shape
    tm = min(T, 512)
    tpb = T // tm
    rope = rope_tabs is not None
    tok = lambda w: pl.BlockSpec((None, tm, w), lambda i: (i // tpb, i % tpb, 0))
    full = lambda a: pl.BlockSpec(a.shape, lambda i: (0,) * a.ndim)
    in_specs = [tok(D_MODEL), pl.BlockSpec((None, N_MOD, D_MODEL), lambda i: (i // tpb, 0, 0)),
                full(g_pre), full(w_main), full(w_gate), full(w_gate_t)]
    args = [x, mod, g_pre, w_main, w_gate, w_gate_t]
    if rope:
        in_specs += [pl.BlockSpec((tm, MIX_SEC), lambda i: (i % tpb, 0))] * 2
        args += list(rope_tabs)
    widths = (4 * MIX_SEC, 3 * MIX_SEC, MIX_SEC, MIX_SEC, MIX_SEC, MIX_SEC, MIX_SEC, LANES)
    out_specs = [tok(w) for w in widths] + [pl.BlockSpec((None, N_GATE, tm), lambda i: (i // tpb, 0, i % tpb))]
    out_shape = [jax.ShapeDtypeStruct((B, T, w), F32) for w in widths] + [jax.ShapeDtypeStruct((B, N_GATE, T), F32)]
    return pl.pallas_call(
        functools.partial(_inproj_kernel, rope=rope),
        grid=(B * tpb,),
        in_specs=in_specs, out_specs=out_specs, out_shape=out_shape,
        compiler_params=_cparams(("parallel",)),
        name="in_proj",
    )(*args)


def _gate_cumsums(f_col, f_row, d):
    incl_lo, incl_up, _, _ = _tri_masks()
    lo = incl_lo.astype(BF16)
    up = incl_up.astype(BF16)
    tri, tri_t = (lo, up) if d == 0 else (up, lo)
    hi, mid, low = _split3(f_col)
    dot = functools.partial(jnp.dot, preferred_element_type=F32)
    b_col = dot(tri, hi) + (dot(tri, mid) + dot(tri, low))
    hi, mid, low = _split3(f_row)
    b_row = dot(hi, tri_t) + (dot(mid, tri_t) + dot(low, tri_t))
    return b_col, b_row


def _split3(a):
    hi = a.astype(BF16)
    r = a - hi.astype(F32)
    mid = r.astype(BF16)
    low = (r - mid.astype(F32)).astype(BF16)
    return hi, mid, low


def _mlstm_kernel(ml_ref, gc_ref, gr_ref, bias_row_ref, bias_col_ref, c0_ref, n0_ref, m0_ref,
                  h_ref, c_ref, n_ref, m_ref, *, n_chunks):
    c_ref[...] = c0_ref[...]
    n_ref[...] = n0_ref[...]
    m_ref[...] = m0_ref[...]
    incl_lo, incl_up, _, _ = _tri_masks()

    def step(c, carry):
        for d in range(2):
            cc = c if d == 0 else n_chunks - 1 - c
            r0 = pl.multiple_of(cc * CHUNK, CHUNK)
            x = ml_ref[pl.ds(r0, CHUNK), :]
            gc = gc_ref[pl.ds(r0, CHUNK), :]
            gr = gr_ref[cc]
            i_col = gc[:, 0:8] + bias_row_ref[0:1, :]
            f_col = _log_sigmoid(gc[:, 8:16] + bias_row_ref[1:2, :])
            i_row = gr[0:8, :] + bias_col_ref[:, 0:1]
            f_row = _log_sigmoid(gr[8:16, :] + bias_col_ref[:, 1:2])
            b_col, b_row = _gate_cumsums(f_col, f_row, d)
            mask = incl_lo if d == 0 else incl_up
            last = CHUNK - 1 if d == 0 else 0
            outs = []
            for hd in range(HEADS):
                r = d * HEADS + hd
                sl = slice(hd * HEAD_DIM, (hd + 1) * HEAD_DIM)
                q = x[:, sl]
                k = x[:, MIX_SEC + hd * HEAD_DIM:MIX_SEC + (hd + 1) * HEAD_DIM] * (HEAD_DIM ** -0.5)
                v = x[:, 2 * MIX_SEC + hd * HEAD_DIM:2 * MIX_SEC + (hd + 1) * HEAD_DIM]
                bc = b_col[:, r:r + 1]
                ic = i_col[:, r:r + 1]
                row_term = i_row[r:r + 1, :] - b_row[r:r + 1, :]
                m_prev = m_ref[r]
                inter = bc + m_prev
                dmat = jnp.where(mask, bc + row_term, -jnp.inf)
                mt = jnp.maximum(inter, jnp.max(dmat, axis=-1, keepdims=True))
                w_inter = jnp.exp(inter - mt)
                s = _mm_nt(q, k) * jnp.exp(dmat - mt)
                c_prev = c_ref[r]
                n_prev = n_ref[r]
                num = w_inter * _mm(q, c_prev) + _mm(s, v)
                den = w_inter * jnp.sum(q * n_prev, axis=-1, keepdims=True) + jnp.sum(s, axis=-1, keepdims=True)
                outs.append(num / jnp.maximum(jnp.abs(den), jnp.exp(-mt)))
                b_last = bc[last:last + 1, :]
                m_new = mt[last:last + 1, :]
                a_prev = jnp.exp(b_last + m_prev - m_new)
                ka = k * jnp.exp(b_last - bc + ic - m_new)
                c_ref[r] = a_prev * c_prev + _mm_tn(ka, v)
                n_ref[r] = a_prev * n_prev + jnp.sum(ka, axis=0, keepdims=True)
                m_ref[r] = m_new
            h_ref[d, pl.ds(r0, CHUNK), :] = jnp.concatenate(outs, axis=-1)
        return carry

    lax.fori_loop(0, n_chunks, step, 0)


def _chunk_rows(g_rows, B, T):
    g = jnp.moveaxis(g_rows, 1, 0).reshape(N_GATE, B, T // CHUNK, CHUNK)
    return jnp.transpose(g, (1, 2, 0, 3))


def _mlstm(ml, g_col, g_row, bias_row, bias_col, c0, n0, m0):
    B, T, _ = ml.shape
    nc = T // CHUNK
    per_b = lambda *s: pl.BlockSpec((None,) + s, lambda b: (b,) + (0,) * len(s))
    full = lambda a: pl.BlockSpec(a.shape, lambda b: (0,) * a.ndim)
    st_shapes = [(2 * HEADS, HEAD_DIM, HEAD_DIM), (2 * HEADS, 1, HEAD_DIM), (2 * HEADS, 1, 1)]
    return pl.pallas_call(
        functools.partial(_mlstm_kernel, n_chunks=nc),
        grid=(B,),
        in_specs=[per_b(T, 4 * MIX_SEC), per_b(T, LANES), per_b(nc, N_GATE, CHUNK), full(bias_row), full(bias_col)]
                 + [per_b(*s) for s in st_shapes],
        out_specs=[pl.BlockSpec((2, None, T, MIX_SEC), lambda b: (0, b, 0, 0))] + [per_b(*s) for s in st_shapes],
        out_shape=[jax.ShapeDtypeStruct((2, B, T, MIX_SEC), F32)]
                  + [jax.ShapeDtypeStruct((B,) + s, F32) for s in st_shapes],
        compiler_params=_cparams(("parallel",)),
        name="mlstm",
    )(ml, g_col, g_row, bias_row, bias_col, c0, n0, m0)


def _unit_tri_inverse(a):
    row = lax.broadcasted_iota(jnp.int32, (CHUNK, CHUNK), 0)
    col = lax.broadcasted_iota(jnp.int32, (CHUNK, CHUNK), 1)
    p = jnp.where(row == col, 1.0, 0.0) - a
    x = _mm3(a, a)
    p = p + _mm3(p, x)
    for _ in range(4):
        x = _mm3(x, x)
        p = p + _mm3(p, x)
    return p


def _deltanet_kernel(x_ref, gc_ref, gr_ref, conv_ref, par_row_ref, par_col_ref, seg_ref, s0_ref,
                     o_ref, s_ref, qkv_scr, *, n_chunks):
    T = n_chunks * CHUNK
    s_ref[...] = s0_ref[...]
    incl_lo, incl_up, strict_lo, strict_up = _tri_masks()

    def prep(c, carry):
        r0 = pl.multiple_of(c * CHUNK, CHUNK)
        xc = x_ref[pl.ds(r0, CHUNK), :]
        before = x_ref[pl.ds(pl.multiple_of(jnp.maximum(r0 - 8, 0), 8), 8), :][7:8, :]
        after = x_ref[pl.ds(pl.multiple_of(jnp.minimum(r0 + CHUNK, T - 8), 8), 8), :][0:1, :]
        before = jnp.where(c > 0, before, 0.0)
        after = jnp.where(c < n_chunks - 1, after, 0.0)
        rows = lax.broadcasted_iota(jnp.int32, xc.shape, 0)
        x_prev = jnp.where(rows == 0, before, pltpu.roll(xc, 1, 0))
        x_next = jnp.where(rows == CHUNK - 1, after, pltpu.roll(xc, CHUNK - 1, 0))
        y = _silu(x_prev * conv_ref[0:1, :] + xc * conv_ref[1:2, :] + x_next * conv_ref[2:3, :])
        q = y[:, 0:MIX_SEC]
        k = y[:, MIX_SEC:2 * MIX_SEC]
        q = q * lax.rsqrt(_head_mean_sq(q, seg_ref) * HEAD_DIM + EPS) * (HEAD_DIM ** -0.5)
        k = k * lax.rsqrt(_head_mean_sq(k, seg_ref) * HEAD_DIM + EPS)
        qkv_scr[pl.ds(r0, CHUNK), 0:MIX_SEC] = q
        qkv_scr[pl.ds(r0, CHUNK), MIX_SEC:2 * MIX_SEC] = k
        qkv_scr[pl.ds(r0, CHUNK), 2 * MIX_SEC:3 * MIX_SEC] = y[:, 2 * MIX_SEC:3 * MIX_SEC]
        return carry

    lax.fori_loop(0, n_chunks, prep, 0)

    def step(c, carry):
        for d in range(2):
            cc = c if d == 0 else n_chunks - 1 - c
            r0 = pl.multiple_of(cc * CHUNK, CHUNK)
            x = qkv_scr[pl.ds(r0, CHUNK), :]
            gc = gc_ref[pl.ds(r0, CHUNK), :]
            gr = gr_ref[cc]
            beta_col = jax.nn.sigmoid(gc[:, 16:24])
            la_col = par_row_ref[0:1, :] * _softplus(gc[:, 24:32] + par_row_ref[1:2, :])
            la_row = par_col_ref[:, 0:1] * _softplus(gr[24:32, :] + par_col_ref[:, 1:2])
            g_col, g_row = _gate_cumsums(la_col, la_row, d)
            incl = incl_lo if d == 0 else incl_up
            strict = strict_lo if d == 0 else strict_up
            last = CHUNK - 1 if d == 0 else 0
            outs = []
            for hd in range(HEADS):
                r = d * HEADS + hd
                q = x[:, hd * HEAD_DIM:(hd + 1) * HEAD_DIM]
                k = x[:, MIX_SEC + hd * HEAD_DIM:MIX_SEC + (hd + 1) * HEAD_DIM]
                v = x[:, 2 * MIX_SEC + hd * HEAD_DIM:2 * MIX_SEC + (hd + 1) * HEAD_DIM]
                g = g_col[:, r:r + 1]
                beta = beta_col[:, r:r + 1]
                decay = jnp.exp(jnp.where(incl, g - g_row[r:r + 1, :], -jnp.inf))
                kb = k * beta
                a = jnp.where(strict, _mm_nt(kb, k) * decay, 0.0)
                t_inv = _unit_tri_inverse(a)
                eg = jnp.exp(g)
                u = _mm3(t_inv, v * beta)
                w = _mm3(t_inv, kb * eg)
                s_prev = s_ref[r]
                v_new = u - _mm(w, s_prev)
                attn = _mm_nt(q, k) * decay
                outs.append(_mm(q * eg, s_prev) + _mm(attn, v_new))
                g_last = g[last:last + 1, :]
                s_ref[r] = jnp.exp(g_last) * s_prev + _mm_tn(k * jnp.exp(g_last - g), v_new)
            o_ref[d, pl.ds(r0, CHUNK), :] = jnp.concatenate(outs, axis=-1)
        return carry

    lax.fori_loop(0, n_chunks, step, 0)


def _deltanet(dn_qkv, g_col, g_row, conv_w, par_row, par_col, seg, s0):
    B, T, _ = dn_qkv.shape
    nc = T // CHUNK
    per_b = lambda *s: pl.BlockSpec((None,) + s, lambda b: (b,) + (0,) * len(s))
    full = lambda a: pl.BlockSpec(a.shape, lambda b: (0,) * a.ndim)
    st = (2 * HEADS, HEAD_DIM, HEAD_DIM)
    return pl.pallas_call(
        functools.partial(_deltanet_kernel, n_chunks=nc),
        grid=(B,),
        in_specs=[per_b(T, 3 * MIX_SEC), per_b(T, LANES), per_b(nc, N_GATE, CHUNK), full(conv_w), full(par_row),
                  full(par_col), full(seg), per_b(*st)],
        out_specs=[pl.BlockSpec((2, None, T, MIX_SEC), lambda b: (0, b, 0, 0)), per_b(*st)],
        out_shape=[jax.ShapeDtypeStruct((2, B, T, MIX_SEC), F32), jax.ShapeDtypeStruct((B,) + st, F32)],
        scratch_shapes=[pltpu.VMEM((T, 3 * MIX_SEC), F32)],
        compiler_params=_cparams(("parallel",)),
        name="deltanet",
    )(dn_qkv, g_col, g_row, conv_w, par_row, par_col, seg, s0)


def _fourier_kernel(x_ref, cg_ref, sg_ref, ct_ref, st_ref, o_ref, p_scr, q_scr, *, scale):
    @pl.when(pl.program_id(1) == 0)
    def _():
        x = x_ref[...].astype(BF16)
        p_scr[...] = jnp.dot(x, cg_ref[...], preferred_element_type=F32).astype(BF16)
        q_scr[...] = jnp.dot(x, sg_ref[...], preferred_element_type=F32).astype(BF16)

    y = (jnp.dot(ct_ref[...], p_scr[...], preferred_element_type=F32)
         - jnp.dot(st_ref[...], q_scr[...], preferred_element_type=F32))
    o_ref[...] = y * scale


def _dft_tables(T):
    def tab(n):
        idx = np.arange(n, dtype=np.int64)
        ang = 2.0 * np.pi * ((idx[:, None] * idx[None, :]) % n).astype(np.float64) / n
        return np.cos(ang), np.sin(ang)
    ct, st = tab(T)
    cg1, sg1 = tab(HEAD_DIM)
    eye = np.eye(HEADS)
    cg, sg = np.kron(eye, cg1), np.kron(eye, sg1)
    return tuple(jnp.asarray(a, dtype=F32).astype(BF16) for a in (cg, sg, ct, st))


def _fourier(x):
    B, T, _ = x.shape
    cg, sg, ct, st = _dft_tables(T)
    tr = min(T, 512)
    return pl.pallas_call(
        functools.partial(_fourier_kernel, scale=1.0 / math.sqrt(T * HEAD_DIM)),
        grid=(B, T // tr),
        in_specs=[pl.BlockSpec((None, T, MIX_SEC), lambda b, i: (b, 0, 0)),
                  pl.BlockSpec((MIX_SEC, MIX_SEC), lambda b, i: (0, 0)),
                  pl.BlockSpec((MIX_SEC, MIX_SEC), lambda b, i: (0, 0)),
                  pl.BlockSpec((tr, T), lambda b, i: (i, 0)),
                  pl.BlockSpec((tr, T), lambda b, i: (i, 0))],
        out_specs=pl.BlockSpec((None, tr, MIX_SEC), lambda b, i: (b, i, 0)),
        out_shape=jax.ShapeDtypeStruct((B, T, MIX_SEC), F32),
        scratch_shapes=[pltpu.VMEM((T, MIX_SEC), BF16), pltpu.VMEM((T, MIX_SEC), BF16)],
        compiler_params=_cparams(("parallel", "arbitrary")),
        name="fourier",
    )(x, cg, sg, ct, st)


def _diffattn_kernel(*refs, lam_init, has_ctx):
    if has_ctx:
        q_ref, k_ref, v_ref, kc_ref, vc_ref, lam_ref, o_ref = refs
    else:
        q_ref, k_ref, v_ref, lam_ref, o_ref = refs
    lp = lam_ref[...]
    lam = (jnp.exp(jnp.sum(lp[0:1, :] * lp[1:2, :], axis=-1, keepdims=True))
           - jnp.exp(jnp.sum(lp[2:3, :] * lp[3:4, :], axis=-1, keepdims=True)) + lam_init)
    q = q_ref[...]
    pieces = [(k_ref[...].astype(BF16), v_ref[...].astype(BF16))]
    if has_ctx:
        pieces.append((kc_ref[...].astype(BF16), vc_ref[...].astype(BF16)))
    lane = lax.broadcasted_iota(jnp.int32, q.shape, 1)
    out = jnp.zeros(q.shape, F32)
    for hd in range(HEADS):
        probs = []
        for mp in range(2):
            lo = (hd * 2 + mp) * DA_QK
            qm = jnp.where((lane >= lo) & (lane < lo + DA_QK), q, 0.0).astype(BF16)
            ss = [lax.dot_general(qm, kk, (((1,), (1,)), ((), ())), preferred_element_type=F32) * (DA_QK ** -0.5)
                  for kk, _ in pieces]
            mx = functools.reduce(jnp.maximum, [jnp.max(s, axis=-1, keepdims=True) for s in ss])
            es = [jnp.exp(s - mx) for s in ss]
            den = functools.reduce(jnp.add, [jnp.sum(e, axis=-1, keepdims=True) for e in es])
            probs.append((es, 1.0 / den))
        (e0, r0), (e1, r1) = probs
        r1 = lam * r1
        acc = None
        for j, (_, vv) in enumerate(pieces):
            w = (e0[j] * r0 - e1[j] * r1).astype(BF16)
            part = jnp.dot(w, vv, preferred_element_type=F32)
            acc = part if acc is None else acc + part
        out = jnp.where((lane >= hd * HEAD_DIM) & (lane < (hd + 1) * HEAD_DIM), acc, out)
    o_ref[...] = out


def _diffattn(q, k, v, ctx, lam_par, lam_init):
    B, T, _ = q.shape
    tq = min(T, 256)
    has_ctx = ctx is not None
    in_specs = [pl.BlockSpec((None, tq, MIX_SEC), lambda b, i: (b, i, 0)),
                pl.BlockSpec((None, T, MIX_SEC), lambda b, i: (b, 0, 0)),
                pl.BlockSpec((None, T, MIX_SEC), lambda b, i: (b, 0, 0))]
    args = [q, k, v]
    if has_ctx:
        ck, cv, li = ctx
        P = ck.shape[2]
        in_specs += [pl.BlockSpec((None, None, P, MIX_SEC), lambda b, i: (b, li, 0, 0))] * 2
        args += [ck, cv]
    in_specs.append(pl.BlockSpec(lam_par.shape, lambda b, i: (0, 0)))
    args.append(lam_par)
    return pl.pallas_call(
        functools.partial(_diffattn_kernel, lam_init=lam_init, has_ctx=has_ctx),
        grid=(B, T // tq),
        in_specs=in_specs,
        out_specs=pl.BlockSpec((None, tq, MIX_SEC), lambda b, i: (b, i, 0)),
        out_shape=jax.ShapeDtypeStruct((B, T, MIX_SEC), F32),
        compiler_params=_cparams(("parallel", "arbitrary")),
        name="diff_attention",
    )(*args)


def _mixout_kernel(x_ref, mod_ref, gpost_ref, hml_ref, og_ref, odn_ref, dng_ref, ft_ref, oda_ref,
                   gains_ref, seg_ref, w_ref, o_ref, *, da_scale):
    def head_norm(z, gain_row):
        return z * lax.rsqrt(_head_mean_sq(z, seg_ref) + EPS) * gain_row

    h_ml = (hml_ref[0] + hml_ref[1]) * jax.nn.sigmoid(og_ref[...])
    y_ml = head_norm(h_ml, gains_ref[0:1, :])
    y_dn = head_norm(odn_ref[0] + odn_ref[1], gains_ref[1:2, :]) * _silu(dng_ref[...])
    y_da = head_norm(oda_ref[...], gains_ref[2:3, :]) * da_scale
    y = (_mm(y_ml, w_ref[0:MIX_SEC, :]) + _mm(y_dn, w_ref[MIX_SEC:2 * MIX_SEC, :])
         + _mm(ft_ref[...], w_ref[2 * MIX_SEC:3 * MIX_SEC, :]) + _mm(y_da, w_ref[3 * MIX_SEC:4 * MIX_SEC, :]))
    o_ref[...] = x_ref[...] + mod_ref[5:6, :] * (_rms(y) * gpost_ref[...])


def _mixout(x, mod, g_post, h_ml, ml, o_dn, dn_g, y_ft, o_da, gains, seg, w_out, da_scale):
    B, T, _ = x.shape
    tm = min(T, 512)
    tpb = T // tm
    n_tok = B * T
    x2 = lambda a: a.reshape(n_tok, a.shape[-1])
    h_ml = h_ml.reshape(2, n_tok, MIX_SEC)
    o_dn = o_dn.reshape(2, n_tok, MIX_SEC)
    tok = lambda w, cb=0: pl.BlockSpec((tm, w), lambda i: (i, cb))
    two = pl.BlockSpec((2, tm, MIX_SEC), lambda i: (0, i, 0))
    full = lambda a: pl.BlockSpec(a.shape, lambda i: (0,) * a.ndim)
    out = pl.pallas_call(
        functools.partial(_mixout_kernel, da_scale=da_scale),
        grid=(n_tok // tm,),
        in_specs=[tok(D_MODEL), pl.BlockSpec((None, N_MOD, D_MODEL), lambda i: (i // tpb, 0, 0)), full(g_post),
                  two, tok(MIX_SEC, 3), two, tok(MIX_SEC), tok(MIX_SEC), tok(MIX_SEC),
                  full(gains), full(seg), full(w_out)],
        out_specs=tok(D_MODEL),
        out_shape=jax.ShapeDtypeStruct((n_tok, D_MODEL), F32),
        compiler_params=_cparams(("parallel",)),
        name="mix_out",
    )(x2(x), mod, g_post, h_ml, x2(ml), o_dn, x2(dn_g), x2(y_ft), x2(o_da), gains, seg, w_out)
    return out.reshape(x.shape)


def _rope_tables(T):
    t = np.arange(T)
    row = (t // GRID_W).astype(np.float64)
    col = (t % GRID_W).astype(np.float64)
    inv = ROPE_BASE ** (-np.arange(ROPE_AXIS_PAIRS, dtype=np.float64) / ROPE_AXIS_PAIRS)
    ang = np.concatenate([row[:, None] * inv, col[:, None] * inv], axis=-1)
    cos = np.repeat(np.cos(ang), 2, axis=-1)
    sin = np.repeat(np.sin(ang), 2, axis=-1) * np.tile(np.array([-1.0, 1.0]), DA_QK // 2)
    reps = MIX_SEC // DA_QK
    return jnp.asarray(np.tile(cos, (1, reps)), dtype=F32), jnp.asarray(np.tile(sin, (1, reps)), dtype=F32)


def _layer_params(li, w_in, w_out, ffn_w_in, ffn_w_out, norm_pre, norm_post, ml_i_bias, ml_f_bias, ml_norm,
                  dn_conv, dn_A_log, dn_dt_bias, dn_norm, da_lambda, da_norm):
    w = w_in[li]
    main = jnp.concatenate([w[:, _OFF_ML:_OFF_MLG], w[:, _OFF_DN:_OFF_DNG], w[:, _OFF_FT:IN_WIDTH]], axis=1)
    gate = jnp.concatenate([w[:, _OFF_MLG:_OFF_DN], w[:, _OFF_DNG:_OFF_FT]], axis=1)
    ml_bias = jnp.stack([ml_i_bias[li].reshape(-1), ml_f_bias[li].reshape(-1)])
    dn_par = jnp.stack([-jnp.exp(dn_A_log[li].reshape(-1)), dn_dt_bias[li].reshape(-1)])
    return dict(
        w_main=main.astype(BF16),
        w_gate=jnp.pad(gate, ((0, 0), (0, LANES - N_GATE))).astype(BF16),
        w_gate_t=gate.T.astype(BF16),
        w_out=w_out[li].astype(BF16),
        ffn_w_in=[ffn_w_in[li, j].astype(BF16) for j in range(2)],
        ffn_w_out=[ffn_w_out[li, j].astype(BF16) for j in range(2)],
        norm_pre=[norm_pre[li, j][None, :] for j in range(3)],
        norm_post=[norm_post[li, j][None, :] for j in range(3)],
        ml_bias_row=ml_bias, ml_bias_col=ml_bias.T,
        dn_par_row=dn_par, dn_par_col=dn_par.T,
        dn_conv=dn_conv[li],
        gains=jnp.stack([ml_norm[li].reshape(-1), jnp.tile(dn_norm[li], HEADS), jnp.tile(da_norm[li], HEADS)]),
        da_lambda=da_lambda[li],
    )


def _mixing(x_tok, mod_tok, shape_bt, lp, li, rope_tabs, states, ctx, seg):
    B, T = shape_bt
    lam_init = 0.8 - 0.6 * math.exp(-0.3 * li)
    ml, dn, dn_g, ft, da_q, da_k, da_v, g_col, g_row = _inproj(
        x_tok, mod_tok, lp['norm_pre'][1], lp['w_main'], lp['w_gate'], lp['w_gate_t'], rope_tabs)
    seq = lambda a: a.reshape(B, T, a.shape[-1])
    g_rows = _chunk_rows(g_row, B, T)
    c0, n0, m0, s0 = states
    h_ml, c_f, n_f, m_f = _mlstm(seq(ml), seq(g_col), g_rows, lp['ml_bias_row'], lp['ml_bias_col'], c0, n0, m0)
    o_dn, s_f = _deltanet(seq(dn), seq(g_col), g_rows, lp['dn_conv'], lp['dn_par_row'], lp['dn_par_col'], seg, s0)
    y_ft = _fourier(seq(ft))
    o_da = _diffattn(seq(da_q), seq(da_k), seq(da_v), ctx, lp['da_lambda'], lam_init)
    x_new = _mixout(x_tok, mod_tok, lp['norm_post'][1], h_ml, ml, o_dn, dn_g, y_ft, o_da, lp['gains'], seg,
                    lp['w_out'], 1.0 - lam_init)
    return x_new, (seq(da_k), seq(da_v), c_f, n_f, m_f, s_f)


def _trunk_layer(x_tok, mod_tok, shape_bt, lp, li, rope_tabs, states, ctx, seg):
    x_tok = _ffn_block(x_tok, mod_tok, lp['norm_pre'][0], lp['norm_post'][0], lp['ffn_w_in'][0], lp['ffn_w_out'][0], 0)
    x_tok, aux = _mixing(x_tok, mod_tok, shape_bt, lp, li, rope_tabs, states, ctx, seg)
    x_tok = _ffn_block(x_tok, mod_tok, lp['norm_pre'][2], lp['norm_post'][2], lp['ffn_w_in'][1], lp['ffn_w_out'][1], 6)
    return x_tok, aux


def kernel(x_prompt, x_sample, cache_k, cache_v, state_mlstm_C, state_mlstm_n, state_mlstm_m, state_delta_S, c, c_ctx, w_mod, b_mod, norm_pre, norm_post, ffn_w_in, ffn_w_out, w_in, w_out, ml_i_bias, ml_f_bias, ml_norm, dn_conv, dn_A_log, dn_dt_bias, dn_norm, da_lambda, da_norm):
    Bp, Tp, _ = x_prompt.shape
    Bs, Ts, _ = x_sample.shape
    P = cache_k.shape[2]
    n_st = 2 * HEADS

    cond = jnp.concatenate([c, c_ctx[None, :]], axis=0)
    cond = jnp.pad(cond, ((0, (-cond.shape[0]) % 8), (0, 0)))
    mod = _modulation(cond, w_mod, b_mod)
    rope_tabs = _rope_tables(Ts)
    seg = jnp.asarray(np.kron(np.eye(HEADS), np.ones((HEAD_DIM, HEAD_DIM))), dtype=BF16)
    ck = cache_k.reshape(Bs, DEPTH, P, MIX_SEC)
    cv = cache_v.reshape(Bs, DEPTH, P, MIX_SEC)
    zeros_p = (jnp.zeros((Bp, n_st, HEAD_DIM, HEAD_DIM), F32), jnp.zeros((Bp, n_st, 1, HEAD_DIM), F32),
               jnp.zeros((Bp, n_st, 1, 1), F32), jnp.zeros((Bp, n_st, HEAD_DIM, HEAD_DIM), F32))

    xp = x_prompt.reshape(1, Bp * Tp, D_MODEL)
    xs = x_sample
    ks, vs, Cs, ns, ms, Ss = [], [], [], [], [], []
    for li in range(DEPTH):
        lp = _layer_params(li, w_in, w_out, ffn_w_in, ffn_w_out, norm_pre, norm_post, ml_i_bias, ml_f_bias, ml_norm,
                           dn_conv, dn_A_log, dn_dt_bias, dn_norm, da_lambda, da_norm)
        mod_s = jnp.transpose(mod[li, :, :Bs], (1, 0, 2))
        mod_p = mod[li, :, Bs:Bs + 1].reshape(1, N_MOD, D_MODEL)
        xp, (k_c, v_c, C_c, n_c, m_c, S_c) = _trunk_layer(xp, mod_p, (Bp, Tp), lp, li, None, zeros_p, None, seg)
        ks.append(k_c.reshape(Bp, Tp, HEADS, 2, DA_QK))
        vs.append(v_c.reshape(Bp, Tp, HEADS, HEAD_DIM))
        Cs.append(C_c.reshape(Bp, 2, HEADS, HEAD_DIM, HEAD_DIM))
        ns.append(n_c.reshape(Bp, 2, HEADS, HEAD_DIM))
        ms.append(m_c.reshape(Bp, 2, HEADS))
        Ss.append(S_c.reshape(Bp, 2, HEADS, HEAD_DIM, HEAD_DIM))
        states = (state_mlstm_C[:, li].reshape(Bs, n_st, HEAD_DIM, HEAD_DIM),
                  state_mlstm_n[:, li].reshape(Bs, n_st, 1, HEAD_DIM),
                  state_mlstm_m[:, li].reshape(Bs, n_st, 1, 1),
                  state_delta_S[:, li].reshape(Bs, n_st, HEAD_DIM, HEAD_DIM))
        xs, _ = _trunk_layer(xs, mod_s, (Bs, Ts), lp, li, rope_tabs, states, (ck, cv, li), seg)
    return (xp.reshape(Bp, Tp, D_MODEL), xs, jnp.stack(ks, axis=1), jnp.stack(vs, axis=1), jnp.stack(Cs, axis=1),
            jnp.stack(ns, axis=1), jnp.stack(ms, axis=1), jnp.stack(Ss, axis=1))
```

```python
import functools
import math

import numpy as np
import jax
import jax.numpy as jnp
from jax import lax
from jax.experimental import pallas as pl
from jax.experimental.pallas import tpu as pltpu

D_MODEL = 1024
DEPTH = 2
GRID_W = 64
HEADS = 4
HEAD_DIM = 64
MIX_SEC = HEADS * HEAD_DIM
DA_QK = 32
CONV_K = 3
D_FF = ((8 * D_MODEL // 3 + 127) // 128) * 128
N_MOD = 9
CHUNK = 64
N_GATE = 32
ROPE_AXIS_PAIRS = DA_QK // 4
ROPE_BASE = 10000.0
EPS = 1e-6
HALF = 0.5
F32 = jnp.float32
BF16 = jnp.bfloat16
LANES = 128
VMEM_LIMIT = 56 * 1024 * 1024

_OFF_ML = 0
_OFF_MLG = 4 * MIX_SEC
_OFF_DN = _OFF_MLG + 16
_OFF_DNG = _OFF_DN + 4 * MIX_SEC
_OFF_FT = _OFF_DNG + 16
_OFF_DA = _OFF_FT + MIX_SEC
IN_WIDTH = _OFF_DA + 3 * MIX_SEC
MAIN_WIDTH = 12 * MIX_SEC


def _cparams(sem):
    return pltpu.CompilerParams(dimension_semantics=sem, vmem_limit_bytes=VMEM_LIMIT)


def _mm(a, b):
    return jnp.dot(a.astype(BF16), b.astype(BF16), preferred_element_type=F32)


def _mm_nt(a, b):
    return lax.dot_general(a.astype(BF16), b.astype(BF16), (((1,), (1,)), ((), ())), preferred_element_type=F32)


def _mm_tn(a, b):
    return lax.dot_general(a.astype(BF16), b.astype(BF16), (((0,), (0,)), ((), ())), preferred_element_type=F32)


def _split(a):
    hi = a.astype(BF16)
    lo = (a - hi.astype(F32)).astype(BF16)
    return hi, lo


def _mm3(a, b):
    ah, al = _split(a)
    bh, bl = _split(b)
    dot = functools.partial(jnp.dot, preferred_element_type=F32)
    return dot(ah, bh) + (dot(ah, bl) + dot(al, bh))


def _mm_exact_rhs(a, b_bf16):
    ah, al = _split(a)
    dot = functools.partial(jnp.dot, preferred_element_type=F32)
    return dot(ah, b_bf16) + dot(al, b_bf16)


def _mm_exact_lhs(a_bf16, b):
    bh, bl = _split(b)
    dot = functools.partial(jnp.dot, preferred_element_type=F32)
    return dot(a_bf16, bh) + dot(a_bf16, bl)


def _rms(x):
    return x * lax.rsqrt(jnp.mean(x * x, axis=-1, keepdims=True) + EPS)


def _silu(x):
    return x * jax.nn.sigmoid(x)


def _log_sigmoid(x):
    return jnp.minimum(x, 0.0) - jnp.log1p(jnp.exp(-jnp.abs(x)))


def _softplus(x):
    return jnp.maximum(x, 0.0) + jnp.log1p(jnp.exp(-jnp.abs(x)))


def _head_mean_sq(x, seg_ref):
    return _mm_exact_rhs(x * x, seg_ref[...]) * (1.0 / HEAD_DIM)


def _tri_masks():
    row = lax.broadcasted_iota(jnp.int32, (CHUNK, CHUNK), 0)
    col = lax.broadcasted_iota(jnp.int32, (CHUNK, CHUNK), 1)
    return row >= col, row <= col, row > col, row < col


def _mod_kernel(c_ref, w_ref, b_ref, o_ref):
    c = c_ref[...]
    o_ref[...] = _mm(_silu(c), w_ref[...]) + b_ref[...]


def _modulation(cond, w_mod, b_mod):
    R = cond.shape[0]
    return pl.pallas_call(
        _mod_kernel,
        grid=(DEPTH, N_MOD),
        in_specs=[pl.BlockSpec((R, D_MODEL), lambda l, j: (0, 0)),
                  pl.BlockSpec((None, D_MODEL, D_MODEL), lambda l, j: (l, 0, j)),
                  pl.BlockSpec((None, None, 1, D_MODEL), lambda l, j: (l, j, 0, 0))],
        out_specs=pl.BlockSpec((None, None, R, D_MODEL), lambda l, j: (l, j, 0, 0)),
        out_shape=jax.ShapeDtypeStruct((DEPTH, N_MOD, R, D_MODEL), F32),
        compiler_params=_cparams(("arbitrary", "arbitrary")),
        name="modulation",
    )(cond, w_mod, b_mod.reshape(DEPTH, N_MOD, 1, D_MODEL))


def _ffn_kernel(x_ref, mod_ref, gpre_ref, gpost_ref, wa_ref, wb_ref, wo_ref, o_ref, h_scr, acc_scr, *, m0, n_ff):
    j = pl.program_id(1)

    @pl.when(j == 0)
    def _():
        h = _rms(x_ref[...]) * gpre_ref[...] * (1.0 + mod_ref[m0 + 1:m0 + 2, :]) + mod_ref[m0:m0 + 1, :]
        h_scr[...] = h.astype(BF16)
        acc_scr[...] = jnp.zeros_like(acc_scr)

    h = h_scr[...]
    a = jnp.dot(h, wa_ref[...], preferred_element_type=F32)
    b = jnp.dot(h, wb_ref[...], preferred_element_type=F32)
    acc_scr[...] += _mm(_silu(a) * b, wo_ref[...])

    @pl.when(j == n_ff - 1)
    def _():
        y = _rms(acc_scr[...]) * gpost_ref[...]
        o_ref[...] = x_ref[...] + HALF * mod_ref[m0 + 2:m0 + 3, :] * y


def _ffn_tiles(T):
    tm = min(T, 1024)
    return tm, 256


def _ffn_block(x, mod, g_pre, g_post, w_in, w_out, m0):
    B, T, _ = x.shape
    tm, tf = _ffn_tiles(T)
    tpb, n_ff = T // tm, D_FF // tf
    return pl.pallas_call(
        functools.partial(_ffn_kernel, m0=m0, n_ff=n_ff),
        grid=(B * tpb, n_ff),
        in_specs=[pl.BlockSpec((None, tm, D_MODEL), lambda i, j: (i // tpb, i % tpb, 0)),
                  pl.BlockSpec((None, N_MOD, D_MODEL), lambda i, j: (i // tpb, 0, 0)),
                  pl.BlockSpec((1, D_MODEL), lambda i, j: (0, 0)),
                  pl.BlockSpec((1, D_MODEL), lambda i, j: (0, 0)),
                  pl.BlockSpec((D_MODEL, tf), lambda i, j: (0, j)),
                  pl.BlockSpec((D_MODEL, tf), lambda i, j: (0, j + n_ff)),
                  pl.BlockSpec((tf, D_MODEL), lambda i, j: (j, 0))],
        out_specs=pl.BlockSpec((None, tm, D_MODEL), lambda i, j: (i // tpb, i % tpb, 0)),
        out_shape=jax.ShapeDtypeStruct(x.shape, F32),
        scratch_shapes=[pltpu.VMEM((tm, D_MODEL), BF16), pltpu.VMEM((tm, D_MODEL), F32)],
        compiler_params=_cparams(("parallel", "arbitrary")),
        name="ffn_block",
    )(x, mod, g_pre, g_post, w_in, w_in, w_out)


def _rope(x, cos, sin_signed):
    lane = lax.broadcasted_iota(jnp.int32, x.shape, 1)
    nxt = pltpu.roll(x, x.shape[1] - 1, 1)
    prv = pltpu.roll(x, 1, 1)
    swapped = jnp.where(lane % 2 == 0, nxt, prv)
    return x * cos + swapped * sin_signed


def _inproj_kernel(*refs, rope):
    if rope:
        (x_ref, mod_ref, g_ref, w_ref, wg_ref, wgt_ref, cos_ref, sin_ref,
         ml_ref, dn_ref, dng_ref, ft_ref, q_ref, k_ref, v_ref, gc_ref, gr_ref) = refs
    else:
        (x_ref, mod_ref, g_ref, w_ref, wg_ref, wgt_ref,
         ml_ref, dn_ref, dng_ref, ft_ref, q_ref, k_ref, v_ref, gc_ref, gr_ref) = refs
    h = (_rms(x_ref[...]) * g_ref[...] * (1.0 + mod_ref[4:5, :]) + mod_ref[3:4, :]).astype(BF16)

    def proj(lo, width):
        return jnp.dot(h, w_ref[:, lo:lo + width], preferred_element_type=F32)

    ml_ref[...] = proj(0, 4 * MIX_SEC)
    dn_ref[...] = proj(4 * MIX_SEC, 3 * MIX_SEC)
    dng_ref[...] = proj(7 * MIX_SEC, MIX_SEC)
    ft_ref[...] = proj(8 * MIX_SEC, MIX_SEC)
    q = proj(9 * MIX_SEC, MIX_SEC)
    k = proj(10 * MIX_SEC, MIX_SEC)
    if rope:
        q = _rope(q, cos_ref[...], sin_ref[...])
        k = _rope(k, cos_ref[...], sin_ref[...])
    q_ref[...] = q
    k_ref[...] = k
    v_ref[...] = proj(11 * MIX_SEC, MIX_SEC)
    gc_ref[...] = jnp.dot(h, wg_ref[...], preferred_element_type=F32)
    gr_ref[...] = lax.dot_general(wgt_ref[...], h, (((1,), (1,)), ((), ())), preferred_element_type=F32)


def _inproj(x, mod, g_pre, w_main, w_gate, w_gate_t, rope_tabs):
    B, T, _ = x.shape
    tm = min(T, 512)
    tpb = T // tm
    rope = rope_tabs is not None
    tok = lambda w: pl.BlockSpec((None, tm, w), lambda i: (i // tpb, i % tpb, 0))
    full = lambda a: pl.BlockSpec(a.shape, lambda i: (0,) * a.ndim)
    in_specs = [tok(D_MODEL), pl.BlockSpec((None, N_MOD, D_MODEL), lambda i: (i // tpb, 0, 0)),
                full(g_pre), full(w_main), full(w_gate), full(w_gate_t)]
    args = [x, mod, g_pre, w_main, w_gate, w_gate_t]
    if rope:
        in_specs += [pl.BlockSpec((tm, MIX_SEC), lambda i: (i % tpb, 0))] * 2
        args += list(rope_tabs)
    widths = (4 * MIX_SEC, 3 * MIX_SEC, MIX_SEC, MIX_SEC, MIX_SEC, MIX_SEC, MIX_SEC, LANES)
    out_specs = [tok(w) for w in widths] + [pl.BlockSpec((None, N_GATE, tm), lambda i: (i // tpb, 0, i % tpb))]
    out_shape = [jax.ShapeDtypeStruct((B, T, w), F32) for w in widths] + [jax.ShapeDtypeStruct((B, N_GATE, T), F32)]
    return pl.pallas_call(
        functools.partial(_inproj_kernel, rope=rope),
        grid=(B * tpb,),
        in_specs=in_specs, out_specs=out_specs, out_shape=out_shape,
        compiler_params=_cparams(("parallel",)),
        name="in_proj",
    )(*args)


def _gate_cumsums(f_col, f_row, d):
    incl_lo, incl_up, _, _ = _tri_masks()
    lo = incl_lo.astype(BF16)
    up = incl_up.astype(BF16)
    tri, tri_t = (lo, up) if d == 0 else (up, lo)
    hi, mid, low = _split3(f_col)
    dot = functools.partial(jnp.dot, preferred_element_type=F32)
    b_col = dot(tri, hi) + (dot(tri, mid) + dot(tri, low))
    hi, mid, low = _split3(f_row)
    b_row = dot(hi, tri_t) + (dot(mid, tri_t) + dot(low, tri_t))
    return b_col, b_row


def _split3(a):
    hi = a.astype(BF16)
    r = a - hi.astype(F32)
    mid = r.astype(BF16)
    low = (r - mid.astype(F32)).astype(BF16)
    return hi, mid, low


def _mlstm_kernel(ml_ref, gc_ref, gr_ref, bias_row_ref, bias_col_ref, c0_ref, n0_ref, m0_ref,
                  h_ref, c_ref, n_ref, m_ref, *, n_chunks):
    c_ref[...] = c0_ref[...]
    n_ref[...] = n0_ref[...]
    m_ref[...] = m0_ref[...]
    incl_lo, incl_up, _, _ = _tri_masks()

    def step(c, carry):
        for d in range(2):
            cc = c if d == 0 else n_chunks - 1 - c
            r0 = pl.multiple_of(cc * CHUNK, CHUNK)
            x = ml_ref[pl.ds(r0, CHUNK), :]
            gc = gc_ref[pl.ds(r0, CHUNK), :]
            gr = gr_ref[cc]
            i_col = gc[:, 0:8] + bias_row_ref[0:1, :]
            f_col = _log_sigmoid(gc[:, 8:16] + bias_row_ref[1:2, :])
            i_row = gr[0:8, :] + bias_col_ref[:, 0:1]
            f_row = _log_sigmoid(gr[8:16, :] + bias_col_ref[:, 1:2])
            b_col, b_row = _gate_cumsums(f_col, f_row, d)
            mask = incl_lo if d == 0 else incl_up
            last = CHUNK - 1 if d == 0 else 0
            outs = []
            for hd in range(HEADS):
                r = d * HEADS + hd
                sl = slice(hd * HEAD_DIM, (hd + 1) * HEAD_DIM)
                q = x[:, sl]
                k = x[:, MIX_SEC + hd * HEAD_DIM:MIX_SEC + (hd + 1) * HEAD_DIM] * (HEAD_DIM ** -0.5)
                v = x[:, 2 * MIX_SEC + hd * HEAD_DIM:2 * MIX_SEC + (hd + 1) * HEAD_DIM]
                bc = b_col[:, r:r + 1]
                ic = i_col[:, r:r + 1]
                row_term = i_row[r:r + 1, :] - b_row[r:r + 1, :]
                m_prev = m_ref[r]
                inter = bc + m_prev
                dmat = jnp.where(mask, bc + row_term, -jnp.inf)
                mt = jnp.maximum(inter, jnp.max(dmat, axis=-1, keepdims=True))
                w_inter = jnp.exp(inter - mt)
                s = _mm_nt(q, k) * jnp.exp(dmat - mt)
                c_prev = c_ref[r]
                n_prev = n_ref[r]
                num = w_inter * _mm(q, c_prev) + _mm(s, v)
                den = w_inter * jnp.sum(q * n_prev, axis=-1, keepdims=True) + jnp.sum(s, axis=-1, keepdims=True)
                outs.append(num / jnp.maximum(jnp.abs(den), jnp.exp(-mt)))
                b_last = bc[last:last + 1, :]
                m_new = mt[last:last + 1, :]
                a_prev = jnp.exp(b_last + m_prev - m_new)
                ka = k * jnp.exp(b_last - bc + ic - m_new)
                c_ref[r] = a_prev * c_prev + _mm_tn(ka, v)
                n_ref[r] = a_prev * n_prev + jnp.sum(ka, axis=0, keepdims=True)
                m_ref[r] = m_new
            h_ref[d, pl.ds(r0, CHUNK), :] = jnp.concatenate(outs, axis=-1)
        return carry

    lax.fori_loop(0, n_chunks, step, 0)


def _chunk_rows(g_rows, B, T):
    g = jnp.moveaxis(g_rows, 1, 0).reshape(N_GATE, B, T // CHUNK, CHUNK)
    return jnp.transpose(g, (1, 2, 0, 3))


def _mlstm(ml, g_col, g_row, bias_row, bias_col, c0, n0, m0):
    B, T, _ = ml.shape
    nc = T // CHUNK
    per_b = lambda *s: pl.BlockSpec((None,) + s, lambda b: (b,) + (0,) * len(s))
    full = lambda a: pl.BlockSpec(a.shape, lambda b: (0,) * a.ndim)
    st_shapes = [(2 * HEADS, HEAD_DIM, HEAD_DIM), (2 * HEADS, 1, HEAD_DIM), (2 * HEADS, 1, 1)]
    return pl.pallas_call(
        functools.partial(_mlstm_kernel, n_chunks=nc),
        grid=(B,),
        in_specs=[per_b(T, 4 * MIX_SEC), per_b(T, LANES), per_b(nc, N_GATE, CHUNK), full(bias_row), full(bias_col)]
                 + [per_b(*s) for s in st_shapes],
        out_specs=[pl.BlockSpec((2, None, T, MIX_SEC), lambda b: (0, b, 0, 0))] + [per_b(*s) for s in st_shapes],
        out_shape=[jax.ShapeDtypeStruct((2, B, T, MIX_SEC), F32)]
                  + [jax.ShapeDtypeStruct((B,) + s, F32) for s in st_shapes],
        compiler_params=_cparams(("parallel",)),
        name="mlstm",
    )(ml, g_col, g_row, bias_row, bias_col, c0, n0, m0)


def _exact3(a, b_bf16):
    hi, mid, low = _split3(a)
    dot = functools.partial(jnp.dot, preferred_element_type=F32)
    return dot(hi, b_bf16) + (dot(mid, b_bf16) + dot(low, b_bf16))


def _exact3_lhs(a_bf16, b):
    hi, mid, low = _split3(b)
    dot = functools.partial(jnp.dot, preferred_element_type=F32)
    return dot(a_bf16, hi) + (dot(a_bf16, mid) + dot(a_bf16, low))


def _block_diag(x, seg):
    return jnp.concatenate([x] * HEADS, axis=0) * seg


def _block_diag_pair(pair, seg):
    return _block_diag(pair[0], seg), _block_diag(pair[1], seg)


def _packed_mm3(l_pair, w_pair):
    (lh, ll), (wh, wl) = l_pair, w_pair
    dot = functools.partial(jnp.dot, preferred_element_type=F32)
    top = dot(jnp.concatenate([lh, ll], axis=0), wh)
    return top[0:CHUNK] + (top[CHUNK:2 * CHUNK] + dot(lh, wl))


def _packed_tri_inverses(a_list, eye, seg):
    a_pairs = [_split(a) for a in a_list]
    ps = [eye - a for a in a_list]
    xs = [_packed_mm3(ap, _block_diag_pair(ap, seg)) for ap in a_pairs]
    for i in range(5):
        x_pairs = [_split(x) for x in xs]
        wxs = [_block_diag_pair(xp, seg) for xp in x_pairs]
        ps = [p + _packed_mm3(_split(p), wx) for p, wx in zip(ps, wxs)]
        if i < 4:
            xs = [_packed_mm3(xp, wx) for xp, wx in zip(x_pairs, wxs)]
    return ps


def _deltanet_kernel(x_ref, gc_ref, grp_ref, conv_ref, par_row_ref, par_pk_ref, seg_ref, segf_ref, sel_ref, tri_ref,
                     s0_ref, o_ref, s_ref, u_scr, w_scr, qg_scr, attn_scr, kgt_scr, egl_scr, sbd_scr, *, n_chunks):
    T = n_chunks * CHUNK
    seg = seg_ref[...]
    row = lax.broadcasted_iota(jnp.int32, (CHUNK, MIX_SEC), 0)
    pos = lax.broadcasted_iota(jnp.int32, (CHUNK, MIX_SEC), 1) % CHUNK
    incl = (pos <= row, pos >= row)
    strict = (pos < row, pos > row)
    eye = jnp.where(pos == row, 1.0, 0.0)
    incl_lo, incl_up, _, _ = _tri_masks()
    tri64 = (incl_lo.astype(BF16), incl_up.astype(BF16))

    zero = jnp.zeros((HEAD_DIM, HEAD_DIM), F32)
    for d in range(2):
        for hd in range(HEADS):
            blocks = [zero] * hd + [s0_ref[d * HEADS + hd]] + [zero] * (HEADS - 1 - hd)
            sbd_scr[d, hd * HEAD_DIM:(hd + 1) * HEAD_DIM, :] = jnp.concatenate(blocks, axis=-1)

    def front(c):
        r0 = pl.multiple_of(c * CHUNK, CHUNK)
        xc = x_ref[pl.ds(r0, CHUNK), :]
        before = x_ref[pl.ds(pl.multiple_of(jnp.maximum(r0 - 8, 0), 8), 8), :][7:8, :]
        after = x_ref[pl.ds(pl.multiple_of(jnp.minimum(r0 + CHUNK, T - 8), 8), 8), :][0:1, :]
        before = jnp.where(c > 0, before, 0.0)
        after = jnp.where(c < n_chunks - 1, after, 0.0)
        rows = lax.broadcasted_iota(jnp.int32, xc.shape, 0)
        x_prev = jnp.where(rows == 0, before, pltpu.roll(xc, 1, 0))
        x_next = jnp.where(rows == CHUNK - 1, after, pltpu.roll(xc, CHUNK - 1, 0))
        y = _silu(x_prev * conv_ref[0:1, :] + xc * conv_ref[1:2, :] + x_next * conv_ref[2:3, :])
        q = y[:, 0:MIX_SEC]
        k = y[:, MIX_SEC:2 * MIX_SEC]
        v = y[:, 2 * MIX_SEC:3 * MIX_SEC]
        q = q * lax.rsqrt(_head_mean_sq(q, seg_ref) * HEAD_DIM + EPS) * (HEAD_DIM ** -0.5)
        k = k * lax.rsqrt(_head_mean_sq(k, seg_ref) * HEAD_DIM + EPS)
        k16 = k.astype(BF16)
        kq = lax.dot_general(jnp.concatenate([k16, q.astype(BF16)], axis=0), _block_diag(k16, seg),
                             (((1,), (1,)), ((), ())), preferred_element_type=F32)
        kk, qk = kq[0:CHUNK], kq[CHUNK:2 * CHUNK]
        gcol = gc_ref[pl.ds(r0, CHUNK), :]
        beta128 = jax.nn.sigmoid(gcol)
        la128 = par_row_ref[0:1, :] * _softplus(gcol + par_row_ref[1:2, :])
        grp = grp_ref[c]
        jobs = []
        for d in range(2):
            g_bc = _exact3(_exact3_lhs(tri64[d], la128), sel_ref[d])
            beta_bc = _exact3(beta128, sel_ref[2 + d])
            la_rows = par_pk_ref[d:d + 1, :] * _softplus(grp + par_pk_ref[2 + d:3 + d, :])
            g_row = _exact3(la_rows, tri_ref[d])[6 + d:7 + d, :]
            decay = jnp.exp(jnp.where(incl[d], g_bc - g_row, -jnp.inf))
            a = jnp.where(strict[d], beta_bc * kk * decay, 0.0)
            eg = jnp.exp(g_bc)
            last = CHUNK - 1 if d == 0 else 0
            g_last = g_bc[last:last + 1, :]
            idx = d * n_chunks + c
            qg_scr[idx] = (q * eg).astype(BF16)
            attn_scr[idx] = (qk * decay).astype(BF16)
            kgt_scr[idx] = (k * jnp.exp(g_last - g_bc)).T.astype(BF16)
            egl_scr[idx] = jnp.broadcast_to(jnp.exp(g_last), (8, MIX_SEC))
            jobs.append((idx, a, _split(v * beta_bc), _split(k * beta_bc * eg)))
        return jobs

    def prep_pair(i, carry):
        jobs = front(2 * i) + front(2 * i + 1)
        t_pairs = [_split(t) for t in _packed_tri_inverses([job[1] for job in jobs], eye, seg)]
        us = [_packed_mm3(tp, _block_diag_pair(job[2], seg)) for tp, job in zip(t_pairs, jobs)]
        ws = [_packed_mm3(tp, _block_diag_pair(job[3], seg)) for tp, job in zip(t_pairs, jobs)]
        for job, u, w in zip(jobs, us, ws):
            u_scr[job[0]] = u
            w_scr[job[0]] = w.astype(BF16)
        return carry

    lax.fori_loop(0, n_chunks // 2, prep_pair, 0)

    def step(c, carry):
        ccs = (c, n_chunks - 1 - c)
        idxs = [d * n_chunks + ccs[d] for d in range(2)]
        s_prev = [sbd_scr[d] for d in range(2)]
        rs = [jnp.dot(jnp.concatenate([w_scr[idxs[d]], qg_scr[idxs[d]]], axis=0), s_prev[d].astype(BF16),
                      preferred_element_type=F32) for d in range(2)]
        v16 = [(u_scr[idxs[d]] - rs[d][0:CHUNK]).astype(BF16) for d in range(2)]
        outs = [rs[d][CHUNK:2 * CHUNK] + jnp.dot(attn_scr[idxs[d]], _block_diag(v16[d], seg),
                                                 preferred_element_type=F32) for d in range(2)]
        upds = [jnp.dot(kgt_scr[idxs[d]], v16[d], preferred_element_type=F32) for d in range(2)]
        for d in range(2):
            o_ref[d, pl.ds(pl.multiple_of(ccs[d] * CHUNK, CHUNK), CHUNK), :] = outs[d]
            sbd_scr[d] = egl_scr[idxs[d]][0:1, :] * s_prev[d] + upds[d] * segf_ref[...]
        return carry

    lax.fori_loop(0, n_chunks, step, 0)

    for d in range(2):
        for hd in range(HEADS):
            sl = slice(hd * HEAD_DIM, (hd + 1) * HEAD_DIM)
            s_ref[d * HEADS + hd] = sbd_scr[d, sl, sl]


def _deltanet(dn_qkv, g_col, g_rows_pk, conv_w, par_row, par_pk, seg, segf, sel, tri_pk, s0):
    B, T, _ = dn_qkv.shape
    nc = T // CHUNK
    per_b = lambda *s: pl.BlockSpec((None,) + s, lambda b: (b,) + (0,) * len(s))
    full = lambda a: pl.BlockSpec(a.shape, lambda b: (0,) * a.ndim)
    st = (2 * HEADS, HEAD_DIM, HEAD_DIM)
    per_chunk = lambda rows, cols, dt: pltpu.VMEM((2 * nc, rows, cols), dt)
    return pl.pallas_call(
        functools.partial(_deltanet_kernel, n_chunks=nc),
        grid=(B,),
        in_specs=[per_b(T, 3 * MIX_SEC), per_b(T, LANES), per_b(nc, 8, MIX_SEC), full(conv_w), full(par_row),
                  full(par_pk), full(seg), full(segf), full(sel), full(tri_pk), per_b(*st)],
        out_specs=[pl.BlockSpec((2, None, T, MIX_SEC), lambda b: (0, b, 0, 0)), per_b(*st)],
        out_shape=[jax.ShapeDtypeStruct((2, B, T, MIX_SEC), F32), jax.ShapeDtypeStruct((B,) + st, F32)],
        scratch_shapes=[per_chunk(CHUNK, MIX_SEC, F32), per_chunk(CHUNK, MIX_SEC, BF16),
                        per_chunk(CHUNK, MIX_SEC, BF16), per_chunk(CHUNK, MIX_SEC, BF16),
                        per_chunk(MIX_SEC, CHUNK, BF16), per_chunk(8, MIX_SEC, F32),
                        pltpu.VMEM((2, MIX_SEC, MIX_SEC), F32)],
        compiler_params=_cparams(("parallel",)),
        name="deltanet",
    )(dn_qkv, g_col, g_rows_pk, conv_w, par_row, par_pk, seg, segf, sel, tri_pk, s0)


def _fourier_kernel(x_ref, cg_ref, sg_ref, ct_ref, st_ref, o_ref, p_scr, q_scr, *, scale):
    @pl.when(pl.program_id(1) == 0)
    def _():
        x = x_ref[...].astype(BF16)
        p_scr[...] = jnp.dot(x, cg_ref[...], preferred_element_type=F32).astype(BF16)
        q_scr[...] = jnp.dot(x, sg_ref[...], preferred_element_type=F32).astype(BF16)

    y = (jnp.dot(ct_ref[...], p_scr[...], preferred_element_type=F32)
         - jnp.dot(st_ref[...], q_scr[...], preferred_element_type=F32))
    o_ref[...] = y * scale


def _dft_tables(T):
    def tab(n):
        idx = np.arange(n, dtype=np.int64)
        ang = 2.0 * np.pi * ((idx[:, None] * idx[None, :]) % n).astype(np.float64) / n
        return np.cos(ang), np.sin(ang)
    ct, st = tab(T)
    cg1, sg1 = tab(HEAD_DIM)
    eye = np.eye(HEADS)
    cg, sg = np.kron(eye, cg1), np.kron(eye, sg1)
    return tuple(jnp.asarray(a, dtype=F32).astype(BF16) for a in (cg, sg, ct, st))


def _fourier(x):
    B, T, _ = x.shape
    cg, sg, ct, st = _dft_tables(T)
    tr = min(T, 512)
    return pl.pallas_call(
        functools.partial(_fourier_kernel, scale=1.0 / math.sqrt(T * HEAD_DIM)),
        grid=(B, T // tr),
        in_specs=[pl.BlockSpec((None, T, MIX_SEC), lambda b, i: (b, 0, 0)),
                  pl.BlockSpec((MIX_SEC, MIX_SEC), lambda b, i: (0, 0)),
                  pl.BlockSpec((MIX_SEC, MIX_SEC), lambda b, i: (0, 0)),
                  pl.BlockSpec((tr, T), lambda b, i: (i, 0)),
                  pl.BlockSpec((tr, T), lambda b, i: (i, 0))],
        out_specs=pl.BlockSpec((None, tr, MIX_SEC), lambda b, i: (b, i, 0)),
        out_shape=jax.ShapeDtypeStruct((B, T, MIX_SEC), F32),
        scratch_shapes=[pltpu.VMEM((T, MIX_SEC), BF16), pltpu.VMEM((T, MIX_SEC), BF16)],
        compiler_params=_cparams(("parallel", "arbitrary")),
        name="fourier",
    )(x, cg, sg, ct, st)


def _diffattn_kernel(*refs, lam_init, has_ctx):
    if has_ctx:
        q_ref, k_ref, v_ref, kc_ref, vc_ref, lam_ref, o_ref = refs
    else:
        q_ref, k_ref, v_ref, lam_ref, o_ref = refs
    lp = lam_ref[...]
    lam = (jnp.exp(jnp.sum(lp[0:1, :] * lp[1:2, :], axis=-1, keepdims=True))
           - jnp.exp(jnp.sum(lp[2:3, :] * lp[3:4, :], axis=-1, keepdims=True)) + lam_init)
    q = q_ref[...]
    pieces = [(k_ref[...].astype(BF16), v_ref[...].astype(BF16))]
    if has_ctx:
        pieces.append((kc_ref[...].astype(BF16), vc_ref[...].astype(BF16)))
    lane = lax.broadcasted_iota(jnp.int32, q.shape, 1)
    out = jnp.zeros(q.shape, F32)
    for hd in range(HEADS):
        probs = []
        for mp in range(2):
            lo = (hd * 2 + mp) * DA_QK
            qm = jnp.where((lane >= lo) & (lane < lo + DA_QK), q, 0.0).astype(BF16)
            ss = [lax.dot_general(qm, kk, (((1,), (1,)), ((), ())), preferred_element_type=F32) * (DA_QK ** -0.5)
                  for kk, _ in pieces]
            mx = functools.reduce(jnp.maximum, [jnp.max(s, axis=-1, keepdims=True) for s in ss])
            es = [jnp.exp(s - mx) for s in ss]
            den = functools.reduce(jnp.add, [jnp.sum(e, axis=-1, keepdims=True) for e in es])
            probs.append((es, 1.0 / den))
        (e0, r0), (e1, r1) = probs
        r1 = lam * r1
        acc = None
        for j, (_, vv) in enumerate(pieces):
            w = (e0[j] * r0 - e1[j] * r1).astype(BF16)
            part = jnp.dot(w, vv, preferred_element_type=F32)
            acc = part if acc is None else acc + part
        out = jnp.where((lane >= hd * HEAD_DIM) & (lane < (hd + 1) * HEAD_DIM), acc, out)
    o_ref[...] = out


def _diffattn(q, k, v, ctx, lam_par, lam_init):
    B, T, _ = q.shape
    tq = min(T, 256)
    has_ctx = ctx is not None
    in_specs = [pl.BlockSpec((None, tq, MIX_SEC), lambda b, i: (b, i, 0)),
                pl.BlockSpec((None, T, MIX_SEC), lambda b, i: (b, 0, 0)),
                pl.BlockSpec((None, T, MIX_SEC), lambda b, i: (b, 0, 0))]
    args = [q, k, v]
    if has_ctx:
        ck, cv, li = ctx
        P = ck.shape[2]
        in_specs += [pl.BlockSpec((None, None, P, MIX_SEC), lambda b, i: (b, li, 0, 0))] * 2
        args += [ck, cv]
    in_specs.append(pl.BlockSpec(lam_par.shape, lambda b, i: (0, 0)))
    args.append(lam_par)
    return pl.pallas_call(
        functools.partial(_diffattn_kernel, lam_init=lam_init, has_ctx=has_ctx),
        grid=(B, T // tq),
        in_specs=in_specs,
        out_specs=pl.BlockSpec((None, tq, MIX_SEC), lambda b, i: (b, i, 0)),
        out_shape=jax.ShapeDtypeStruct((B, T, MIX_SEC), F32),
        compiler_params=_cparams(("parallel", "arbitrary")),
        name="diff_attention",
    )(*args)


def _mixout_kernel(x_ref, mod_ref, gpost_ref, hml_ref, og_ref, odn_ref, dng_ref, ft_ref, oda_ref,
                   gains_ref, seg_ref, w_ref, o_ref, *, da_scale):
    def head_norm(z, gain_row):
        return z * lax.rsqrt(_head_mean_sq(z, seg_ref) + EPS) * gain_row

    h_ml = (hml_ref[0] + hml_ref[1]) * jax.nn.sigmoid(og_ref[...])
    y_ml = head_norm(h_ml, gains_ref[0:1, :])
    y_dn = head_norm(odn_ref[0] + odn_ref[1], gains_ref[1:2, :]) * _silu(dng_ref[...])
    y_da = head_norm(oda_ref[...], gains_ref[2:3, :]) * da_scale
    y = (_mm(y_ml, w_ref[0:MIX_SEC, :]) + _mm(y_dn, w_ref[MIX_SEC:2 * MIX_SEC, :])
         + _mm(ft_ref[...], w_ref[2 * MIX_SEC:3 * MIX_SEC, :]) + _mm(y_da, w_ref[3 * MIX_SEC:4 * MIX_SEC, :]))
    o_ref[...] = x_ref[...] + mod_ref[5:6, :] * (_rms(y) * gpost_ref[...])


def _mixout(x, mod, g_post, h_ml, ml, o_dn, dn_g, y_ft, o_da, gains, seg, w_out, da_scale):
    B, T, _ = x.shape
    tm = min(T, 512)
    tpb = T // tm
    n_tok = B * T
    x2 = lambda a: a.reshape(n_tok, a.shape[-1])
    h_ml = h_ml.reshape(2, n_tok, MIX_SEC)
    o_dn = o_dn.reshape(2, n_tok, MIX_SEC)
    tok = lambda w, cb=0: pl.BlockSpec((tm, w), lambda i: (i, cb))
    two = pl.BlockSpec((2, tm, MIX_SEC), lambda i: (0, i, 0))
    full = lambda a: pl.BlockSpec(a.shape, lambda i: (0,) * a.ndim)
    out = pl.pallas_call(
        functools.partial(_mixout_kernel, da_scale=da_scale),
        grid=(n_tok // tm,),
        in_specs=[tok(D_MODEL), pl.BlockSpec((None, N_MOD, D_MODEL), lambda i: (i // tpb, 0, 0)), full(g_post),
                  two, tok(MIX_SEC, 3), two, tok(MIX_SEC), tok(MIX_SEC), tok(MIX_SEC),
                  full(gains), full(seg), full(w_out)],
        out_specs=tok(D_MODEL),
        out_shape=jax.ShapeDtypeStruct((n_tok, D_MODEL), F32),
        compiler_params=_cparams(("parallel",)),
        name="mix_out",
    )(x2(x), mod, g_post, h_ml, x2(ml), o_dn, x2(dn_g), x2(y_ft), x2(o_da), gains, seg, w_out)
    return out.reshape(x.shape)


def _rope_tables(T):
    t = np.arange(T)
    row = (t // GRID_W).astype(np.float64)
    col = (t % GRID_W).astype(np.float64)
    inv = ROPE_BASE ** (-np.arange(ROPE_AXIS_PAIRS, dtype=np.float64) / ROPE_AXIS_PAIRS)
    ang = np.concatenate([row[:, None] * inv, col[:, None] * inv], axis=-1)
    cos = np.repeat(np.cos(ang), 2, axis=-1)
    sin = np.repeat(np.sin(ang), 2, axis=-1) * np.tile(np.array([-1.0, 1.0]), DA_QK // 2)
    reps = MIX_SEC // DA_QK
    return jnp.asarray(np.tile(cos, (1, reps)), dtype=F32), jnp.asarray(np.tile(sin, (1, reps)), dtype=F32)


def _layer_params(li, w_in, w_out, ffn_w_in, ffn_w_out, norm_pre, norm_post, ml_i_bias, ml_f_bias, ml_norm,
                  dn_conv, dn_A_log, dn_dt_bias, dn_norm, da_lambda, da_norm):
    w = w_in[li]
    main = jnp.concatenate([w[:, _OFF_ML:_OFF_MLG], w[:, _OFF_DN:_OFF_DNG], w[:, _OFF_FT:IN_WIDTH]], axis=1)
    gate = jnp.concatenate([w[:, _OFF_MLG:_OFF_DN], w[:, _OFF_DNG:_OFF_FT]], axis=1)
    ml_bias = jnp.stack([ml_i_bias[li].reshape(-1), ml_f_bias[li].reshape(-1)])
    dn_par = jnp.stack([-jnp.exp(dn_A_log[li].reshape(-1)), dn_dt_bias[li].reshape(-1)])
    dn_par_row = jnp.pad(dn_par, ((0, 0), (24, LANES - N_GATE)))
    dn_par_pk = jnp.repeat(dn_par.reshape(4, HEADS), HEAD_DIM, axis=1)
    return dict(
        w_main=main.astype(BF16),
        w_gate=jnp.pad(gate, ((0, 0), (0, LANES - N_GATE))).astype(BF16),
        w_gate_t=gate.T.astype(BF16),
        w_out=w_out[li].astype(BF16),
        ffn_w_in=[ffn_w_in[li, j].astype(BF16) for j in range(2)],
        ffn_w_out=[ffn_w_out[li, j].astype(BF16) for j in range(2)],
        norm_pre=[norm_pre[li, j][None, :] for j in range(3)],
        norm_post=[norm_post[li, j][None, :] for j in range(3)],
        ml_bias_row=ml_bias, ml_bias_col=ml_bias.T,
        dn_par_row=dn_par_row, dn_par_pk=dn_par_pk,
        dn_conv=dn_conv[li],
        gains=jnp.stack([ml_norm[li].reshape(-1), jnp.tile(dn_norm[li], HEADS), jnp.tile(da_norm[li], HEADS)]),
        da_lambda=da_lambda[li],
    )


def _packed_consts():
    seg = np.kron(np.eye(HEADS), np.ones((HEAD_DIM, HEAD_DIM)))
    sel = np.zeros((4, LANES, MIX_SEC))
    for d in range(2):
        for hd in range(HEADS):
            sel[d, 24 + d * HEADS + hd, hd * HEAD_DIM:(hd + 1) * HEAD_DIM] = 1.0
            sel[2 + d, 16 + d * HEADS + hd, hd * HEAD_DIM:(hd + 1) * HEAD_DIM] = 1.0
    s_idx = np.arange(CHUNK)
    upper = (s_idx[:, None] <= s_idx[None, :]).astype(np.float64)
    tri = np.stack([np.kron(np.eye(HEADS), upper), np.kron(np.eye(HEADS), upper.T)])
    return dict(seg=jnp.asarray(seg, dtype=BF16), segf=jnp.asarray(seg, dtype=F32),
                sel=jnp.asarray(sel, dtype=BF16), tri=jnp.asarray(tri, dtype=BF16))


def _chunk_rows_packed(g_rows, B, T):
    g = jnp.moveaxis(g_rows, 1, 0).reshape(4, 2, HEADS, B, T // CHUNK, CHUNK)
    return jnp.transpose(g, (3, 4, 0, 1, 2, 5)).reshape(B, T // CHUNK, 8, MIX_SEC)


def _mixing(x_tok, mod_tok, shape_bt, lp, li, rope_tabs, states, ctx, consts):
    B, T = shape_bt
    seg = consts['seg']
    lam_init = 0.8 - 0.6 * math.exp(-0.3 * li)
    ml, dn, dn_g, ft, da_q, da_k, da_v, g_col, g_row = _inproj(
        x_tok, mod_tok, lp['norm_pre'][1], lp['w_main'], lp['w_gate'], lp['w_gate_t'], rope_tabs)
    seq = lambda a: a.reshape(B, T, a.shape[-1])
    g_rows = _chunk_rows(g_row, B, T)
    g_rows_pk = _chunk_rows_packed(g_row, B, T)
    c0, n0, m0, s0 = states
    h_ml, c_f, n_f, m_f = _mlstm(seq(ml), seq(g_col), g_rows, lp['ml_bias_row'], lp['ml_bias_col'], c0, n0, m0)
    o_dn, s_f = _deltanet(seq(dn), seq(g_col), g_rows_pk, lp['dn_conv'], lp['dn_par_row'], lp['dn_par_pk'], seg,
                          consts['segf'], consts['sel'], consts['tri'], s0)
    y_ft = _fourier(seq(ft))
    o_da = _diffattn(seq(da_q), seq(da_k), seq(da_v), ctx, lp['da_lambda'], lam_init)
    x_new = _mixout(x_tok, mod_tok, lp['norm_post'][1], h_ml, ml, o_dn, dn_g, y_ft, o_da, lp['gains'], seg,
                    lp['w_out'], 1.0 - lam_init)
    return x_new, (seq(da_k), seq(da_v), c_f, n_f, m_f, s_f)


def _trunk_layer(x_tok, mod_tok, shape_bt, lp, li, rope_tabs, states, ctx, seg):
    x_tok = _ffn_block(x_tok, mod_tok, lp['norm_pre'][0], lp['norm_post'][0], lp['ffn_w_in'][0], lp['ffn_w_out'][0], 0)
    x_tok, aux = _mixing(x_tok, mod_tok, shape_bt, lp, li, rope_tabs, states, ctx, seg)
    x_tok = _ffn_block(x_tok, mod_tok, lp['norm_pre'][2], lp['norm_post'][2], lp['ffn_w_in'][1], lp['ffn_w_out'][1], 6)
    return x_tok, aux


def kernel(x_prompt, x_sample, cache_k, cache_v, state_mlstm_C, state_mlstm_n, state_mlstm_m, state_delta_S, c, c_ctx, w_mod, b_mod, norm_pre, norm_post, ffn_w_in, ffn_w_out, w_in, w_out, ml_i_bias, ml_f_bias, ml_norm, dn_conv, dn_A_log, dn_dt_bias, dn_norm, da_lambda, da_norm):
    Bp, Tp, _ = x_prompt.shape
    Bs, Ts, _ = x_sample.shape
    P = cache_k.shape[2]
    n_st = 2 * HEADS

    cond = jnp.concatenate([c, c_ctx[None, :]], axis=0)
    cond = jnp.pad(cond, ((0, (-cond.shape[0]) % 8), (0, 0)))
    mod = _modulation(cond, w_mod, b_mod)
    rope_tabs = _rope_tables(Ts)
    seg = _packed_consts()
    ck = cache_k.reshape(Bs, DEPTH, P, MIX_SEC)
    cv = cache_v.reshape(Bs, DEPTH, P, MIX_SEC)
    zeros_p = (jnp.zeros((Bp, n_st, HEAD_DIM, HEAD_DIM), F32), jnp.zeros((Bp, n_st, 1, HEAD_DIM), F32),
               jnp.zeros((Bp, n_st, 1, 1), F32), jnp.zeros((Bp, n_st, HEAD_DIM, HEAD_DIM), F32))

    xp = x_prompt.reshape(1, Bp * Tp, D_MODEL)
    xs = x_sample
    ks, vs, Cs, ns, ms, Ss = [], [], [], [], [], []
    for li in range(DEPTH):
        lp = _layer_params(li, w_in, w_out, ffn_w_in, ffn_w_out, norm_pre, norm_post, ml_i_bias, ml_f_bias, ml_norm,
                           dn_conv, dn_A_log, dn_dt_bias, dn_norm, da_lambda, da_norm)
        mod_s = jnp.transpose(mod[li, :, :Bs], (1, 0, 2))
        mod_p = mod[li, :, Bs:Bs + 1].reshape(1, N_MOD, D_MODEL)
        xp, (k_c, v_c, C_c, n_c, m_c, S_c) = _trunk_layer(xp, mod_p, (Bp, Tp), lp, li, None, zeros_p, None, seg)
        ks.append(k_c.reshape(Bp, Tp, HEADS, 2, DA_QK))
        vs.append(v_c.reshape(Bp, Tp, HEADS, HEAD_DIM))
        Cs.append(C_c.reshape(Bp, 2, HEADS, HEAD_DIM, HEAD_DIM))
        ns.append(n_c.reshape(Bp, 2, HEADS, HEAD_DIM))
        ms.append(m_c.reshape(Bp, 2, HEADS))
        Ss.append(S_c.reshape(Bp, 2, HEADS, HEAD_DIM, HEAD_DIM))
        states = (state_mlstm_C[:, li].reshape(Bs, n_st, HEAD_DIM, HEAD_DIM),
                  state_mlstm_n[:, li].reshape(Bs, n_st, 1, HEAD_DIM),
                  state_mlstm_m[:, li].reshape(Bs, n_st, 1, 1),
                  state_delta_S[:, li].reshape(Bs, n_st, HEAD_DIM, HEAD_DIM))
        xs, _ = _trunk_layer(xs, mod_s, (Bs, Ts), lp, li, rope_tabs, states, (ck, cv, li), seg)
    return (xp.reshape(Bp, Tp, D_MODEL), xs, jnp.stack(ks, axis=1), jnp.stack(vs, axis=1), jnp.stack(Cs, axis=1),
            jnp.stack(ns, axis=1), jnp.stack(ms, axis=1), jnp.stack(Ss, axis=1))
```

```python
import functools
import math

import numpy as np
import jax
import jax.numpy as jnp
from jax import lax
from jax.experimental import pallas as pl
from jax.experimental.pallas import tpu as pltpu

D_MODEL = 1024
DEPTH = 2
GRID_W = 64
HEADS = 4
HEAD_DIM = 64
MIX_SEC = HEADS * HEAD_DIM
DA_QK = 32
CONV_K = 3
D_FF = ((8 * D_MODEL // 3 + 127) // 128) * 128
N_MOD = 9
CHUNK = 64
N_GATE = 32
ROPE_AXIS_PAIRS = DA_QK // 4
ROPE_BASE = 10000.0
EPS = 1e-6
HALF = 0.5
F32 = jnp.float32
BF16 = jnp.bfloat16
LANES = 128
VMEM_LIMIT = 56 * 1024 * 1024

_OFF_ML = 0
_OFF_MLG = 4 * MIX_SEC
_OFF_DN = _OFF_MLG + 16
_OFF_DNG = _OFF_DN + 4 * MIX_SEC
_OFF_FT = _OFF_DNG + 16
_OFF_DA = _OFF_FT + MIX_SEC
IN_WIDTH = _OFF_DA + 3 * MIX_SEC
MAIN_WIDTH = 12 * MIX_SEC


def _cparams(sem):
    return pltpu.CompilerParams(dimension_semantics=sem, vmem_limit_bytes=VMEM_LIMIT)


def _mm(a, b):
    return jnp.dot(a.astype(BF16), b.astype(BF16), preferred_element_type=F32)


def _mm_nt(a, b):
    return lax.dot_general(a.astype(BF16), b.astype(BF16), (((1,), (1,)), ((), ())), preferred_element_type=F32)


def _mm_tn(a, b):
    return lax.dot_general(a.astype(BF16), b.astype(BF16), (((0,), (0,)), ((), ())), preferred_element_type=F32)


def _split(a):
    hi = a.astype(BF16)
    lo = (a - hi.astype(F32)).astype(BF16)
    return hi, lo


def _mm3(a, b):
    ah, al = _split(a)
    bh, bl = _split(b)
    dot = functools.partial(jnp.dot, preferred_element_type=F32)
    return dot(ah, bh) + (dot(ah, bl) + dot(al, bh))


def _mm_exact_rhs(a, b_bf16):
    ah, al = _split(a)
    dot = functools.partial(jnp.dot, preferred_element_type=F32)
    return dot(ah, b_bf16) + dot(al, b_bf16)


def _mm_exact_lhs(a_bf16, b):
    bh, bl = _split(b)
    dot = functools.partial(jnp.dot, preferred_element_type=F32)
    return dot(a_bf16, bh) + dot(a_bf16, bl)


def _rms(x):
    return x * lax.rsqrt(jnp.mean(x * x, axis=-1, keepdims=True) + EPS)


def _silu(x):
    return x * jax.nn.sigmoid(x)


def _log_sigmoid(x):
    return jnp.minimum(x, 0.0) - jnp.log1p(jnp.exp(-jnp.abs(x)))


def _softplus(x):
    return jnp.maximum(x, 0.0) + jnp.log1p(jnp.exp(-jnp.abs(x)))


def _head_mean_sq(x, seg_ref):
    return _mm_exact_rhs(x * x, seg_ref[...]) * (1.0 / HEAD_DIM)


def _tri_masks():
    row = lax.broadcasted_iota(jnp.int32, (CHUNK, CHUNK), 0)
    col = lax.broadcasted_iota(jnp.int32, (CHUNK, CHUNK), 1)
    return row >= col, row <= col, row > col, row < col


def _mod_kernel(c_ref, w_ref, b_ref, o_ref):
    c = c_ref[...]
    o_ref[...] = _mm(_silu(c), w_ref[...]) + b_ref[...]


def _modulation(cond, w_mod, b_mod):
    R = cond.shape[0]
    return pl.pallas_call(
        _mod_kernel,
        grid=(DEPTH, N_MOD),
        in_specs=[pl.BlockSpec((R, D_MODEL), lambda l, j: (0, 0)),
                  pl.BlockSpec((None, D_MODEL, D_MODEL), lambda l, j: (l, 0, j)),
                  pl.BlockSpec((None, None, 1, D_MODEL), lambda l, j: (l, j, 0, 0))],
        out_specs=pl.BlockSpec((None, None, R, D_MODEL), lambda l, j: (l, j, 0, 0)),
        out_shape=jax.ShapeDtypeStruct((DEPTH, N_MOD, R, D_MODEL), F32),
        compiler_params=_cparams(("arbitrary", "arbitrary")),
        name="modulation",
    )(cond, w_mod, b_mod.reshape(DEPTH, N_MOD, 1, D_MODEL))


def _ffn_kernel(x_ref, mod_ref, gpre_ref, gpost_ref, wa_ref, wb_ref, wo_ref, o_ref, h_scr, acc_scr, *, m0, n_ff):
    j = pl.program_id(1)

    @pl.when(j == 0)
    def _():
        h = _rms(x_ref[...]) * gpre_ref[...] * (1.0 + mod_ref[m0 + 1:m0 + 2, :]) + mod_ref[m0:m0 + 1, :]
        h_scr[...] = h.astype(BF16)
        acc_scr[...] = jnp.zeros_like(acc_scr)

    h = h_scr[...]
    a = jnp.dot(h, wa_ref[...], preferred_element_type=F32)
    b = jnp.dot(h, wb_ref[...], preferred_element_type=F32)
    acc_scr[...] += _mm(_silu(a) * b, wo_ref[...])

    @pl.when(j == n_ff - 1)
    def _():
        y = _rms(acc_scr[...]) * gpost_ref[...]
        o_ref[...] = x_ref[...] + HALF * mod_ref[m0 + 2:m0 + 3, :] * y


def _ffn_tiles(T):
    tm = min(T, 1024)
    return tm, 256


def _ffn_block(x, mod, g_pre, g_post, w_in, w_out, m0):
    B, T, _ = x.shape
    tm, tf = _ffn_tiles(T)
    tpb, n_ff = T // tm, D_FF // tf
    return pl.pallas_call(
        functools.partial(_ffn_kernel, m0=m0, n_ff=n_ff),
        grid=(B * tpb, n_ff),
        in_specs=[pl.BlockSpec((None, tm, D_MODEL), lambda i, j: (i // tpb, i % tpb, 0)),
                  pl.BlockSpec((None, N_MOD, D_MODEL), lambda i, j: (i // tpb, 0, 0)),
                  pl.BlockSpec((1, D_MODEL), lambda i, j: (0, 0)),
                  pl.BlockSpec((1, D_MODEL), lambda i, j: (0, 0)),
                  pl.BlockSpec((D_MODEL, tf), lambda i, j: (0, j)),
                  pl.BlockSpec((D_MODEL, tf), lambda i, j: (0, j + n_ff)),
                  pl.BlockSpec((tf, D_MODEL), lambda i, j: (j, 0))],
        out_specs=pl.BlockSpec((None, tm, D_MODEL), lambda i, j: (i // tpb, i % tpb, 0)),
        out_shape=jax.ShapeDtypeStruct(x.shape, F32),
        scratch_shapes=[pltpu.VMEM((tm, D_MODEL), BF16), pltpu.VMEM((tm, D_MODEL), F32)],
        compiler_params=_cparams(("parallel", "arbitrary")),
        name="ffn_block",
    )(x, mod, g_pre, g_post, w_in, w_in, w_out)


def _rope(x, cos, sin_signed):
    lane = lax.broadcasted_iota(jnp.int32, x.shape, 1)
    nxt = pltpu.roll(x, x.shape[1] - 1, 1)
    prv = pltpu.roll(x, 1, 1)
    swapped = jnp.where(lane % 2 == 0, nxt, prv)
    return x * cos + swapped * sin_signed


def _inproj_kernel(*refs, rope):
    if rope:
        (x_ref, mod_ref, g_ref, w_ref, wg_ref, cos_ref, sin_ref,
         ml_ref, mlo_ref, dn_ref, dng_ref, ft_ref, q_ref, k_ref, v_ref, gc_ref) = refs
    else:
        (x_ref, mod_ref, g_ref, w_ref, wg_ref,
         ml_ref, mlo_ref, dn_ref, dng_ref, ft_ref, q_ref, k_ref, v_ref, gc_ref) = refs
    h = (_rms(x_ref[...]) * g_ref[...] * (1.0 + mod_ref[4:5, :]) + mod_ref[3:4, :]).astype(BF16)

    def proj(lo, width):
        return jnp.dot(h, w_ref[:, lo:lo + width], preferred_element_type=F32)

    ml_ref[...] = proj(0, 3 * MIX_SEC)
    mlo_ref[...] = proj(3 * MIX_SEC, MIX_SEC)
    dn_ref[...] = proj(4 * MIX_SEC, 3 * MIX_SEC)
    dng_ref[...] = proj(7 * MIX_SEC, MIX_SEC)
    ft_ref[...] = proj(8 * MIX_SEC, MIX_SEC)
    q = proj(9 * MIX_SEC, MIX_SEC)
    k = proj(10 * MIX_SEC, MIX_SEC)
    if rope:
        q = _rope(q, cos_ref[...], sin_ref[...])
        k = _rope(k, cos_ref[...], sin_ref[...])
    q_ref[...] = q
    k_ref[...] = k
    v_ref[...] = proj(11 * MIX_SEC, MIX_SEC)
    gc_ref[...] = jnp.dot(h, wg_ref[...], preferred_element_type=F32)


def _inproj(x, mod, g_pre, w_main, w_gate, rope_tabs):
    B, T, _ = x.shape
    tm = min(T, 512)
    tpb = T // tm
    rope = rope_tabs is not None
    tok = lambda w: pl.BlockSpec((None, tm, w), lambda i: (i // tpb, i % tpb, 0))
    full = lambda a: pl.BlockSpec(a.shape, lambda i: (0,) * a.ndim)
    in_specs = [tok(D_MODEL), pl.BlockSpec((None, N_MOD, D_MODEL), lambda i: (i // tpb, 0, 0)),
                full(g_pre), full(w_main), full(w_gate)]
    args = [x, mod, g_pre, w_main, w_gate]
    if rope:
        in_specs += [pl.BlockSpec((tm, MIX_SEC), lambda i: (i % tpb, 0))] * 2
        args += list(rope_tabs)
    widths = (3 * MIX_SEC, MIX_SEC, 3 * MIX_SEC, MIX_SEC, MIX_SEC, MIX_SEC, MIX_SEC, MIX_SEC, LANES)
    out_specs = [tok(w) for w in widths]
    out_shape = [jax.ShapeDtypeStruct((B, T, w), F32) for w in widths]
    return pl.pallas_call(
        functools.partial(_inproj_kernel, rope=rope),
        grid=(B * tpb,),
        in_specs=in_specs, out_specs=out_specs, out_shape=out_shape,
        compiler_params=_cparams(("parallel",)),
        name="in_proj",
    )(*args)


def _split3(a):
    hi = a.astype(BF16)
    r = a - hi.astype(F32)
    mid = r.astype(BF16)
    low = (r - mid.astype(F32)).astype(BF16)
    return hi, mid, low


def _exact3(a, b_bf16):
    hi, mid, low = _split3(a)
    dot = functools.partial(jnp.dot, preferred_element_type=F32)
    return dot(hi, b_bf16) + (dot(mid, b_bf16) + dot(low, b_bf16))


def _exact3_lhs(a_bf16, b):
    hi, mid, low = _split3(b)
    dot = functools.partial(jnp.dot, preferred_element_type=F32)
    return dot(a_bf16, hi) + (dot(a_bf16, mid) + dot(a_bf16, low))


def _block_diag(x, seg):
    return jnp.concatenate([x] * HEADS, axis=0) * seg


def _block_diag_pair(pair, seg):
    return _block_diag(pair[0], seg), _block_diag(pair[1], seg)


def _packed_masks():
    row = lax.broadcasted_iota(jnp.int32, (CHUNK, MIX_SEC), 0)
    pos = lax.broadcasted_iota(jnp.int32, (CHUNK, MIX_SEC), 1) % CHUNK
    return (pos <= row, pos >= row), (pos < row, pos > row), jnp.where(pos == row, 1.0, 0.0)


def _cumsum_matrices():
    incl_lo, incl_up, _, _ = _tri_masks()
    return incl_lo.astype(BF16), incl_up.astype(BF16)


def _bc_to_row(x_bc, eye):
    return jnp.sum(x_bc * eye, axis=0, keepdims=True)


def _head_rows(blocks):
    return jnp.concatenate(blocks, axis=-1)


def _mlstm_kernel(x_ref, gc_ref, bias_ref, seg_ref, segf_ref, sel_ref, c0_ref, n0_ref, m0_ref,
                  h_ref, c_ref, n_ref, m_ref,
                  dl_scr, emu_scr, num_scr, sum_scr, ka_scr, nloc_scr, mul_scr, cbd_scr, nrow_scr, mrow_scr,
                  *, n_chunks):
    seg = seg_ref[...]
    incl, _, eye = _packed_masks()
    tri64 = _cumsum_matrices()
    lane_head = lax.broadcasted_iota(jnp.int32, (CHUNK, MIX_SEC), 1) // HEAD_DIM

    zero = jnp.zeros((HEAD_DIM, HEAD_DIM), F32)
    for d in range(2):
        for hd in range(HEADS):
            blocks = [zero] * hd + [c0_ref[d * HEADS + hd]] + [zero] * (HEADS - 1 - hd)
            cbd_scr[d, hd * HEAD_DIM:(hd + 1) * HEAD_DIM, :] = _head_rows(blocks)
        n_row = _head_rows([n0_ref[d * HEADS + hd] for hd in range(HEADS)])
        m_row = _head_rows([jnp.broadcast_to(m0_ref[d * HEADS + hd], (1, HEAD_DIM)) for hd in range(HEADS)])
        nrow_scr[d] = jnp.broadcast_to(n_row, (8, MIX_SEC))
        mrow_scr[d] = jnp.broadcast_to(m_row, (8, MIX_SEC))

    def front(c):
        r0 = pl.multiple_of(c * CHUNK, CHUNK)
        x = x_ref[pl.ds(r0, CHUNK), :]
        q = x[:, 0:MIX_SEC]
        k = x[:, MIX_SEC:2 * MIX_SEC] * (HEAD_DIM ** -0.5)
        v16 = x[:, 2 * MIX_SEC:3 * MIX_SEC].astype(BF16)
        k16 = k.astype(BF16)
        qk = lax.dot_general(q.astype(BF16), _block_diag(k16, seg), (((1,), (1,)), ((), ())),
                             preferred_element_type=F32)
        vbd = _block_diag(v16, seg)
        gcol = gc_ref[pl.ds(r0, CHUNK), :]
        i128 = gcol + bias_ref[0:1, :]
        f128 = _log_sigmoid(gcol + bias_ref[1:2, :])
        jobs = []
        for d in range(2):
            i_bc = _exact3(i128, sel_ref[d])
            b_bc = _exact3(_exact3_lhs(tri64[d], f128), sel_ref[2 + d])
            r_bc = i_bc - b_bc
            dmat = jnp.where(incl[d], b_bc + _bc_to_row(r_bc, eye), -jnp.inf)
            mu = [jnp.max(dmat[:, hd * HEAD_DIM:(hd + 1) * HEAD_DIM], axis=-1, keepdims=True) for hd in range(HEADS)]
            mu_bc = jnp.where(lane_head == 0, mu[0], jnp.where(lane_head == 1, mu[1],
                                                               jnp.where(lane_head == 2, mu[2], mu[3])))
            s_loc = qk * jnp.exp(dmat - mu_bc)
            last = CHUNK - 1 if d == 0 else 0
            mu_last = mu_bc[last:last + 1, :]
            ka = k * jnp.exp(b_bc[last:last + 1, :] + r_bc - mu_last)
            idx = d * n_chunks + c
            dl_scr[idx] = b_bc - mu_bc
            emu_scr[idx] = jnp.exp(-mu_bc)
            ka_scr[idx] = ka.astype(BF16)
            nloc_scr[idx] = jnp.broadcast_to(jnp.sum(ka, axis=0, keepdims=True), (8, MIX_SEC))
            mul_scr[idx] = jnp.broadcast_to(mu_last, (8, MIX_SEC))
            jobs.append((idx, s_loc, vbd))
        return jobs

    def prep_pair(i, carry):
        jobs = front(2 * i) + front(2 * i + 1)
        nums = [jnp.dot(s_loc.astype(BF16), vbd, preferred_element_type=F32) for _, s_loc, vbd in jobs]
        sums = [_mm_exact_rhs(s_loc, seg) for _, s_loc, _ in jobs]
        for (idx, _, _), num, ssum in zip(jobs, nums, sums):
            num_scr[idx] = num
            sum_scr[idx] = ssum
        return carry

    lax.fori_loop(0, n_chunks // 2, prep_pair, 0)

    def step(c, carry):
        ccs = (c, n_chunks - 1 - c)
        rows = [pl.ds(pl.multiple_of(ccs[d] * CHUNK, CHUNK), CHUNK) for d in range(2)]
        idxs = [d * n_chunks + ccs[d] for d in range(2)]
        qs = [x_ref[rows[d], 0:MIX_SEC] for d in range(2)]
        v16 = [x_ref[rows[d], 2 * MIX_SEC:3 * MIX_SEC].astype(BF16) for d in range(2)]
        c_prev = [cbd_scr[d] for d in range(2)]
        n_prev = [nrow_scr[d][0:1, :] for d in range(2)]
        m_prev = [mrow_scr[d][0:1, :] for d in range(2)]
        qc = [jnp.dot(qs[d].astype(BF16), c_prev[d].astype(BF16), preferred_element_type=F32) for d in range(2)]
        qn = [_mm_exact_rhs(qs[d] * n_prev[d], seg) for d in range(2)]
        upd = [_mm_tn(ka_scr[idxs[d]], v16[d]) for d in range(2)]
        for d in range(2):
            z = dl_scr[idxs[d]] + m_prev[d]
            zp = jnp.maximum(z, 0.0)
            w_inter = jnp.exp(z - zp)
            corr = jnp.exp(-zp)
            num = w_inter * qc[d] + corr * num_scr[idxs[d]]
            den = w_inter * qn[d] + corr * sum_scr[idxs[d]]
            h_ref[d, rows[d], :] = num / jnp.maximum(jnp.abs(den), emu_scr[idxs[d]] * corr)
            last = CHUNK - 1 if d == 0 else 0
            a_prev = w_inter[last:last + 1, :]
            gamma = corr[last:last + 1, :]
            cbd_scr[d] = a_prev * c_prev[d] + gamma * (upd[d] * segf_ref[...])
            nrow_scr[d] = jnp.broadcast_to(a_prev * n_prev[d] + gamma * nloc_scr[idxs[d]][0:1, :], (8, MIX_SEC))
            mrow_scr[d] = jnp.broadcast_to(mul_scr[idxs[d]][0:1, :] + zp[last:last + 1, :], (8, MIX_SEC))
        return carry

    lax.fori_loop(0, n_chunks, step, 0)

    for d in range(2):
        for hd in range(HEADS):
            sl = slice(hd * HEAD_DIM, (hd + 1) * HEAD_DIM)
            c_ref[d * HEADS + hd] = cbd_scr[d, sl, sl]
            n_ref[d * HEADS + hd] = nrow_scr[d][0:1, sl]
            m_ref[d * HEADS + hd] = mrow_scr[d][0:1, hd * HEAD_DIM:hd * HEAD_DIM + 1]


def _mlstm(ml_qkv, g_col, bias_row, seg, segf, sel, c0, n0, m0):
    B, T, _ = ml_qkv.shape
    nc = T // CHUNK
    per_b = lambda *s: pl.BlockSpec((None,) + s, lambda b: (b,) + (0,) * len(s))
    full = lambda a: pl.BlockSpec(a.shape, lambda b: (0,) * a.ndim)
    st_shapes = [(2 * HEADS, HEAD_DIM, HEAD_DIM), (2 * HEADS, 1, HEAD_DIM), (2 * HEADS, 1, 1)]
    per_chunk = lambda rows, dt: pltpu.VMEM((2 * nc, rows, MIX_SEC), dt)
    return pl.pallas_call(
        functools.partial(_mlstm_kernel, n_chunks=nc),
        grid=(B,),
        in_specs=[per_b(T, 3 * MIX_SEC), per_b(T, LANES), full(bias_row), full(seg), full(segf), full(sel)]
                 + [per_b(*s) for s in st_shapes],
        out_specs=[pl.BlockSpec((2, None, T, MIX_SEC), lambda b: (0, b, 0, 0))] + [per_b(*s) for s in st_shapes],
        out_shape=[jax.ShapeDtypeStruct((2, B, T, MIX_SEC), F32)]
                  + [jax.ShapeDtypeStruct((B,) + s, F32) for s in st_shapes],
        scratch_shapes=[per_chunk(CHUNK, F32), per_chunk(CHUNK, F32), per_chunk(CHUNK, F32), per_chunk(CHUNK, F32),
                        per_chunk(CHUNK, BF16), per_chunk(8, F32), per_chunk(8, F32),
                        pltpu.VMEM((2, MIX_SEC, MIX_SEC), F32), pltpu.VMEM((2, 8, MIX_SEC), F32),
                        pltpu.VMEM((2, 8, MIX_SEC), F32)],
        compiler_params=_cparams(("parallel",)),
        name="mlstm",
    )(ml_qkv, g_col, bias_row, seg, segf, sel, c0, n0, m0)


def _packed_mm3(l_pair, w_pair):
    (lh, ll), (wh, wl) = l_pair, w_pair
    dot = functools.partial(jnp.dot, preferred_element_type=F32)
    top = dot(jnp.concatenate([lh, ll], axis=0), wh)
    return top[0:CHUNK] + (top[CHUNK:2 * CHUNK] + dot(lh, wl))


def _packed_tri_inverses(a_list, eye, seg):
    a_pairs = [_split(a) for a in a_list]
    ps = [eye - a for a in a_list]
    xs = [_packed_mm3(ap, _block_diag_pair(ap, seg)) for ap in a_pairs]
    for i in range(5):
        x_pairs = [_split(x) for x in xs]
        wxs = [_block_diag_pair(xp, seg) for xp in x_pairs]
        ps = [p + _packed_mm3(_split(p), wx) for p, wx in zip(ps, wxs)]
        if i < 4:
            xs = [_packed_mm3(xp, wx) for xp, wx in zip(x_pairs, wxs)]
    return ps


def _deltanet_kernel(x_ref, gc_ref, conv_ref, par_row_ref, seg_ref, segf_ref, sel_ref,
                     s0_ref, o_ref, s_ref, u_scr, w_scr, qg_scr, attn_scr, kgt_scr, egl_scr, sbd_scr, *, n_chunks):
    T = n_chunks * CHUNK
    seg = seg_ref[...]
    incl, strict, eye = _packed_masks()
    tri64 = _cumsum_matrices()

    zero = jnp.zeros((HEAD_DIM, HEAD_DIM), F32)
    for d in range(2):
        for hd in range(HEADS):
            blocks = [zero] * hd + [s0_ref[d * HEADS + hd]] + [zero] * (HEADS - 1 - hd)
            sbd_scr[d, hd * HEAD_DIM:(hd + 1) * HEAD_DIM, :] = jnp.concatenate(blocks, axis=-1)

    def front(c):
        r0 = pl.multiple_of(c * CHUNK, CHUNK)
        xc = x_ref[pl.ds(r0, CHUNK), :]
        before = x_ref[pl.ds(pl.multiple_of(jnp.maximum(r0 - 8, 0), 8), 8), :][7:8, :]
        after = x_ref[pl.ds(pl.multiple_of(jnp.minimum(r0 + CHUNK, T - 8), 8), 8), :][0:1, :]
        before = jnp.where(c > 0, before, 0.0)
        after = jnp.where(c < n_chunks - 1, after, 0.0)
        rows = lax.broadcasted_iota(jnp.int32, xc.shape, 0)
        x_prev = jnp.where(rows == 0, before, pltpu.roll(xc, 1, 0))
        x_next = jnp.where(rows == CHUNK - 1, after, pltpu.roll(xc, CHUNK - 1, 0))
        y = _silu(x_prev * conv_ref[0:1, :] + xc * conv_ref[1:2, :] + x_next * conv_ref[2:3, :])
        q = y[:, 0:MIX_SEC]
        k = y[:, MIX_SEC:2 * MIX_SEC]
        v = y[:, 2 * MIX_SEC:3 * MIX_SEC]
        q = q * lax.rsqrt(_head_mean_sq(q, seg_ref) * HEAD_DIM + EPS) * (HEAD_DIM ** -0.5)
        k = k * lax.rsqrt(_head_mean_sq(k, seg_ref) * HEAD_DIM + EPS)
        k16 = k.astype(BF16)
        kq = lax.dot_general(jnp.concatenate([k16, q.astype(BF16)], axis=0), _block_diag(k16, seg),
                             (((1,), (1,)), ((), ())), preferred_element_type=F32)
        kk, qk = kq[0:CHUNK], kq[CHUNK:2 * CHUNK]
        gcol = gc_ref[pl.ds(r0, CHUNK), :]
        beta128 = jax.nn.sigmoid(gcol)
        la128 = par_row_ref[0:1, :] * _softplus(gcol + par_row_ref[1:2, :])
        jobs = []
        for d in range(2):
            g_bc = _exact3(_exact3_lhs(tri64[d], la128), sel_ref[d])
            beta_bc = _exact3(beta128, sel_ref[2 + d])
            decay = jnp.exp(jnp.where(incl[d], g_bc - _bc_to_row(g_bc, eye), -jnp.inf))
            a = jnp.where(strict[d], beta_bc * kk * decay, 0.0)
            eg = jnp.exp(g_bc)
            last = CHUNK - 1 if d == 0 else 0
            g_last = g_bc[last:last + 1, :]
            idx = d * n_chunks + c
            qg_scr[idx] = (q * eg).astype(BF16)
            attn_scr[idx] = (qk * decay).astype(BF16)
            kgt_scr[idx] = (k * jnp.exp(g_last - g_bc)).T.astype(BF16)
            egl_scr[idx] = jnp.broadcast_to(jnp.exp(g_last), (8, MIX_SEC))
            jobs.append((idx, a, _split(v * beta_bc), _split(k * beta_bc * eg)))
        return jobs

    def prep_pair(i, carry):
        jobs = front(2 * i) + front(2 * i + 1)
        t_pairs = [_split(t) for t in _packed_tri_inverses([job[1] for job in jobs], eye, seg)]
        us = [_packed_mm3(tp, _block_diag_pair(job[2], seg)) for tp, job in zip(t_pairs, jobs)]
        ws = [_packed_mm3(tp, _block_diag_pair(job[3], seg)) for tp, job in zip(t_pairs, jobs)]
        for job, u, w in zip(jobs, us, ws):
            u_scr[job[0]] = u
            w_scr[job[0]] = w.astype(BF16)
        return carry

    lax.fori_loop(0, n_chunks // 2, prep_pair, 0)

    def step(c, carry):
        ccs = (c, n_chunks - 1 - c)
        idxs = [d * n_chunks + ccs[d] for d in range(2)]
        s_prev = [sbd_scr[d] for d in range(2)]
        rs = [jnp.dot(jnp.concatenate([w_scr[idxs[d]], qg_scr[idxs[d]]], axis=0), s_prev[d].astype(BF16),
                      preferred_element_type=F32) for d in range(2)]
        v16 = [(u_scr[idxs[d]] - rs[d][0:CHUNK]).astype(BF16) for d in range(2)]
        outs = [rs[d][CHUNK:2 * CHUNK] + jnp.dot(attn_scr[idxs[d]], _block_diag(v16[d], seg),
                                                 preferred_element_type=F32) for d in range(2)]
        upds = [jnp.dot(kgt_scr[idxs[d]], v16[d], preferred_element_type=F32) for d in range(2)]
        for d in range(2):
            o_ref[d, pl.ds(pl.multiple_of(ccs[d] * CHUNK, CHUNK), CHUNK), :] = outs[d]
            sbd_scr[d] = egl_scr[idxs[d]][0:1, :] * s_prev[d] + upds[d] * segf_ref[...]
        return carry

    lax.fori_loop(0, n_chunks, step, 0)

    for d in range(2):
        for hd in range(HEADS):
            sl = slice(hd * HEAD_DIM, (hd + 1) * HEAD_DIM)
            s_ref[d * HEADS + hd] = sbd_scr[d, sl, sl]


def _deltanet(dn_qkv, g_col, conv_w, par_row, seg, segf, sel, s0):
    B, T, _ = dn_qkv.shape
    nc = T // CHUNK
    per_b = lambda *s: pl.BlockSpec((None,) + s, lambda b: (b,) + (0,) * len(s))
    full = lambda a: pl.BlockSpec(a.shape, lambda b: (0,) * a.ndim)
    st = (2 * HEADS, HEAD_DIM, HEAD_DIM)
    per_chunk = lambda rows, cols, dt: pltpu.VMEM((2 * nc, rows, cols), dt)
    return pl.pallas_call(
        functools.partial(_deltanet_kernel, n_chunks=nc),
        grid=(B,),
        in_specs=[per_b(T, 3 * MIX_SEC), per_b(T, LANES), full(conv_w), full(par_row),
                  full(seg), full(segf), full(sel), per_b(*st)],
        out_specs=[pl.BlockSpec((2, None, T, MIX_SEC), lambda b: (0, b, 0, 0)), per_b(*st)],
        out_shape=[jax.ShapeDtypeStruct((2, B, T, MIX_SEC), F32), jax.ShapeDtypeStruct((B,) + st, F32)],
        scratch_shapes=[per_chunk(CHUNK, MIX_SEC, F32), per_chunk(CHUNK, MIX_SEC, BF16),
                        per_chunk(CHUNK, MIX_SEC, BF16), per_chunk(CHUNK, MIX_SEC, BF16),
                        per_chunk(MIX_SEC, CHUNK, BF16), per_chunk(8, MIX_SEC, F32),
                        pltpu.VMEM((2, MIX_SEC, MIX_SEC), F32)],
        compiler_params=_cparams(("parallel",)),
        name="deltanet",
    )(dn_qkv, g_col, conv_w, par_row, seg, segf, sel, s0)


def _fourier_kernel(x_ref, cg_ref, sg_ref, ct_ref, st_ref, o_ref, p_scr, q_scr, *, scale):
    @pl.when(pl.program_id(1) == 0)
    def _():
        x = x_ref[...].astype(BF16)
        p_scr[...] = jnp.dot(x, cg_ref[...], preferred_element_type=F32).astype(BF16)
        q_scr[...] = jnp.dot(x, sg_ref[...], preferred_element_type=F32).astype(BF16)

    y = (jnp.dot(ct_ref[...], p_scr[...], preferred_element_type=F32)
         - jnp.dot(st_ref[...], q_scr[...], preferred_element_type=F32))
    o_ref[...] = y * scale


def _dft_tables(T):
    def tab(n):
        idx = np.arange(n, dtype=np.int64)
        ang = 2.0 * np.pi * ((idx[:, None] * idx[None, :]) % n).astype(np.float64) / n
        return np.cos(ang), np.sin(ang)
    ct, st = tab(T)
    cg1, sg1 = tab(HEAD_DIM)
    eye = np.eye(HEADS)
    cg, sg = np.kron(eye, cg1), np.kron(eye, sg1)
    return tuple(jnp.asarray(a, dtype=F32).astype(BF16) for a in (cg, sg, ct, st))


def _fourier(x):
    B, T, _ = x.shape
    cg, sg, ct, st = _dft_tables(T)
    tr = min(T, 512)
    return pl.pallas_call(
        functools.partial(_fourier_kernel, scale=1.0 / math.sqrt(T * HEAD_DIM)),
        grid=(B, T // tr),
        in_specs=[pl.BlockSpec((None, T, MIX_SEC), lambda b, i: (b, 0, 0)),
                  pl.BlockSpec((MIX_SEC, MIX_SEC), lambda b, i: (0, 0)),
                  pl.BlockSpec((MIX_SEC, MIX_SEC), lambda b, i: (0, 0)),
                  pl.BlockSpec((tr, T), lambda b, i: (i, 0)),
                  pl.BlockSpec((tr, T), lambda b, i: (i, 0))],
        out_specs=pl.BlockSpec((None, tr, MIX_SEC), lambda b, i: (b, i, 0)),
        out_shape=jax.ShapeDtypeStruct((B, T, MIX_SEC), F32),
        scratch_shapes=[pltpu.VMEM((T, MIX_SEC), BF16), pltpu.VMEM((T, MIX_SEC), BF16)],
        compiler_params=_cparams(("parallel", "arbitrary")),
        name="fourier",
    )(x, cg, sg, ct, st)


def _diffattn_kernel(*refs, lam_init, has_ctx):
    if has_ctx:
        q_ref, k_ref, v_ref, kc_ref, vc_ref, lam_ref, o_ref = refs
    else:
        q_ref, k_ref, v_ref, lam_ref, o_ref = refs
    lp = lam_ref[...]
    lam = (jnp.exp(jnp.sum(lp[0:1, :] * lp[1:2, :], axis=-1, keepdims=True))
           - jnp.exp(jnp.sum(lp[2:3, :] * lp[3:4, :], axis=-1, keepdims=True)) + lam_init)
    q = q_ref[...]
    pieces = [(k_ref[...].astype(BF16), v_ref[...].astype(BF16))]
    if has_ctx:
        pieces.append((kc_ref[...].astype(BF16), vc_ref[...].astype(BF16)))
    lane = lax.broadcasted_iota(jnp.int32, q.shape, 1)
    out = jnp.zeros(q.shape, F32)
    for hd in range(HEADS):
        probs = []
        for mp in range(2):
            lo = (hd * 2 + mp) * DA_QK
            qm = jnp.where((lane >= lo) & (lane < lo + DA_QK), q, 0.0).astype(BF16)
            ss = [lax.dot_general(qm, kk, (((1,), (1,)), ((), ())), preferred_element_type=F32) * (DA_QK ** -0.5)
                  for kk, _ in pieces]
            mx = functools.reduce(jnp.maximum, [jnp.max(s, axis=-1, keepdims=True) for s in ss])
            es = [jnp.exp(s - mx) for s in ss]
            den = functools.reduce(jnp.add, [jnp.sum(e, axis=-1, keepdims=True) for e in es])
            probs.append((es, 1.0 / den))
        (e0, r0), (e1, r1) = probs
        r1 = lam * r1
        acc = None
        for j, (_, vv) in enumerate(pieces):
            w = (e0[j] * r0 - e1[j] * r1).astype(BF16)
            part = jnp.dot(w, vv, preferred_element_type=F32)
            acc = part if acc is None else acc + part
        out = jnp.where((lane >= hd * HEAD_DIM) & (lane < (hd + 1) * HEAD_DIM), acc, out)
    o_ref[...] = out


def _diffattn(q, k, v, ctx, lam_par, lam_init):
    B, T, _ = q.shape
    tq = min(T, 256)
    has_ctx = ctx is not None
    in_specs = [pl.BlockSpec((None, tq, MIX_SEC), lambda b, i: (b, i, 0)),
                pl.BlockSpec((None, T, MIX_SEC), lambda b, i: (b, 0, 0)),
                pl.BlockSpec((None, T, MIX_SEC), lambda b, i: (b, 0, 0))]
    args = [q, k, v]
    if has_ctx:
        ck, cv, li = ctx
        P = ck.shape[2]
        in_specs += [pl.BlockSpec((None, None, P, MIX_SEC), lambda b, i: (b, li, 0, 0))] * 2
        args += [ck, cv]
    in_specs.append(pl.BlockSpec(lam_par.shape, lambda b, i: (0, 0)))
    args.append(lam_par)
    return pl.pallas_call(
        functools.partial(_diffattn_kernel, lam_init=lam_init, has_ctx=has_ctx),
        grid=(B, T // tq),
        in_specs=in_specs,
        out_specs=pl.BlockSpec((None, tq, MIX_SEC), lambda b, i: (b, i, 0)),
        out_shape=jax.ShapeDtypeStruct((B, T, MIX_SEC), F32),
        compiler_params=_cparams(("parallel", "arbitrary")),
        name="diff_attention",
    )(*args)


def _mixout_kernel(x_ref, mod_ref, gpost_ref, hml_ref, og_ref, odn_ref, dng_ref, ft_ref, oda_ref,
                   gains_ref, seg_ref, w_ref, o_ref, *, da_scale):
    def head_norm(z, gain_row):
        return z * lax.rsqrt(_head_mean_sq(z, seg_ref) + EPS) * gain_row

    h_ml = (hml_ref[0] + hml_ref[1]) * jax.nn.sigmoid(og_ref[...])
    y_ml = head_norm(h_ml, gains_ref[0:1, :])
    y_dn = head_norm(odn_ref[0] + odn_ref[1], gains_ref[1:2, :]) * _silu(dng_ref[...])
    y_da = head_norm(oda_ref[...], gains_ref[2:3, :]) * da_scale
    y = (_mm(y_ml, w_ref[0:MIX_SEC, :]) + _mm(y_dn, w_ref[MIX_SEC:2 * MIX_SEC, :])
         + _mm(ft_ref[...], w_ref[2 * MIX_SEC:3 * MIX_SEC, :]) + _mm(y_da, w_ref[3 * MIX_SEC:4 * MIX_SEC, :]))
    o_ref[...] = x_ref[...] + mod_ref[5:6, :] * (_rms(y) * gpost_ref[...])


def _mixout(x, mod, g_post, h_ml, ml, o_dn, dn_g, y_ft, o_da, gains, seg, w_out, da_scale):
    B, T, _ = x.shape
    tm = min(T, 512)
    tpb = T // tm
    n_tok = B * T
    x2 = lambda a: a.reshape(n_tok, a.shape[-1])
    h_ml = h_ml.reshape(2, n_tok, MIX_SEC)
    o_dn = o_dn.reshape(2, n_tok, MIX_SEC)
    tok = lambda w: pl.BlockSpec((tm, w), lambda i: (i, 0))
    two = pl.BlockSpec((2, tm, MIX_SEC), lambda i: (0, i, 0))
    full = lambda a: pl.BlockSpec(a.shape, lambda i: (0,) * a.ndim)
    out = pl.pallas_call(
        functools.partial(_mixout_kernel, da_scale=da_scale),
        grid=(n_tok // tm,),
        in_specs=[tok(D_MODEL), pl.BlockSpec((None, N_MOD, D_MODEL), lambda i: (i // tpb, 0, 0)), full(g_post),
                  two, tok(MIX_SEC), two, tok(MIX_SEC), tok(MIX_SEC), tok(MIX_SEC),
                  full(gains), full(seg), full(w_out)],
        out_specs=tok(D_MODEL),
        out_shape=jax.ShapeDtypeStruct((n_tok, D_MODEL), F32),
        compiler_params=_cparams(("parallel",)),
        name="mix_out",
    )(x2(x), mod, g_post, h_ml, x2(ml), o_dn, x2(dn_g), x2(y_ft), x2(o_da), gains, seg, w_out)
    return out.reshape(x.shape)


def _rope_tables(T):
    t = np.arange(T)
    row = (t // GRID_W).astype(np.float64)
    col = (t % GRID_W).astype(np.float64)
    inv = ROPE_BASE ** (-np.arange(ROPE_AXIS_PAIRS, dtype=np.float64) / ROPE_AXIS_PAIRS)
    ang = np.concatenate([row[:, None] * inv, col[:, None] * inv], axis=-1)
    cos = np.repeat(np.cos(ang), 2, axis=-1)
    sin = np.repeat(np.sin(ang), 2, axis=-1) * np.tile(np.array([-1.0, 1.0]), DA_QK // 2)
    reps = MIX_SEC // DA_QK
    return jnp.asarray(np.tile(cos, (1, reps)), dtype=F32), jnp.asarray(np.tile(sin, (1, reps)), dtype=F32)


def _layer_params(li, w_in, w_out, ffn_w_in, ffn_w_out, norm_pre, norm_post, ml_i_bias, ml_f_bias, ml_norm,
                  dn_conv, dn_A_log, dn_dt_bias, dn_norm, da_lambda, da_norm):
    w = w_in[li]
    main = jnp.concatenate([w[:, _OFF_ML:_OFF_MLG], w[:, _OFF_DN:_OFF_DNG], w[:, _OFF_FT:IN_WIDTH]], axis=1)
    gate = jnp.concatenate([w[:, _OFF_MLG:_OFF_DN], w[:, _OFF_DNG:_OFF_FT]], axis=1)
    on_lanes = lambda vec, lo: jnp.pad(vec.reshape(-1), (lo, LANES - lo - 2 * HEADS))
    ml_bias = jnp.stack([on_lanes(ml_i_bias[li], 0), on_lanes(ml_f_bias[li], 8)])
    dn_par_row = jnp.stack([on_lanes(-jnp.exp(dn_A_log[li]), 24), on_lanes(dn_dt_bias[li], 24)])
    return dict(
        w_main=main.astype(BF16),
        w_gate=jnp.pad(gate, ((0, 0), (0, LANES - N_GATE))).astype(BF16),
        w_out=w_out[li].astype(BF16),
        ffn_w_in=[ffn_w_in[li, j].astype(BF16) for j in range(2)],
        ffn_w_out=[ffn_w_out[li, j].astype(BF16) for j in range(2)],
        norm_pre=[norm_pre[li, j][None, :] for j in range(3)],
        norm_post=[norm_post[li, j][None, :] for j in range(3)],
        ml_bias_row=ml_bias,
        dn_par_row=dn_par_row,
        dn_conv=dn_conv[li],
        gains=jnp.stack([ml_norm[li].reshape(-1), jnp.tile(dn_norm[li], HEADS), jnp.tile(da_norm[li], HEADS)]),
        da_lambda=da_lambda[li],
    )


def _packed_consts():
    seg = np.kron(np.eye(HEADS), np.ones((HEAD_DIM, HEAD_DIM)))

    def spread(first_lanes):
        sel = np.zeros((4, LANES, MIX_SEC))
        for j, lo in enumerate(first_lanes):
            for hd in range(HEADS):
                sel[j, lo + hd, hd * HEAD_DIM:(hd + 1) * HEAD_DIM] = 1.0
        return jnp.asarray(sel, dtype=BF16)

    return dict(seg=jnp.asarray(seg, dtype=BF16), segf=jnp.asarray(seg, dtype=F32),
                sel_ml=spread((0, HEADS, 8, 8 + HEADS)), sel_dn=spread((24, 24 + HEADS, 16, 16 + HEADS)))


def _mixing(x_tok, mod_tok, shape_bt, lp, li, rope_tabs, states, ctx, consts):
    B, T = shape_bt
    seg, segf = consts['seg'], consts['segf']
    lam_init = 0.8 - 0.6 * math.exp(-0.3 * li)
    ml, ml_o, dn, dn_g, ft, da_q, da_k, da_v, g_col = _inproj(
        x_tok, mod_tok, lp['norm_pre'][1], lp['w_main'], lp['w_gate'], rope_tabs)
    seq = lambda a: a.reshape(B, T, a.shape[-1])
    c0, n0, m0, s0 = states
    h_ml, c_f, n_f, m_f = _mlstm(seq(ml), seq(g_col), lp['ml_bias_row'], seg, segf, consts['sel_ml'], c0, n0, m0)
    o_dn, s_f = _deltanet(seq(dn), seq(g_col), lp['dn_conv'], lp['dn_par_row'], seg, segf, consts['sel_dn'], s0)
    y_ft = _fourier(seq(ft))
    o_da = _diffattn(seq(da_q), seq(da_k), seq(da_v), ctx, lp['da_lambda'], lam_init)
    x_new = _mixout(x_tok, mod_tok, lp['norm_post'][1], h_ml, ml_o, o_dn, dn_g, y_ft, o_da, lp['gains'], seg,
                    lp['w_out'], 1.0 - lam_init)
    return x_new, (seq(da_k), seq(da_v), c_f, n_f, m_f, s_f)


def _trunk_layer(x_tok, mod_tok, shape_bt, lp, li, rope_tabs, states, ctx, consts):
    x_tok = _ffn_block(x_tok, mod_tok, lp['norm_pre'][0], lp['norm_post'][0], lp['ffn_w_in'][0], lp['ffn_w_out'][0], 0)
    x_tok, aux = _mixing(x_tok, mod_tok, shape_bt, lp, li, rope_tabs, states, ctx, consts)
    x_tok = _ffn_block(x_tok, mod_tok, lp['norm_pre'][2], lp['norm_post'][2], lp['ffn_w_in'][1], lp['ffn_w_out'][1], 6)
    return x_tok, aux


def kernel(x_prompt, x_sample, cache_k, cache_v, state_mlstm_C, state_mlstm_n, state_mlstm_m, state_delta_S, c, c_ctx, w_mod, b_mod, norm_pre, norm_post, ffn_w_in, ffn_w_out, w_in, w_out, ml_i_bias, ml_f_bias, ml_norm, dn_conv, dn_A_log, dn_dt_bias, dn_norm, da_lambda, da_norm):
    Bp, Tp, _ = x_prompt.shape
    Bs, Ts, _ = x_sample.shape
    P = cache_k.shape[2]
    n_st = 2 * HEADS

    cond = jnp.concatenate([c, c_ctx[None, :]], axis=0)
    cond = jnp.pad(cond, ((0, (-cond.shape[0]) % 8), (0, 0)))
    mod = _modulation(cond, w_mod, b_mod)
    rope_tabs = _rope_tables(Ts)
    consts = _packed_consts()
    ck = cache_k.reshape(Bs, DEPTH, P, MIX_SEC)
    cv = cache_v.reshape(Bs, DEPTH, P, MIX_SEC)
    zeros_p = (jnp.zeros((Bp, n_st, HEAD_DIM, HEAD_DIM), F32), jnp.zeros((Bp, n_st, 1, HEAD_DIM), F32),
               jnp.zeros((Bp, n_st, 1, 1), F32), jnp.zeros((Bp, n_st, HEAD_DIM, HEAD_DIM), F32))

    xp = x_prompt.reshape(1, Bp * Tp, D_MODEL)
    xs = x_sample
    ks, vs, Cs, ns, ms, Ss = [], [], [], [], [], []
    for li in range(DEPTH):
        lp = _layer_params(li, w_in, w_out, ffn_w_in, ffn_w_out, norm_pre, norm_post, ml_i_bias, ml_f_bias, ml_norm,
                           dn_conv, dn_A_log, dn_dt_bias, dn_norm, da_lambda, da_norm)
        mod_s = jnp.transpose(mod[li, :, :Bs], (1, 0, 2))
        mod_p = mod[li, :, Bs:Bs + 1].reshape(1, N_MOD, D_MODEL)
        xp, (k_c, v_c, C_c, n_c, m_c, S_c) = _trunk_layer(xp, mod_p, (Bp, Tp), lp, li, None, zeros_p, None, consts)
        ks.append(k_c.reshape(Bp, Tp, HEADS, 2, DA_QK))
        vs.append(v_c.reshape(Bp, Tp, HEADS, HEAD_DIM))
        Cs.append(C_c.reshape(Bp, 2, HEADS, HEAD_DIM, HEAD_DIM))
        ns.append(n_c.reshape(Bp, 2, HEADS, HEAD_DIM))
        ms.append(m_c.reshape(Bp, 2, HEADS))
        Ss.append(S_c.reshape(Bp, 2, HEADS, HEAD_DIM, HEAD_DIM))
        states = (state_mlstm_C[:, li].reshape(Bs, n_st, HEAD_DIM, HEAD_DIM),
                  state_mlstm_n[:, li].reshape(Bs, n_st, 1, HEAD_DIM),
                  state_mlstm_m[:, li].reshape(Bs, n_st, 1, 1),
                  state_delta_S[:, li].reshape(Bs, n_st, HEAD_DIM, HEAD_DIM))
        xs, _ = _trunk_layer(xs, mod_s, (Bs, Ts), lp, li, rope_tabs, states, (ck, cv, li), consts)
    return (xp.reshape(Bp, Tp, D_MODEL), xs, jnp.stack(ks, axis=1), jnp.stack(vs, axis=1), jnp.stack(Cs, axis=1),
            jnp.stack(ns, axis=1), jnp.stack(ms, axis=1), jnp.stack(Ss, axis=1))
```

```python
import functools
import math

import numpy as np
import jax
import jax.numpy as jnp
from jax import lax
from jax.experimental import pallas as pl
from jax.experimental.pallas import tpu as pltpu

D_MODEL = 1024
DEPTH = 2
GRID_W = 64
HEADS = 4
HEAD_DIM = 64
MIX_SEC = HEADS * HEAD_DIM
DA_QK = 32
CONV_K = 3
D_FF = ((8 * D_MODEL // 3 + 127) // 128) * 128
N_MOD = 9
CHUNK = 64
N_GATE = 32
ROPE_AXIS_PAIRS = DA_QK // 4
ROPE_BASE = 10000.0
EPS = 1e-6
HALF = 0.5
LOG2_E = 1.4426950408889634
F32 = jnp.float32
BF16 = jnp.bfloat16
LANES = 128
VMEM_LIMIT = 56 * 1024 * 1024

_OFF_ML = 0
_OFF_MLG = 4 * MIX_SEC
_OFF_DN = _OFF_MLG + 16
_OFF_DNG = _OFF_DN + 4 * MIX_SEC
_OFF_FT = _OFF_DNG + 16
_OFF_DA = _OFF_FT + MIX_SEC
IN_WIDTH = _OFF_DA + 3 * MIX_SEC
MAIN_WIDTH = 12 * MIX_SEC


def _cparams(sem):
    return pltpu.CompilerParams(dimension_semantics=sem, vmem_limit_bytes=VMEM_LIMIT)


def _mm(a, b):
    return jnp.dot(a.astype(BF16), b.astype(BF16), preferred_element_type=F32)


def _mm_nt(a, b):
    return lax.dot_general(a.astype(BF16), b.astype(BF16), (((1,), (1,)), ((), ())), preferred_element_type=F32)


def _mm_tn(a, b):
    return lax.dot_general(a.astype(BF16), b.astype(BF16), (((0,), (0,)), ((), ())), preferred_element_type=F32)


def _split(a):
    hi = a.astype(BF16)
    lo = (a - hi.astype(F32)).astype(BF16)
    return hi, lo


def _mm3(a, b):
    ah, al = _split(a)
    bh, bl = _split(b)
    dot = functools.partial(jnp.dot, preferred_element_type=F32)
    return dot(ah, bh) + (dot(ah, bl) + dot(al, bh))


def _mm_exact_rhs(a, b_bf16):
    ah, al = _split(a)
    dot = functools.partial(jnp.dot, preferred_element_type=F32)
    return dot(ah, b_bf16) + dot(al, b_bf16)


def _mm_exact_lhs(a_bf16, b):
    bh, bl = _split(b)
    dot = functools.partial(jnp.dot, preferred_element_type=F32)
    return dot(a_bf16, bh) + dot(a_bf16, bl)


def _rms(x):
    return x * lax.rsqrt(jnp.mean(x * x, axis=-1, keepdims=True) + EPS)


def _silu(x):
    return x * jax.nn.sigmoid(x)


def _log_sigmoid(x):
    return jnp.minimum(x, 0.0) - jnp.log1p(jnp.exp(-jnp.abs(x)))


def _softplus(x):
    return jnp.maximum(x, 0.0) + jnp.log1p(jnp.exp(-jnp.abs(x)))


def _head_mean_sq(x, seg_ref):
    return _mm_exact_rhs(x * x, seg_ref[...]) * (1.0 / HEAD_DIM)


def _tri_masks():
    row = lax.broadcasted_iota(jnp.int32, (CHUNK, CHUNK), 0)
    col = lax.broadcasted_iota(jnp.int32, (CHUNK, CHUNK), 1)
    return row >= col, row <= col, row > col, row < col


def _mod_kernel(c_ref, w_ref, b_ref, o_ref):
    c = c_ref[...]
    o_ref[...] = _mm(_silu(c), w_ref[...]) + b_ref[...]


def _modulation(cond, w_mod, b_mod):
    R = cond.shape[0]
    return pl.pallas_call(
        _mod_kernel,
        grid=(DEPTH, N_MOD),
        in_specs=[pl.BlockSpec((R, D_MODEL), lambda l, j: (0, 0)),
                  pl.BlockSpec((None, D_MODEL, D_MODEL), lambda l, j: (l, 0, j)),
                  pl.BlockSpec((None, None, 1, D_MODEL), lambda l, j: (l, j, 0, 0))],
        out_specs=pl.BlockSpec((None, None, R, D_MODEL), lambda l, j: (l, j, 0, 0)),
        out_shape=jax.ShapeDtypeStruct((DEPTH, N_MOD, R, D_MODEL), F32),
        compiler_params=_cparams(("arbitrary", "arbitrary")),
        name="modulation",
    )(cond, w_mod, b_mod.reshape(DEPTH, N_MOD, 1, D_MODEL))


FFN_SLICE = 256


def _ffn_kernel(x_ref, mod_ref, gpre_ref, gpost_ref, win_ref, wout_ref, o_ref, *, m0):
    x = x_ref[...]
    h = (_rms(x) * gpre_ref[...] * (1.0 + mod_ref[m0 + 1:m0 + 2, :]) + mod_ref[m0:m0 + 1, :]).astype(BF16)
    acc = None
    for lo in range(0, D_FF, FFN_SLICE):
        a = jnp.dot(h, win_ref[:, lo:lo + FFN_SLICE], preferred_element_type=F32)
        b = jnp.dot(h, win_ref[:, D_FF + lo:D_FF + lo + FFN_SLICE], preferred_element_type=F32)
        part = _mm(_silu(a) * b, wout_ref[lo:lo + FFN_SLICE, :])
        acc = part if acc is None else acc + part
    o_ref[...] = x + HALF * mod_ref[m0 + 2:m0 + 3, :] * (_rms(acc) * gpost_ref[...])


def _ffn_block(x, mod, g_pre, g_post, w_in, w_out, m0):
    B, T, _ = x.shape
    tm = min(T, 512)
    tpb = T // tm
    resident = lambda a: pl.BlockSpec(a.shape, lambda i: (0, 0), pipeline_mode=pl.Buffered(1))
    return pl.pallas_call(
        functools.partial(_ffn_kernel, m0=m0),
        grid=(B * tpb,),
        in_specs=[pl.BlockSpec((None, tm, D_MODEL), lambda i: (i // tpb, i % tpb, 0)),
                  pl.BlockSpec((None, N_MOD, D_MODEL), lambda i: (i // tpb, 0, 0)),
                  pl.BlockSpec((1, D_MODEL), lambda i: (0, 0)),
                  pl.BlockSpec((1, D_MODEL), lambda i: (0, 0)),
                  resident(w_in), resident(w_out)],
        out_specs=pl.BlockSpec((None, tm, D_MODEL), lambda i: (i // tpb, i % tpb, 0)),
        out_shape=jax.ShapeDtypeStruct(x.shape, F32),
        compiler_params=_cparams(("parallel",)),
        name="ffn_block",
    )(x, mod, g_pre, g_post, w_in, w_out)


def _rope(x, cos, sin_signed):
    lane = lax.broadcasted_iota(jnp.int32, x.shape, 1)
    nxt = pltpu.roll(x, x.shape[1] - 1, 1)
    prv = pltpu.roll(x, 1, 1)
    swapped = jnp.where(lane % 2 == 0, nxt, prv)
    return x * cos + swapped * sin_signed


def _inproj_kernel(*refs, rope):
    if rope:
        (x_ref, mod_ref, g_ref, w_ref, wg_ref, cos_ref, sin_ref,
         ml_ref, mlo_ref, dn_ref, dng_ref, ft_ref, q_ref, k_ref, v_ref, gc_ref) = refs
    else:
        (x_ref, mod_ref, g_ref, w_ref, wg_ref,
         ml_ref, mlo_ref, dn_ref, dng_ref, ft_ref, q_ref, k_ref, v_ref, gc_ref) = refs
    h = (_rms(x_ref[...]) * g_ref[...] * (1.0 + mod_ref[4:5, :]) + mod_ref[3:4, :]).astype(BF16)

    def proj(lo, width):
        return jnp.dot(h, w_ref[:, lo:lo + width], preferred_element_type=F32)

    ml_ref[...] = proj(0, 3 * MIX_SEC)
    mlo_ref[...] = proj(3 * MIX_SEC, MIX_SEC)
    dn_ref[...] = proj(4 * MIX_SEC, 3 * MIX_SEC)
    dng_ref[...] = proj(7 * MIX_SEC, MIX_SEC)
    ft_ref[...] = proj(8 * MIX_SEC, MIX_SEC)
    q = proj(9 * MIX_SEC, MIX_SEC)
    k = proj(10 * MIX_SEC, MIX_SEC)
    if rope:
        q = _rope(q, cos_ref[...], sin_ref[...])
        k = _rope(k, cos_ref[...], sin_ref[...])
    q_ref[...] = q
    k_ref[...] = k
    v_ref[...] = proj(11 * MIX_SEC, MIX_SEC)
    gc_ref[...] = jnp.dot(h, wg_ref[...], preferred_element_type=F32)


def _inproj(x, mod, g_pre, w_main, w_gate, rope_tabs):
    B, T, _ = x.shape
    tm = min(T, 512)
    tpb = T // tm
    rope = rope_tabs is not None
    tok = lambda w: pl.BlockSpec((None, tm, w), lambda i: (i // tpb, i % tpb, 0))
    full = lambda a: pl.BlockSpec(a.shape, lambda i: (0,) * a.ndim)
    in_specs = [tok(D_MODEL), pl.BlockSpec((None, N_MOD, D_MODEL), lambda i: (i // tpb, 0, 0)),
                full(g_pre), full(w_main), full(w_gate)]
    args = [x, mod, g_pre, w_main, w_gate]
    if rope:
        in_specs += [pl.BlockSpec((tm, MIX_SEC), lambda i: (i % tpb, 0))] * 2
        args += list(rope_tabs)
    widths = (3 * MIX_SEC, MIX_SEC, 3 * MIX_SEC, MIX_SEC, MIX_SEC, MIX_SEC, MIX_SEC, MIX_SEC, LANES)
    out_specs = [tok(w) for w in widths]
    out_shape = [jax.ShapeDtypeStruct((B, T, w), F32) for w in widths]
    return pl.pallas_call(
        functools.partial(_inproj_kernel, rope=rope),
        grid=(B * tpb,),
        in_specs=in_specs, out_specs=out_specs, out_shape=out_shape,
        compiler_params=_cparams(("parallel",)),
        name="in_proj",
    )(*args)


def _split3(a):
    hi = a.astype(BF16)
    r = a - hi.astype(F32)
    mid = r.astype(BF16)
    low = (r - mid.astype(F32)).astype(BF16)
    return hi, mid, low


def _exact3(a, b_bf16):
    hi, mid, low = _split3(a)
    dot = functools.partial(jnp.dot, preferred_element_type=F32)
    return dot(hi, b_bf16) + (dot(mid, b_bf16) + dot(low, b_bf16))


def _exact3_lhs(a_bf16, b):
    hi, mid, low = _split3(b)
    dot = functools.partial(jnp.dot, preferred_element_type=F32)
    return dot(a_bf16, hi) + (dot(a_bf16, mid) + dot(a_bf16, low))


def _block_diag(x, seg):
    return jnp.concatenate([x] * HEADS, axis=0) * seg


def _block_diag_pair(pair, seg):
    return _block_diag(pair[0], seg), _block_diag(pair[1], seg)


def _packed_masks():
    row = lax.broadcasted_iota(jnp.int32, (CHUNK, MIX_SEC), 0)
    pos = lax.broadcasted_iota(jnp.int32, (CHUNK, MIX_SEC), 1) % CHUNK
    return (pos <= row, pos >= row), (pos < row, pos > row), jnp.where(pos == row, 1.0, 0.0)


def _cumsum_matrices():
    incl_lo, incl_up, _, _ = _tri_masks()
    return incl_lo.astype(BF16), incl_up.astype(BF16)


def _bc_to_row(x_bc, eye):
    return jnp.sum(x_bc * eye, axis=0, keepdims=True)


def _head_rows(blocks):
    return jnp.concatenate(blocks, axis=-1)


def _mlstm_kernel(x_ref, gc_ref, bias_ref, seg_ref, segf_ref, sel_ref, c0_ref, n0_ref, m0_ref,
                  h_ref, c_ref, n_ref, m_ref,
                  dl_scr, emu_scr, num_scr, sum_scr, ka_scr, nloc_scr, mul_scr, cbd_scr, nrow_scr, mrow_scr,
                  *, n_chunks):
    seg = seg_ref[...]
    incl, _, eye = _packed_masks()
    tri64 = _cumsum_matrices()
    lane_head = lax.broadcasted_iota(jnp.int32, (CHUNK, MIX_SEC), 1) // HEAD_DIM

    zero = jnp.zeros((HEAD_DIM, HEAD_DIM), F32)
    for d in range(2):
        for hd in range(HEADS):
            blocks = [zero] * hd + [c0_ref[d * HEADS + hd]] + [zero] * (HEADS - 1 - hd)
            cbd_scr[d, hd * HEAD_DIM:(hd + 1) * HEAD_DIM, :] = _head_rows(blocks)
        n_row = _head_rows([n0_ref[d * HEADS + hd] for hd in range(HEADS)])
        m_row = _head_rows([jnp.broadcast_to(m0_ref[d * HEADS + hd], (1, HEAD_DIM)) for hd in range(HEADS)])
        nrow_scr[d] = jnp.broadcast_to(n_row, (8, MIX_SEC))
        mrow_scr[d] = jnp.broadcast_to(m_row, (8, MIX_SEC))

    def front(c):
        r0 = pl.multiple_of(c * CHUNK, CHUNK)
        x = x_ref[pl.ds(r0, CHUNK), :]
        q = x[:, 0:MIX_SEC]
        k = x[:, MIX_SEC:2 * MIX_SEC] * (HEAD_DIM ** -0.5)
        v16 = x[:, 2 * MIX_SEC:3 * MIX_SEC].astype(BF16)
        k16 = k.astype(BF16)
        qk = lax.dot_general(q.astype(BF16), _block_diag(k16, seg), (((1,), (1,)), ((), ())),
                             preferred_element_type=F32)
        vbd = _block_diag(v16, seg)
        gcol = gc_ref[pl.ds(r0, CHUNK), :]
        i128 = gcol + bias_ref[0:1, :]
        f128 = _log_sigmoid(gcol + bias_ref[1:2, :])
        jobs = []
        for d in range(2):
            i_bc = _exact3(i128, sel_ref[d])
            b_bc = _exact3(_exact3_lhs(tri64[d], f128), sel_ref[2 + d])
            r_bc = i_bc - b_bc
            dmat = jnp.where(incl[d], b_bc + _bc_to_row(r_bc, eye), -jnp.inf)
            mu = [jnp.max(dmat[:, hd * HEAD_DIM:(hd + 1) * HEAD_DIM], axis=-1, keepdims=True) for hd in range(HEADS)]
            mu_bc = jnp.where(lane_head == 0, mu[0], jnp.where(lane_head == 1, mu[1],
                                                               jnp.where(lane_head == 2, mu[2], mu[3])))
            s_loc = qk * jnp.exp(dmat - mu_bc)
            last = CHUNK - 1 if d == 0 else 0
            mu_last = mu_bc[last:last + 1, :]
            ka = k * jnp.exp(b_bc[last:last + 1, :] + r_bc - mu_last)
            idx = d * n_chunks + c
            dl_scr[idx] = b_bc - mu_bc
            emu_scr[idx] = jnp.exp(-mu_bc)
            ka_scr[idx] = ka.astype(BF16)
            nloc_scr[idx] = jnp.broadcast_to(jnp.sum(ka, axis=0, keepdims=True), (8, MIX_SEC))
            mul_scr[idx] = jnp.broadcast_to(mu_last, (8, MIX_SEC))
            jobs.append((idx, s_loc, vbd))
        return jobs

    def prep_pair(i, carry):
        jobs = front(2 * i) + front(2 * i + 1)
        nums = [jnp.dot(s_loc.astype(BF16), vbd, preferred_element_type=F32) for _, s_loc, vbd in jobs]
        sums = [_mm_exact_rhs(s_loc, seg) for _, s_loc, _ in jobs]
        for (idx, _, _), num, ssum in zip(jobs, nums, sums):
            num_scr[idx] = num
            sum_scr[idx] = ssum
        return carry

    lax.fori_loop(0, n_chunks // 2, prep_pair, 0)

    def step(c, carry):
        ccs = (c, n_chunks - 1 - c)
        rows = [pl.ds(pl.multiple_of(ccs[d] * CHUNK, CHUNK), CHUNK) for d in range(2)]
        idxs = [d * n_chunks + ccs[d] for d in range(2)]
        qs = [x_ref[rows[d], 0:MIX_SEC] for d in range(2)]
        v16 = [x_ref[rows[d], 2 * MIX_SEC:3 * MIX_SEC].astype(BF16) for d in range(2)]
        c_prev = [cbd_scr[d] for d in range(2)]
        n_prev = [nrow_scr[d][0:1, :] for d in range(2)]
        m_prev = [mrow_scr[d][0:1, :] for d in range(2)]
        qc = [jnp.dot(qs[d].astype(BF16), c_prev[d].astype(BF16), preferred_element_type=F32) for d in range(2)]
        qn = [_mm_exact_rhs(qs[d] * n_prev[d], seg) for d in range(2)]
        upd = [_mm_tn(ka_scr[idxs[d]], v16[d]) for d in range(2)]
        for d in range(2):
            z = dl_scr[idxs[d]] + m_prev[d]
            zp = jnp.maximum(z, 0.0)
            w_inter = jnp.exp(z - zp)
            corr = jnp.exp(-zp)
            num = w_inter * qc[d] + corr * num_scr[idxs[d]]
            den = w_inter * qn[d] + corr * sum_scr[idxs[d]]
            h_ref[d, rows[d], :] = num / jnp.maximum(jnp.abs(den), emu_scr[idxs[d]] * corr)
            last = CHUNK - 1 if d == 0 else 0
            a_prev = w_inter[last:last + 1, :]
            gamma = corr[last:last + 1, :]
            cbd_scr[d] = a_prev * c_prev[d] + gamma * (upd[d] * segf_ref[...])
            nrow_scr[d] = jnp.broadcast_to(a_prev * n_prev[d] + gamma * nloc_scr[idxs[d]][0:1, :], (8, MIX_SEC))
            mrow_scr[d] = jnp.broadcast_to(mul_scr[idxs[d]][0:1, :] + zp[last:last + 1, :], (8, MIX_SEC))
        return carry

    lax.fori_loop(0, n_chunks, step, 0)

    for d in range(2):
        for hd in range(HEADS):
            sl = slice(hd * HEAD_DIM, (hd + 1) * HEAD_DIM)
            c_ref[d * HEADS + hd] = cbd_scr[d, sl, sl]
            n_ref[d * HEADS + hd] = nrow_scr[d][0:1, sl]
            m_ref[d * HEADS + hd] = mrow_scr[d][0:1, hd * HEAD_DIM:hd * HEAD_DIM + 1]


def _mlstm(ml_qkv, g_col, bias_row, seg, segf, sel, c0, n0, m0):
    B, T, _ = ml_qkv.shape
    nc = T // CHUNK
    per_b = lambda *s: pl.BlockSpec((None,) + s, lambda b: (b,) + (0,) * len(s))
    full = lambda a: pl.BlockSpec(a.shape, lambda b: (0,) * a.ndim)
    st_shapes = [(2 * HEADS, HEAD_DIM, HEAD_DIM), (2 * HEADS, 1, HEAD_DIM), (2 * HEADS, 1, 1)]
    per_chunk = lambda rows, dt: pltpu.VMEM((2 * nc, rows, MIX_SEC), dt)
    return pl.pallas_call(
        functools.partial(_mlstm_kernel, n_chunks=nc),
        grid=(B,),
        in_specs=[per_b(T, 3 * MIX_SEC), per_b(T, LANES), full(bias_row), full(seg), full(segf), full(sel)]
                 + [per_b(*s) for s in st_shapes],
        out_specs=[pl.BlockSpec((2, None, T, MIX_SEC), lambda b: (0, b, 0, 0))] + [per_b(*s) for s in st_shapes],
        out_shape=[jax.ShapeDtypeStruct((2, B, T, MIX_SEC), F32)]
                  + [jax.ShapeDtypeStruct((B,) + s, F32) for s in st_shapes],
        scratch_shapes=[per_chunk(CHUNK, F32), per_chunk(CHUNK, F32), per_chunk(CHUNK, F32), per_chunk(CHUNK, F32),
                        per_chunk(CHUNK, BF16), per_chunk(8, F32), per_chunk(8, F32),
                        pltpu.VMEM((2, MIX_SEC, MIX_SEC), F32), pltpu.VMEM((2, 8, MIX_SEC), F32),
                        pltpu.VMEM((2, 8, MIX_SEC), F32)],
        compiler_params=_cparams(("parallel",)),
        name="mlstm",
    )(ml_qkv, g_col, bias_row, seg, segf, sel, c0, n0, m0)


def _packed_mm3(l_pair, w_pair):
    (lh, ll), (wh, wl) = l_pair, w_pair
    rows = lh.shape[0]
    dot = functools.partial(jnp.dot, preferred_element_type=F32)
    top = dot(jnp.concatenate([lh, ll], axis=0), wh)
    return top[0:rows] + (top[rows:2 * rows] + dot(lh, wl))


def _packed_tri_inverses(a_list, eye, seg):
    a_pairs = [_split(a) for a in a_list]
    ps = [eye - a for a in a_list]
    xs = [_packed_mm3(ap, _block_diag_pair(ap, seg)) for ap in a_pairs]
    for i in range(4):
        x_pairs = [_split(x) for x in xs]
        wxs = [_block_diag_pair(xp, seg) for xp in x_pairs]
        p_pairs = [_split(p) for p in ps]
        both = [_packed_mm3((jnp.concatenate([pp[0], xp[0]], axis=0), jnp.concatenate([pp[1], xp[1]], axis=0)), wx)
                for pp, xp, wx in zip(p_pairs, x_pairs, wxs)]
        ps = [p + pq[0:CHUNK] for p, pq in zip(ps, both)]
        xs = [pq[CHUNK:2 * CHUNK] for pq in both]
    wxs = [_block_diag_pair(_split(x), seg) for x in xs]
    return [p + _packed_mm3(_split(p), wx) for p, wx in zip(ps, wxs)]


def _deltanet_kernel(x_ref, gc_ref, conv_ref, par_row_ref, seg_ref, segf_ref, sel_ref,
                     s0_ref, o_ref, s_ref, u_scr, w_scr, qg_scr, attn_scr, kgt_scr, egl_scr, sbd_scr, *, n_chunks):
    T = n_chunks * CHUNK
    seg = seg_ref[...]
    incl, strict, eye = _packed_masks()
    tri64 = _cumsum_matrices()

    zero = jnp.zeros((HEAD_DIM, HEAD_DIM), F32)
    for d in range(2):
        for hd in range(HEADS):
            blocks = [zero] * hd + [s0_ref[d * HEADS + hd]] + [zero] * (HEADS - 1 - hd)
            sbd_scr[d, hd * HEAD_DIM:(hd + 1) * HEAD_DIM, :] = jnp.concatenate(blocks, axis=-1)

    def front(c):
        r0 = pl.multiple_of(c * CHUNK, CHUNK)
        xc = x_ref[pl.ds(r0, CHUNK), :]
        before = x_ref[pl.ds(pl.multiple_of(jnp.maximum(r0 - 8, 0), 8), 8), :][7:8, :]
        after = x_ref[pl.ds(pl.multiple_of(jnp.minimum(r0 + CHUNK, T - 8), 8), 8), :][0:1, :]
        before = jnp.where(c > 0, before, 0.0)
        after = jnp.where(c < n_chunks - 1, after, 0.0)
        rows = lax.broadcasted_iota(jnp.int32, xc.shape, 0)
        x_prev = jnp.where(rows == 0, before, pltpu.roll(xc, 1, 0))
        x_next = jnp.where(rows == CHUNK - 1, after, pltpu.roll(xc, CHUNK - 1, 0))
        y = _silu(x_prev * conv_ref[0:1, :] + xc * conv_ref[1:2, :] + x_next * conv_ref[2:3, :])
        q = y[:, 0:MIX_SEC]
        k = y[:, MIX_SEC:2 * MIX_SEC]
        v = y[:, 2 * MIX_SEC:3 * MIX_SEC]
        q = q * lax.rsqrt(_head_mean_sq(q, seg_ref) * HEAD_DIM + EPS) * (HEAD_DIM ** -0.5)
        k = k * lax.rsqrt(_head_mean_sq(k, seg_ref) * HEAD_DIM + EPS)
        k16 = k.astype(BF16)
        kq = lax.dot_general(jnp.concatenate([k16, q.astype(BF16)], axis=0), _block_diag(k16, seg),
                             (((1,), (1,)), ((), ())), preferred_element_type=F32)
        kk, qk = kq[0:CHUNK], kq[CHUNK:2 * CHUNK]
        gcol = gc_ref[pl.ds(r0, CHUNK), :]
        beta128 = jax.nn.sigmoid(gcol)
        la128 = par_row_ref[0:1, :] * _softplus(gcol + par_row_ref[1:2, :])
        jobs = []
        for d in range(2):
            g_bc = _exact3(_exact3_lhs(tri64[d], la128), sel_ref[d])
            beta_bc = _exact3(beta128, sel_ref[2 + d])
            decay = jnp.exp(jnp.where(incl[d], g_bc - _bc_to_row(g_bc, eye), -jnp.inf))
            a = jnp.where(strict[d], beta_bc * kk * decay, 0.0)
            eg = jnp.exp(g_bc)
            last = CHUNK - 1 if d == 0 else 0
            g_last = g_bc[last:last + 1, :]
            idx = d * n_chunks + c
            qg_scr[idx] = (q * eg).astype(BF16)
            attn_scr[idx] = (qk * decay).astype(BF16)
            kgt_scr[idx] = (k * jnp.exp(g_last - g_bc)).T.astype(BF16)
            egl_scr[idx] = jnp.broadcast_to(jnp.exp(g_last), (8, MIX_SEC))
            jobs.append((idx, a, _split(v * beta_bc), _split(k * beta_bc * eg)))
        return jobs

    def prep_pair(i, carry):
        jobs = front(2 * i) + front(2 * i + 1)
        t_pairs = [_split(t) for t in _packed_tri_inverses([job[1] for job in jobs], eye, seg)]
        us = [_packed_mm3(tp, _block_diag_pair(job[2], seg)) for tp, job in zip(t_pairs, jobs)]
        ws = [_packed_mm3(tp, _block_diag_pair(job[3], seg)) for tp, job in zip(t_pairs, jobs)]
        for job, u, w in zip(jobs, us, ws):
            u_scr[job[0]] = u
            w_scr[job[0]] = w.astype(BF16)
        return carry

    lax.fori_loop(0, n_chunks // 2, prep_pair, 0)

    def step(c, carry):
        ccs = (c, n_chunks - 1 - c)
        idxs = [d * n_chunks + ccs[d] for d in range(2)]
        s_prev = [sbd_scr[d] for d in range(2)]
        rs = [jnp.dot(jnp.concatenate([w_scr[idxs[d]], qg_scr[idxs[d]]], axis=0), s_prev[d].astype(BF16),
                      preferred_element_type=F32) for d in range(2)]
        v16 = [(u_scr[idxs[d]] - rs[d][0:CHUNK]).astype(BF16) for d in range(2)]
        outs = [rs[d][CHUNK:2 * CHUNK] + jnp.dot(attn_scr[idxs[d]], _block_diag(v16[d], seg),
                                                 preferred_element_type=F32) for d in range(2)]
        upds = [jnp.dot(kgt_scr[idxs[d]], v16[d], preferred_element_type=F32) for d in range(2)]
        for d in range(2):
            o_ref[d, pl.ds(pl.multiple_of(ccs[d] * CHUNK, CHUNK), CHUNK), :] = outs[d]
            sbd_scr[d] = egl_scr[idxs[d]][0:1, :] * s_prev[d] + upds[d] * segf_ref[...]
        return carry

    lax.fori_loop(0, n_chunks, step, 0)

    for d in range(2):
        for hd in range(HEADS):
            sl = slice(hd * HEAD_DIM, (hd + 1) * HEAD_DIM)
            s_ref[d * HEADS + hd] = sbd_scr[d, sl, sl]


def _deltanet(dn_qkv, g_col, conv_w, par_row, seg, segf, sel, s0):
    B, T, _ = dn_qkv.shape
    nc = T // CHUNK
    per_b = lambda *s: pl.BlockSpec((None,) + s, lambda b: (b,) + (0,) * len(s))
    full = lambda a: pl.BlockSpec(a.shape, lambda b: (0,) * a.ndim)
    st = (2 * HEADS, HEAD_DIM, HEAD_DIM)
    per_chunk = lambda rows, cols, dt: pltpu.VMEM((2 * nc, rows, cols), dt)
    return pl.pallas_call(
        functools.partial(_deltanet_kernel, n_chunks=nc),
        grid=(B,),
        in_specs=[per_b(T, 3 * MIX_SEC), per_b(T, LANES), full(conv_w), full(par_row),
                  full(seg), full(segf), full(sel), per_b(*st)],
        out_specs=[pl.BlockSpec((2, None, T, MIX_SEC), lambda b: (0, b, 0, 0)), per_b(*st)],
        out_shape=[jax.ShapeDtypeStruct((2, B, T, MIX_SEC), F32), jax.ShapeDtypeStruct((B,) + st, F32)],
        scratch_shapes=[per_chunk(CHUNK, MIX_SEC, F32), per_chunk(CHUNK, MIX_SEC, BF16),
                        per_chunk(CHUNK, MIX_SEC, BF16), per_chunk(CHUNK, MIX_SEC, BF16),
                        per_chunk(MIX_SEC, CHUNK, BF16), per_chunk(8, MIX_SEC, F32),
                        pltpu.VMEM((2, MIX_SEC, MIX_SEC), F32)],
        compiler_params=_cparams(("parallel",)),
        name="deltanet",
    )(dn_qkv, g_col, conv_w, par_row, seg, segf, sel, s0)


def _fourier_kernel(x_ref, cg_ref, sg_ref, ct_ref, st_ref, o_ref, p_scr, q_scr, *, scale):
    @pl.when(pl.program_id(1) == 0)
    def _():
        x = x_ref[...].astype(BF16)
        p_scr[...] = jnp.dot(x, cg_ref[...], preferred_element_type=F32).astype(BF16)
        q_scr[...] = jnp.dot(x, sg_ref[...], preferred_element_type=F32).astype(BF16)

    y = (jnp.dot(ct_ref[...], p_scr[...], preferred_element_type=F32)
         - jnp.dot(st_ref[...], q_scr[...], preferred_element_type=F32))
    o_ref[...] = y * scale


def _dft_tables(T):
    def tab(n):
        idx = np.arange(n, dtype=np.int64)
        ang = 2.0 * np.pi * ((idx[:, None] * idx[None, :]) % n).astype(np.float64) / n
        return np.cos(ang), np.sin(ang)
    ct, st = tab(T)
    cg1, sg1 = tab(HEAD_DIM)
    eye = np.eye(HEADS)
    cg, sg = np.kron(eye, cg1), np.kron(eye, sg1)
    return tuple(jnp.asarray(a, dtype=F32).astype(BF16) for a in (cg, sg, ct, st))


def _fourier(x):
    B, T, _ = x.shape
    cg, sg, ct, st = _dft_tables(T)
    tr = min(T, 512)
    return pl.pallas_call(
        functools.partial(_fourier_kernel, scale=1.0 / math.sqrt(T * HEAD_DIM)),
        grid=(B, T // tr),
        in_specs=[pl.BlockSpec((None, T, MIX_SEC), lambda b, i: (b, 0, 0)),
                  pl.BlockSpec((MIX_SEC, MIX_SEC), lambda b, i: (0, 0)),
                  pl.BlockSpec((MIX_SEC, MIX_SEC), lambda b, i: (0, 0)),
                  pl.BlockSpec((tr, T), lambda b, i: (i, 0)),
                  pl.BlockSpec((tr, T), lambda b, i: (i, 0))],
        out_specs=pl.BlockSpec((None, tr, MIX_SEC), lambda b, i: (b, i, 0)),
        out_shape=jax.ShapeDtypeStruct((B, T, MIX_SEC), F32),
        scratch_shapes=[pltpu.VMEM((T, MIX_SEC), BF16), pltpu.VMEM((T, MIX_SEC), BF16)],
        compiler_params=_cparams(("parallel", "arbitrary")),
        name="fourier",
    )(x, cg, sg, ct, st)


def _diffattn_kernel(*refs, lam_init, has_ctx):
    if has_ctx:
        q_ref, k_ref, v_ref, kc_ref, vc_ref, lam_ref, o_ref = refs
    else:
        q_ref, k_ref, v_ref, lam_ref, o_ref = refs
    lp = lam_ref[...]
    lam = (jnp.exp(jnp.sum(lp[0:1, :] * lp[1:2, :], axis=-1, keepdims=True))
           - jnp.exp(jnp.sum(lp[2:3, :] * lp[3:4, :], axis=-1, keepdims=True)) + lam_init)
    q = q_ref[...] * ((DA_QK ** -0.5) * LOG2_E)
    pieces = [(k_ref[...].astype(BF16), v_ref[...].astype(BF16))]
    if has_ctx:
        pieces.append((kc_ref[...].astype(BF16), vc_ref[...].astype(BF16)))
    lane = lax.broadcasted_iota(jnp.int32, q.shape, 1)
    out = jnp.zeros(q.shape, F32)
    for hd in range(HEADS):
        probs = []
        for mp in range(2):
            lo = (hd * 2 + mp) * DA_QK
            qm = jnp.where((lane >= lo) & (lane < lo + DA_QK), q, 0.0).astype(BF16)
            ss = [lax.dot_general(qm, kk, (((1,), (1,)), ((), ())), preferred_element_type=F32) for kk, _ in pieces]
            mx = functools.reduce(jnp.maximum, [jnp.max(s, axis=-1, keepdims=True) for s in ss])
            es = [jnp.exp2(s - mx) for s in ss]
            den = functools.reduce(jnp.add, [jnp.sum(e, axis=-1, keepdims=True) for e in es])
            probs.append((es, 1.0 / den))
        (e0, r0), (e1, r1) = probs
        r1 = lam * r1
        acc = None
        for j, (_, vv) in enumerate(pieces):
            w = (e0[j] * r0 - e1[j] * r1).astype(BF16)
            part = jnp.dot(w, vv, preferred_element_type=F32)
            acc = part if acc is None else acc + part
        out = jnp.where((lane >= hd * HEAD_DIM) & (lane < (hd + 1) * HEAD_DIM), acc, out)
    o_ref[...] = out


def _diffattn(q, k, v, ctx, lam_par, lam_init):
    B, T, _ = q.shape
    tq = min(T, 256)
    has_ctx = ctx is not None
    in_specs = [pl.BlockSpec((None, tq, MIX_SEC), lambda b, i: (b, i, 0)),
                pl.BlockSpec((None, T, MIX_SEC), lambda b, i: (b, 0, 0)),
                pl.BlockSpec((None, T, MIX_SEC), lambda b, i: (b, 0, 0))]
    args = [q, k, v]
    if has_ctx:
        ck, cv, li = ctx
        P = ck.shape[2]
        in_specs += [pl.BlockSpec((None, None, P, MIX_SEC), lambda b, i: (b, li, 0, 0))] * 2
        args += [ck, cv]
    in_specs.append(pl.BlockSpec(lam_par.shape, lambda b, i: (0, 0)))
    args.append(lam_par)
    return pl.pallas_call(
        functools.partial(_diffattn_kernel, lam_init=lam_init, has_ctx=has_ctx),
        grid=(B, T // tq),
        in_specs=in_specs,
        out_specs=pl.BlockSpec((None, tq, MIX_SEC), lambda b, i: (b, i, 0)),
        out_shape=jax.ShapeDtypeStruct((B, T, MIX_SEC), F32),
        compiler_params=_cparams(("parallel", "arbitrary")),
        name="diff_attention",
    )(*args)


def _mixout_kernel(x_ref, mod_ref, gpost_ref, hml_ref, og_ref, odn_ref, dng_ref, ft_ref, oda_ref,
                   gains_ref, seg_ref, w_ref, o_ref, *, da_scale):
    def head_norm(z, gain_row):
        return z * lax.rsqrt(_head_mean_sq(z, seg_ref) + EPS) * gain_row

    h_ml = (hml_ref[0] + hml_ref[1]) * jax.nn.sigmoid(og_ref[...])
    y_ml = head_norm(h_ml, gains_ref[0:1, :])
    y_dn = head_norm(odn_ref[0] + odn_ref[1], gains_ref[1:2, :]) * _silu(dng_ref[...])
    y_da = head_norm(oda_ref[...], gains_ref[2:3, :]) * da_scale
    y = (_mm(y_ml, w_ref[0:MIX_SEC, :]) + _mm(y_dn, w_ref[MIX_SEC:2 * MIX_SEC, :])
         + _mm(ft_ref[...], w_ref[2 * MIX_SEC:3 * MIX_SEC, :]) + _mm(y_da, w_ref[3 * MIX_SEC:4 * MIX_SEC, :]))
    o_ref[...] = x_ref[...] + mod_ref[5:6, :] * (_rms(y) * gpost_ref[...])


def _mixout(x, mod, g_post, h_ml, ml, o_dn, dn_g, y_ft, o_da, gains, seg, w_out, da_scale):
    B, T, _ = x.shape
    tm = min(T, 512)
    tpb = T // tm
    n_tok = B * T
    x2 = lambda a: a.reshape(n_tok, a.shape[-1])
    h_ml = h_ml.reshape(2, n_tok, MIX_SEC)
    o_dn = o_dn.reshape(2, n_tok, MIX_SEC)
    tok = lambda w: pl.BlockSpec((tm, w), lambda i: (i, 0))
    two = pl.BlockSpec((2, tm, MIX_SEC), lambda i: (0, i, 0))
    full = lambda a: pl.BlockSpec(a.shape, lambda i: (0,) * a.ndim)
    out = pl.pallas_call(
        functools.partial(_mixout_kernel, da_scale=da_scale),
        grid=(n_tok // tm,),
        in_specs=[tok(D_MODEL), pl.BlockSpec((None, N_MOD, D_MODEL), lambda i: (i // tpb, 0, 0)), full(g_post),
                  two, tok(MIX_SEC), two, tok(MIX_SEC), tok(MIX_SEC), tok(MIX_SEC),
                  full(gains), full(seg), full(w_out)],
        out_specs=tok(D_MODEL),
        out_shape=jax.ShapeDtypeStruct((n_tok, D_MODEL), F32),
        compiler_params=_cparams(("parallel",)),
        name="mix_out",
    )(x2(x), mod, g_post, h_ml, x2(ml), o_dn, x2(dn_g), x2(y_ft), x2(o_da), gains, seg, w_out)
    return out.reshape(x.shape)


def _rope_tables(T):
    t = np.arange(T)
    row = (t // GRID_W).astype(np.float64)
    col = (t % GRID_W).astype(np.float64)
    inv = ROPE_BASE ** (-np.arange(ROPE_AXIS_PAIRS, dtype=np.float64) / ROPE_AXIS_PAIRS)
    ang = np.concatenate([row[:, None] * inv, col[:, None] * inv], axis=-1)
    cos = np.repeat(np.cos(ang), 2, axis=-1)
    sin = np.repeat(np.sin(ang), 2, axis=-1) * np.tile(np.array([-1.0, 1.0]), DA_QK // 2)
    reps = MIX_SEC // DA_QK
    return jnp.asarray(np.tile(cos, (1, reps)), dtype=F32), jnp.asarray(np.tile(sin, (1, reps)), dtype=F32)


def _layer_params(li, w_in, w_out, ffn_w_in, ffn_w_out, norm_pre, norm_post, ml_i_bias, ml_f_bias, ml_norm,
                  dn_conv, dn_A_log, dn_dt_bias, dn_norm, da_lambda, da_norm):
    w = w_in[li]
    main = jnp.concatenate([w[:, _OFF_ML:_OFF_MLG], w[:, _OFF_DN:_OFF_DNG], w[:, _OFF_FT:IN_WIDTH]], axis=1)
    gate = jnp.concatenate([w[:, _OFF_MLG:_OFF_DN], w[:, _OFF_DNG:_OFF_FT]], axis=1)
    on_lanes = lambda vec, lo: jnp.pad(vec.reshape(-1), (lo, LANES - lo - 2 * HEADS))
    ml_bias = jnp.stack([on_lanes(ml_i_bias[li], 0), on_lanes(ml_f_bias[li], 8)])
    dn_par_row = jnp.stack([on_lanes(-jnp.exp(dn_A_log[li]), 24), on_lanes(dn_dt_bias[li], 24)])
    return dict(
        w_main=main.astype(BF16),
        w_gate=jnp.pad(gate, ((0, 0), (0, LANES - N_GATE))).astype(BF16),
        w_out=w_out[li].astype(BF16),
        ffn_w_in=[ffn_w_in[li, j].astype(BF16) for j in range(2)],
        ffn_w_out=[ffn_w_out[li, j].astype(BF16) for j in range(2)],
        norm_pre=[norm_pre[li, j][None, :] for j in range(3)],
        norm_post=[norm_post[li, j][None, :] for j in range(3)],
        ml_bias_row=ml_bias,
        dn_par_row=dn_par_row,
        dn_conv=dn_conv[li],
        gains=jnp.stack([ml_norm[li].reshape(-1), jnp.tile(dn_norm[li], HEADS), jnp.tile(da_norm[li], HEADS)]),
        da_lambda=da_lambda[li],
    )


def _packed_consts():
    seg = np.kron(np.eye(HEADS), np.ones((HEAD_DIM, HEAD_DIM)))

    def spread(first_lanes):
        sel = np.zeros((4, LANES, MIX_SEC))
        for j, lo in enumerate(first_lanes):
            for hd in range(HEADS):
                sel[j, lo + hd, hd * HEAD_DIM:(hd + 1) * HEAD_DIM] = 1.0
        return jnp.asarray(sel, dtype=BF16)

    return dict(seg=jnp.asarray(seg, dtype=BF16), segf=jnp.asarray(seg, dtype=F32),
                sel_ml=spread((0, HEADS, 8, 8 + HEADS)), sel_dn=spread((24, 24 + HEADS, 16, 16 + HEADS)))


def _mixing(x_tok, mod_tok, shape_bt, lp, li, rope_tabs, states, ctx, consts):
    B, T = shape_bt
    seg, segf = consts['seg'], consts['segf']
    lam_init = 0.8 - 0.6 * math.exp(-0.3 * li)
    ml, ml_o, dn, dn_g, ft, da_q, da_k, da_v, g_col = _inproj(
        x_tok, mod_tok, lp['norm_pre'][1], lp['w_main'], lp['w_gate'], rope_tabs)
    seq = lambda a: a.reshape(B, T, a.shape[-1])
    c0, n0, m0, s0 = states
    h_ml, c_f, n_f, m_f = _mlstm(seq(ml), seq(g_col), lp['ml_bias_row'], seg, segf, consts['sel_ml'], c0, n0, m0)
    o_dn, s_f = _deltanet(seq(dn), seq(g_col), lp['dn_conv'], lp['dn_par_row'], seg, segf, consts['sel_dn'], s0)
    y_ft = _fourier(seq(ft))
    o_da = _diffattn(seq(da_q), seq(da_k), seq(da_v), ctx, lp['da_lambda'], lam_init)
    x_new = _mixout(x_tok, mod_tok, lp['norm_post'][1], h_ml, ml_o, o_dn, dn_g, y_ft, o_da, lp['gains'], seg,
                    lp['w_out'], 1.0 - lam_init)
    return x_new, (seq(da_k), seq(da_v), c_f, n_f, m_f, s_f)


def _trunk_layer(x_tok, mod_tok, shape_bt, lp, li, rope_tabs, states, ctx, consts):
    x_tok = _ffn_block(x_tok, mod_tok, lp['norm_pre'][0], lp['norm_post'][0], lp['ffn_w_in'][0], lp['ffn_w_out'][0], 0)
    x_tok, aux = _mixing(x_tok, mod_tok, shape_bt, lp, li, rope_tabs, states, ctx, consts)
    x_tok = _ffn_block(x_tok, mod_tok, lp['norm_pre'][2], lp['norm_post'][2], lp['ffn_w_in'][1], lp['ffn_w_out'][1], 6)
    return x_tok, aux


def kernel(x_prompt, x_sample, cache_k, cache_v, state_mlstm_C, state_mlstm_n, state_mlstm_m, state_delta_S, c, c_ctx, w_mod, b_mod, norm_pre, norm_post, ffn_w_in, ffn_w_out, w_in, w_out, ml_i_bias, ml_f_bias, ml_norm, dn_conv, dn_A_log, dn_dt_bias, dn_norm, da_lambda, da_norm):
    Bp, Tp, _ = x_prompt.shape
    Bs, Ts, _ = x_sample.shape
    P = cache_k.shape[2]
    n_st = 2 * HEADS

    cond = jnp.concatenate([c, c_ctx[None, :]], axis=0)
    cond = jnp.pad(cond, ((0, (-cond.shape[0]) % 8), (0, 0)))
    mod = _modulation(cond, w_mod, b_mod)
    rope_tabs = _rope_tables(Ts)
    consts = _packed_consts()
    ck = cache_k.reshape(Bs, DEPTH, P, MIX_SEC)
    cv = cache_v.reshape(Bs, DEPTH, P, MIX_SEC)
    zeros_p = (jnp.zeros((Bp, n_st, HEAD_DIM, HEAD_DIM), F32), jnp.zeros((Bp, n_st, 1, HEAD_DIM), F32),
               jnp.zeros((Bp, n_st, 1, 1), F32), jnp.zeros((Bp, n_st, HEAD_DIM, HEAD_DIM), F32))

    xp = x_prompt.reshape(1, Bp * Tp, D_MODEL)
    xs = x_sample
    ks, vs, Cs, ns, ms, Ss = [], [], [], [], [], []
    for li in range(DEPTH):
        lp = _layer_params(li, w_in, w_out, ffn_w_in, ffn_w_out, norm_pre, norm_post, ml_i_bias, ml_f_bias, ml_norm,
                           dn_conv, dn_A_log, dn_dt_bias, dn_norm, da_lambda, da_norm)
        mod_s = jnp.transpose(mod[li, :, :Bs], (1, 0, 2))
        mod_p = mod[li, :, Bs:Bs + 1].reshape(1, N_MOD, D_MODEL)
        xp, (k_c, v_c, C_c, n_c, m_c, S_c) = _trunk_layer(xp, mod_p, (Bp, Tp), lp, li, None, zeros_p, None, consts)
        ks.append(k_c.reshape(Bp, Tp, HEADS, 2, DA_QK))
        vs.append(v_c.reshape(Bp, Tp, HEADS, HEAD_DIM))
        Cs.append(C_c.reshape(Bp, 2, HEADS, HEAD_DIM, HEAD_DIM))
        ns.append(n_c.reshape(Bp, 2, HEADS, HEAD_DIM))
        ms.append(m_c.reshape(Bp, 2, HEADS))
        Ss.append(S_c.reshape(Bp, 2, HEADS, HEAD_DIM, HEAD_DIM))
        states = (state_mlstm_C[:, li].reshape(Bs, n_st, HEAD_DIM, HEAD_DIM),
                  state_mlstm_n[:, li].reshape(Bs, n_st, 1, HEAD_DIM),
                  state_mlstm_m[:, li].reshape(Bs, n_st, 1, 1),
                  state_delta_S[:, li].reshape(Bs, n_st, HEAD_DIM, HEAD_DIM))
        xs, _ = _trunk_layer(xs, mod_s, (Bs, Ts), lp, li, rope_tabs, states, (ck, cv, li), consts)
    return (xp.reshape(Bp, Tp, D_MODEL), xs, jnp.stack(ks, axis=1), jnp.stack(vs, axis=1), jnp.stack(Cs, axis=1),
            jnp.stack(ns, axis=1), jnp.stack(ms, axis=1), jnp.stack(Ss, axis=1))
```

```python
import functools
import math

import numpy as np
import jax
import jax.numpy as jnp
from jax import lax
from jax.experimental import pallas as pl
from jax.experimental.pallas import tpu as pltpu

D_MODEL = 1024
DEPTH = 2
GRID_W = 64
HEADS = 4
HEAD_DIM = 64
MIX_SEC = HEADS * HEAD_DIM
DA_QK = 32
CONV_K = 3
D_FF = ((8 * D_MODEL // 3 + 127) // 128) * 128
N_MOD = 9
CHUNK = 64
N_GATE = 32
ROPE_AXIS_PAIRS = DA_QK // 4
ROPE_BASE = 10000.0
EPS = 1e-6
HALF = 0.5
LOG2_E = 1.4426950408889634
F32 = jnp.float32
BF16 = jnp.bfloat16
LANES = 128
VMEM_LIMIT = 56 * 1024 * 1024

_OFF_ML = 0
_OFF_MLG = 4 * MIX_SEC
_OFF_DN = _OFF_MLG + 16
_OFF_DNG = _OFF_DN + 4 * MIX_SEC
_OFF_FT = _OFF_DNG + 16
_OFF_DA = _OFF_FT + MIX_SEC
IN_WIDTH = _OFF_DA + 3 * MIX_SEC
MAIN_WIDTH = 12 * MIX_SEC


def _cparams(sem):
    return pltpu.CompilerParams(dimension_semantics=sem, vmem_limit_bytes=VMEM_LIMIT)


def _mm(a, b):
    return jnp.dot(a.astype(BF16), b.astype(BF16), preferred_element_type=F32)


def _mm_nt(a, b):
    return lax.dot_general(a.astype(BF16), b.astype(BF16), (((1,), (1,)), ((), ())), preferred_element_type=F32)


def _mm_tn(a, b):
    return lax.dot_general(a.astype(BF16), b.astype(BF16), (((0,), (0,)), ((), ())), preferred_element_type=F32)


def _split(a):
    hi = a.astype(BF16)
    lo = (a - hi.astype(F32)).astype(BF16)
    return hi, lo


def _mm3(a, b):
    ah, al = _split(a)
    bh, bl = _split(b)
    dot = functools.partial(jnp.dot, preferred_element_type=F32)
    return dot(ah, bh) + (dot(ah, bl) + dot(al, bh))


def _mm_exact_rhs(a, b_bf16):
    ah, al = _split(a)
    dot = functools.partial(jnp.dot, preferred_element_type=F32)
    return dot(ah, b_bf16) + dot(al, b_bf16)


def _mm_exact_lhs(a_bf16, b):
    bh, bl = _split(b)
    dot = functools.partial(jnp.dot, preferred_element_type=F32)
    return dot(a_bf16, bh) + dot(a_bf16, bl)


def _rms(x):
    return x * lax.rsqrt(jnp.mean(x * x, axis=-1, keepdims=True) + EPS)


def _silu(x):
    return x * jax.nn.sigmoid(x)


def _log_sigmoid(x):
    return jnp.minimum(x, 0.0) - jnp.log1p(jnp.exp(-jnp.abs(x)))


def _softplus(x):
    return jnp.maximum(x, 0.0) + jnp.log1p(jnp.exp(-jnp.abs(x)))


def _head_mean_sq(x, seg_ref):
    return _mm_exact_rhs(x * x, seg_ref[...]) * (1.0 / HEAD_DIM)


def _tri_masks():
    row = lax.broadcasted_iota(jnp.int32, (CHUNK, CHUNK), 0)
    col = lax.broadcasted_iota(jnp.int32, (CHUNK, CHUNK), 1)
    return row >= col, row <= col, row > col, row < col


def _mod_kernel(c_ref, w_ref, b_ref, o_ref):
    c = c_ref[...]
    o_ref[...] = _mm(_silu(c), w_ref[...]) + b_ref[...]


def _modulation(cond, w_mod, b_mod):
    R = cond.shape[0]
    return pl.pallas_call(
        _mod_kernel,
        grid=(DEPTH, N_MOD),
        in_specs=[pl.BlockSpec((R, D_MODEL), lambda l, j: (0, 0)),
                  pl.BlockSpec((None, D_MODEL, D_MODEL), lambda l, j: (l, 0, j)),
                  pl.BlockSpec((None, None, 1, D_MODEL), lambda l, j: (l, j, 0, 0))],
        out_specs=pl.BlockSpec((None, None, R, D_MODEL), lambda l, j: (l, j, 0, 0)),
        out_shape=jax.ShapeDtypeStruct((DEPTH, N_MOD, R, D_MODEL), F32),
        compiler_params=_cparams(("arbitrary", "arbitrary")),
        name="modulation",
    )(cond, w_mod, b_mod.reshape(DEPTH, N_MOD, 1, D_MODEL))


FFN_SLICE = 256


def _ffn_kernel(x_ref, mod_ref, gpre_ref, gpost_ref, win_ref, wout_ref, o_ref, *, m0):
    x = x_ref[...]
    h = (_rms(x) * gpre_ref[...] * (1.0 + mod_ref[m0 + 1:m0 + 2, :]) + mod_ref[m0:m0 + 1, :]).astype(BF16)
    acc = None
    for lo in range(0, D_FF, FFN_SLICE):
        a = jnp.dot(h, win_ref[:, lo:lo + FFN_SLICE], preferred_element_type=F32)
        b = jnp.dot(h, win_ref[:, D_FF + lo:D_FF + lo + FFN_SLICE], preferred_element_type=F32)
        part = _mm(_silu(a) * b, wout_ref[lo:lo + FFN_SLICE, :])
        acc = part if acc is None else acc + part
    o_ref[...] = x + HALF * mod_ref[m0 + 2:m0 + 3, :] * (_rms(acc) * gpost_ref[...])


def _ffn_block(x, mod, g_pre, g_post, w_in, w_out, m0):
    B, T, _ = x.shape
    tm = min(T, 512)
    tpb = T // tm
    resident = lambda a: pl.BlockSpec(a.shape, lambda i: (0, 0), pipeline_mode=pl.Buffered(1))
    return pl.pallas_call(
        functools.partial(_ffn_kernel, m0=m0),
        grid=(B * tpb,),
        in_specs=[pl.BlockSpec((None, tm, D_MODEL), lambda i: (i // tpb, i % tpb, 0)),
                  pl.BlockSpec((None, N_MOD, D_MODEL), lambda i: (i // tpb, 0, 0)),
                  pl.BlockSpec((1, D_MODEL), lambda i: (0, 0)),
                  pl.BlockSpec((1, D_MODEL), lambda i: (0, 0)),
                  resident(w_in), resident(w_out)],
        out_specs=pl.BlockSpec((None, tm, D_MODEL), lambda i: (i // tpb, i % tpb, 0)),
        out_shape=jax.ShapeDtypeStruct(x.shape, F32),
        compiler_params=_cparams(("parallel",)),
        name="ffn_block",
    )(x, mod, g_pre, g_post, w_in, w_out)


def _rope(x, cos, sin_signed):
    lane = lax.broadcasted_iota(jnp.int32, x.shape, 1)
    nxt = pltpu.roll(x, x.shape[1] - 1, 1)
    prv = pltpu.roll(x, 1, 1)
    swapped = jnp.where(lane % 2 == 0, nxt, prv)
    return x * cos + swapped * sin_signed


def _inproj_kernel(*refs, rope):
    if rope:
        (x_ref, mod_ref, g_ref, w_ref, wg_ref, cos_ref, sin_ref,
         ml_ref, mlo_ref, dn_ref, dng_ref, ft_ref, q_ref, k_ref, v_ref, gc_ref) = refs
    else:
        (x_ref, mod_ref, g_ref, w_ref, wg_ref,
         ml_ref, mlo_ref, dn_ref, dng_ref, ft_ref, q_ref, k_ref, v_ref, gc_ref) = refs
    h = (_rms(x_ref[...]) * g_ref[...] * (1.0 + mod_ref[4:5, :]) + mod_ref[3:4, :]).astype(BF16)

    def proj(lo, width):
        return jnp.dot(h, w_ref[:, lo:lo + width], preferred_element_type=F32)

    ml_ref[...] = proj(0, 3 * MIX_SEC)
    mlo_ref[...] = proj(3 * MIX_SEC, MIX_SEC)
    dn_ref[...] = proj(4 * MIX_SEC, 3 * MIX_SEC)
    dng_ref[...] = proj(7 * MIX_SEC, MIX_SEC)
    ft_ref[...] = proj(8 * MIX_SEC, MIX_SEC).astype(ft_ref.dtype)
    q = proj(9 * MIX_SEC, MIX_SEC)
    k = proj(10 * MIX_SEC, MIX_SEC)
    if rope:
        q = _rope(q, cos_ref[...], sin_ref[...])
        k = _rope(k, cos_ref[...], sin_ref[...])
    q_ref[...] = (q * ((DA_QK ** -0.5) * LOG2_E)).astype(q_ref.dtype)
    k_ref[...] = k.astype(k_ref.dtype)
    v_ref[...] = proj(11 * MIX_SEC, MIX_SEC).astype(v_ref.dtype)
    gc_ref[...] = jnp.dot(h, wg_ref[...], preferred_element_type=F32)


def _inproj(x, mod, g_pre, w_main, w_gate, rope_tabs):
    B, T, _ = x.shape
    tm = min(T, 512)
    tpb = T // tm
    rope = rope_tabs is not None
    tok = lambda w: pl.BlockSpec((None, tm, w), lambda i: (i // tpb, i % tpb, 0))
    full = lambda a: pl.BlockSpec(a.shape, lambda i: (0,) * a.ndim)
    in_specs = [tok(D_MODEL), pl.BlockSpec((None, N_MOD, D_MODEL), lambda i: (i // tpb, 0, 0)),
                full(g_pre), full(w_main), full(w_gate)]
    args = [x, mod, g_pre, w_main, w_gate]
    if rope:
        in_specs += [pl.BlockSpec((tm, MIX_SEC), lambda i: (i % tpb, 0))] * 2
        args += list(rope_tabs)
    widths = (3 * MIX_SEC, MIX_SEC, 3 * MIX_SEC, MIX_SEC, MIX_SEC, MIX_SEC, MIX_SEC, MIX_SEC, LANES)
    kv_dtype = BF16 if rope else F32
    dtypes = (F32, F32, F32, F32, BF16, BF16, kv_dtype, kv_dtype, F32)
    out_specs = [tok(w) for w in widths]
    out_shape = [jax.ShapeDtypeStruct((B, T, w), dt) for w, dt in zip(widths, dtypes)]
    return pl.pallas_call(
        functools.partial(_inproj_kernel, rope=rope),
        grid=(B * tpb,),
        in_specs=in_specs, out_specs=out_specs, out_shape=out_shape,
        compiler_params=_cparams(("parallel",)),
        name="in_proj",
    )(*args)


PREP_GROUP = 4


def _spread(a, b_bf16):
    return _mm_exact_rhs(a, b_bf16)


def _spread_lhs(a_bf16, b):
    return _mm_exact_lhs(a_bf16, b)


def _block_diag(x, seg):
    return jnp.concatenate([x] * HEADS, axis=0) * seg


def _block_diag_pair(pair, seg):
    return _block_diag(pair[0], seg), _block_diag(pair[1], seg)


def _packed_masks():
    row = lax.broadcasted_iota(jnp.int32, (CHUNK, MIX_SEC), 0)
    pos = lax.broadcasted_iota(jnp.int32, (CHUNK, MIX_SEC), 1) % CHUNK
    return (pos <= row, pos >= row), (pos < row, pos > row), jnp.where(pos == row, 1.0, 0.0)


def _cumsum_matrices():
    incl_lo, incl_up, _, _ = _tri_masks()
    return incl_lo.astype(BF16), incl_up.astype(BF16)


def _bc_to_row(x_bc, eye):
    return jnp.sum(x_bc * eye, axis=0, keepdims=True)


def _head_rows(blocks):
    return jnp.concatenate(blocks, axis=-1)


def _mlstm_kernel(x_ref, gc_ref, bias_ref, seg_ref, segf_ref, sel_ref, c0_ref, n0_ref, m0_ref,
                  h_ref, c_ref, n_ref, m_ref,
                  dl_scr, emu_scr, num_scr, sum_scr, ka_scr, nloc_scr, mul_scr, cbd_scr, nrow_scr, mrow_scr,
                  *, n_chunks):
    seg = seg_ref[...]
    incl, _, eye = _packed_masks()
    tri64 = _cumsum_matrices()
    lane_head = lax.broadcasted_iota(jnp.int32, (CHUNK, MIX_SEC), 1) // HEAD_DIM

    zero = jnp.zeros((HEAD_DIM, HEAD_DIM), F32)
    for d in range(2):
        for hd in range(HEADS):
            blocks = [zero] * hd + [c0_ref[d * HEADS + hd]] + [zero] * (HEADS - 1 - hd)
            cbd_scr[d, hd * HEAD_DIM:(hd + 1) * HEAD_DIM, :] = _head_rows(blocks)
        n_row = _head_rows([n0_ref[d * HEADS + hd] for hd in range(HEADS)])
        m_row = _head_rows([jnp.broadcast_to(m0_ref[d * HEADS + hd], (1, HEAD_DIM)) for hd in range(HEADS)])
        nrow_scr[d] = jnp.broadcast_to(n_row, (8, MIX_SEC))
        mrow_scr[d] = jnp.broadcast_to(m_row, (8, MIX_SEC))

    def prep_group(i, carry):
        chunks = [PREP_GROUP * i + j for j in range(PREP_GROUP)]
        jobs = [(ci, d) for ci in range(PREP_GROUP) for d in range(2)]
        rows = [pl.ds(pl.multiple_of(c * CHUNK, CHUNK), CHUNK) for c in chunks]
        xs = [x_ref[r, :] for r in rows]
        qs = [x[:, 0:MIX_SEC] for x in xs]
        ks = [x[:, MIX_SEC:2 * MIX_SEC] * (HEAD_DIM ** -0.5) for x in xs]
        vbd = [_block_diag(x[:, 2 * MIX_SEC:3 * MIX_SEC].astype(BF16), seg) for x in xs]
        gcols = [gc_ref[r, :] for r in rows]
        i128 = [g + bias_ref[0:1, :] for g in gcols]
        f128 = [_log_sigmoid(g + bias_ref[1:2, :]) for g in gcols]
        cums = [_spread_lhs(tri64[d], f128[ci]) for ci, d in jobs]
        qk = [lax.dot_general(q.astype(BF16), _block_diag(k.astype(BF16), seg), (((1,), (1,)), ((), ())),
                              preferred_element_type=F32) for q, k in zip(qs, ks)]
        b_bc = [_spread(cum, sel_ref[2 + d]) for cum, (ci, d) in zip(cums, jobs)]
        i_bc = [_spread(i128[ci], sel_ref[d]) for ci, d in jobs]
        s_locs = []
        for (ci, d), b, ib in zip(jobs, b_bc, i_bc):
            r_bc = ib - b
            dmat = jnp.where(incl[d], b + _bc_to_row(r_bc, eye), -jnp.inf)
            mu = [jnp.max(dmat[:, hd * HEAD_DIM:(hd + 1) * HEAD_DIM], axis=-1, keepdims=True) for hd in range(HEADS)]
            mu_bc = jnp.where(lane_head == 0, mu[0], jnp.where(lane_head == 1, mu[1],
                                                               jnp.where(lane_head == 2, mu[2], mu[3])))
            s_locs.append(qk[ci] * jnp.exp(dmat - mu_bc))
            last = CHUNK - 1 if d == 0 else 0
            mu_last = mu_bc[last:last + 1, :]
            ka = ks[ci] * jnp.exp(b[last:last + 1, :] + r_bc - mu_last)
            idx = d * n_chunks + chunks[ci]
            dl_scr[idx] = b - mu_bc
            emu_scr[idx] = jnp.exp(-mu_bc)
            ka_scr[idx] = ka.astype(BF16)
            nloc_scr[idx] = jnp.broadcast_to(jnp.sum(ka, axis=0, keepdims=True), (8, MIX_SEC))
            mul_scr[idx] = jnp.broadcast_to(mu_last, (8, MIX_SEC))
        nums = [jnp.dot(s.astype(BF16), vbd[ci], preferred_element_type=F32) for s, (ci, d) in zip(s_locs, jobs)]
        sums = [_mm_exact_rhs(s, seg) for s in s_locs]
        for (ci, d), num, ssum in zip(jobs, nums, sums):
            num_scr[d * n_chunks + chunks[ci], :, :] = num
            sum_scr[d * n_chunks + chunks[ci], :, :] = ssum
        return carry

    lax.fori_loop(0, n_chunks // PREP_GROUP, prep_group, 0)

    def step(c, carry):
        ccs = (c, n_chunks - 1 - c)
        rows = [pl.ds(pl.multiple_of(ccs[d] * CHUNK, CHUNK), CHUNK) for d in range(2)]
        idxs = [d * n_chunks + ccs[d] for d in range(2)]
        qs = [x_ref[rows[d], 0:MIX_SEC] for d in range(2)]
        v16 = [x_ref[rows[d], 2 * MIX_SEC:3 * MIX_SEC].astype(BF16) for d in range(2)]
        c_prev = [cbd_scr[d] for d in range(2)]
        n_prev = [nrow_scr[d][0:1, :] for d in range(2)]
        m_prev = [mrow_scr[d][0:1, :] for d in range(2)]
        qc = [jnp.dot(qs[d].astype(BF16), c_prev[d].astype(BF16), preferred_element_type=F32) for d in range(2)]
        qn = [_mm_exact_rhs(qs[d] * n_prev[d], seg) for d in range(2)]
        upd = [_mm_tn(ka_scr[idxs[d]], v16[d]) for d in range(2)]
        for d in range(2):
            z = dl_scr[idxs[d]] + m_prev[d]
            zp = jnp.maximum(z, 0.0)
            w_inter = jnp.exp(z - zp)
            corr = jnp.exp(-zp)
            num = w_inter * qc[d] + corr * num_scr[idxs[d]]
            den = w_inter * qn[d] + corr * sum_scr[idxs[d]]
            h_ref[d, rows[d], :] = num / jnp.maximum(jnp.abs(den), emu_scr[idxs[d]] * corr)
            last = CHUNK - 1 if d == 0 else 0
            a_prev = w_inter[last:last + 1, :]
            gamma = corr[last:last + 1, :]
            cbd_scr[d] = a_prev * c_prev[d] + gamma * (upd[d] * segf_ref[...])
            nrow_scr[d] = jnp.broadcast_to(a_prev * n_prev[d] + gamma * nloc_scr[idxs[d]][0:1, :], (8, MIX_SEC))
            mrow_scr[d] = jnp.broadcast_to(mul_scr[idxs[d]][0:1, :] + zp[last:last + 1, :], (8, MIX_SEC))
        return carry

    lax.fori_loop(0, n_chunks, step, 0)

    for d in range(2):
        for hd in range(HEADS):
            sl = slice(hd * HEAD_DIM, (hd + 1) * HEAD_DIM)
            c_ref[d * HEADS + hd] = cbd_scr[d, sl, sl]
            n_ref[d * HEADS + hd] = nrow_scr[d][0:1, sl]
            m_ref[d * HEADS + hd] = mrow_scr[d][0:1, hd * HEAD_DIM:hd * HEAD_DIM + 1]


def _mlstm(ml_qkv, g_col, bias_row, seg, segf, sel, c0, n0, m0):
    B, T, _ = ml_qkv.shape
    nc = T // CHUNK
    per_b = lambda *s: pl.BlockSpec((None,) + s, lambda b: (b,) + (0,) * len(s))
    full = lambda a: pl.BlockSpec(a.shape, lambda b: (0,) * a.ndim)
    st_shapes = [(2 * HEADS, HEAD_DIM, HEAD_DIM), (2 * HEADS, 1, HEAD_DIM), (2 * HEADS, 1, 1)]
    per_chunk = lambda rows, dt: pltpu.VMEM((2 * nc, rows, MIX_SEC), dt)
    return pl.pallas_call(
        functools.partial(_mlstm_kernel, n_chunks=nc),
        grid=(B,),
        in_specs=[per_b(T, 3 * MIX_SEC), per_b(T, LANES), full(bias_row), full(seg), full(segf), full(sel)]
                 + [per_b(*s) for s in st_shapes],
        out_specs=[pl.BlockSpec((2, None, T, MIX_SEC), lambda b: (0, b, 0, 0))] + [per_b(*s) for s in st_shapes],
        out_shape=[jax.ShapeDtypeStruct((2, B, T, MIX_SEC), F32)]
                  + [jax.ShapeDtypeStruct((B,) + s, F32) for s in st_shapes],
        scratch_shapes=[per_chunk(CHUNK, F32), per_chunk(CHUNK, F32), per_chunk(CHUNK, F32), per_chunk(CHUNK, F32),
                        per_chunk(CHUNK, BF16), per_chunk(8, F32), per_chunk(8, F32),
                        pltpu.VMEM((2, MIX_SEC, MIX_SEC), F32), pltpu.VMEM((2, 8, MIX_SEC), F32),
                        pltpu.VMEM((2, 8, MIX_SEC), F32)],
        compiler_params=_cparams(("parallel",)),
        name="mlstm",
    )(ml_qkv, g_col, bias_row, seg, segf, sel, c0, n0, m0)


def _packed_mm3(l_pair, w_pair):
    (lh, ll), (wh, wl) = l_pair, w_pair
    rows = lh.shape[0]
    dot = functools.partial(jnp.dot, preferred_element_type=F32)
    top = dot(jnp.concatenate([lh, ll], axis=0), wh)
    return top[0:rows] + (top[rows:2 * rows] + dot(lh, wl))


def _packed_tri_inverses(a_list, eye, seg):
    a_pairs = [_split(a) for a in a_list]
    ps = [eye - a for a in a_list]
    xs = [_packed_mm3(ap, _block_diag_pair(ap, seg)) for ap in a_pairs]
    for i in range(4):
        x_pairs = [_split(x) for x in xs]
        wxs = [_block_diag_pair(xp, seg) for xp in x_pairs]
        p_pairs = [_split(p) for p in ps]
        both = [_packed_mm3((jnp.concatenate([pp[0], xp[0]], axis=0), jnp.concatenate([pp[1], xp[1]], axis=0)), wx)
                for pp, xp, wx in zip(p_pairs, x_pairs, wxs)]
        ps = [p + pq[0:CHUNK] for p, pq in zip(ps, both)]
        xs = [pq[CHUNK:2 * CHUNK] for pq in both]
    wxs = [_block_diag_pair(_split(x), seg) for x in xs]
    return [p + _packed_mm3(_split(p), wx) for p, wx in zip(ps, wxs)]


def _deltanet_kernel(x_ref, gc_ref, conv_ref, par_row_ref, seg_ref, segf_ref, sel_ref,
                     s0_ref, o_ref, s_ref, u_scr, w_scr, qg_scr, attn_scr, kgt_scr, egl_scr, sbd_scr, *, n_chunks):
    T = n_chunks * CHUNK
    seg = seg_ref[...]
    incl, strict, eye = _packed_masks()
    tri64 = _cumsum_matrices()

    zero = jnp.zeros((HEAD_DIM, HEAD_DIM), F32)
    for d in range(2):
        for hd in range(HEADS):
            blocks = [zero] * hd + [s0_ref[d * HEADS + hd]] + [zero] * (HEADS - 1 - hd)
            sbd_scr[d, hd * HEAD_DIM:(hd + 1) * HEAD_DIM, :] = jnp.concatenate(blocks, axis=-1)

    def conv_silu(c):
        r0 = pl.multiple_of(c * CHUNK, CHUNK)
        xc = x_ref[pl.ds(r0, CHUNK), :]
        before = x_ref[pl.ds(pl.multiple_of(jnp.maximum(r0 - 8, 0), 8), 8), :][7:8, :]
        after = x_ref[pl.ds(pl.multiple_of(jnp.minimum(r0 + CHUNK, T - 8), 8), 8), :][0:1, :]
        before = jnp.where(c > 0, before, 0.0)
        after = jnp.where(c < n_chunks - 1, after, 0.0)
        rows = lax.broadcasted_iota(jnp.int32, xc.shape, 0)
        x_prev = jnp.where(rows == 0, before, pltpu.roll(xc, 1, 0))
        x_next = jnp.where(rows == CHUNK - 1, after, pltpu.roll(xc, CHUNK - 1, 0))
        return _silu(x_prev * conv_ref[0:1, :] + xc * conv_ref[1:2, :] + x_next * conv_ref[2:3, :])

    def prep_group(i, carry):
        chunks = [PREP_GROUP * i + j for j in range(PREP_GROUP)]
        jobs = [(ci, d) for ci in range(PREP_GROUP) for d in range(2)]
        ys = [conv_silu(c) for c in chunks]
        qs = [y[:, 0:MIX_SEC] for y in ys]
        ks = [y[:, MIX_SEC:2 * MIX_SEC] for y in ys]
        vs = [y[:, 2 * MIX_SEC:3 * MIX_SEC] for y in ys]
        gcols = [gc_ref[pl.ds(pl.multiple_of(c * CHUNK, CHUNK), CHUNK), :] for c in chunks]
        beta128 = [jax.nn.sigmoid(g) for g in gcols]
        la128 = [par_row_ref[0:1, :] * _softplus(g + par_row_ref[1:2, :]) for g in gcols]
        q_ss = [_head_mean_sq(q, seg_ref) for q in qs]
        k_ss = [_head_mean_sq(k, seg_ref) for k in ks]
        cums = [_spread_lhs(tri64[d], la128[ci]) for ci, d in jobs]
        qs = [q * lax.rsqrt(ss * HEAD_DIM + EPS) * (HEAD_DIM ** -0.5) for q, ss in zip(qs, q_ss)]
        ks = [k * lax.rsqrt(ss * HEAD_DIM + EPS) for k, ss in zip(ks, k_ss)]
        k16 = [k.astype(BF16) for k in ks]
        kq = [lax.dot_general(jnp.concatenate([kb, q.astype(BF16)], axis=0), _block_diag(kb, seg),
                              (((1,), (1,)), ((), ())), preferred_element_type=F32) for kb, q in zip(k16, qs)]
        g_bc = [_spread(cum, sel_ref[d]) for cum, (ci, d) in zip(cums, jobs)]
        beta_bc = [_spread(beta128[ci], sel_ref[2 + d]) for ci, d in jobs]
        a_list, rhs_u, rhs_w = [], [], []
        for (ci, d), g, beta in zip(jobs, g_bc, beta_bc):
            kk, qk = kq[ci][0:CHUNK], kq[ci][CHUNK:2 * CHUNK]
            decay = jnp.exp(jnp.where(incl[d], g - _bc_to_row(g, eye), -jnp.inf))
            a_list.append(jnp.where(strict[d], beta * kk * decay, 0.0))
            eg = jnp.exp(g)
            last = CHUNK - 1 if d == 0 else 0
            g_last = g[last:last + 1, :]
            idx = d * n_chunks + chunks[ci]
            qg_scr[idx] = (qs[ci] * eg).astype(BF16)
            attn_scr[idx] = (qk * decay).astype(BF16)
            kgt_scr[idx] = (ks[ci] * jnp.exp(g_last - g)).T.astype(BF16)
            egl_scr[idx] = jnp.broadcast_to(jnp.exp(g_last), (8, MIX_SEC))
            rhs_u.append(_split(vs[ci] * beta))
            rhs_w.append(_split(ks[ci] * beta * eg))
        t_pairs = [_split(t) for t in _packed_tri_inverses(a_list, eye, seg)]
        us = [_packed_mm3(tp, _block_diag_pair(rhs, seg)) for tp, rhs in zip(t_pairs, rhs_u)]
        ws = [_packed_mm3(tp, _block_diag_pair(rhs, seg)) for tp, rhs in zip(t_pairs, rhs_w)]
        for (ci, d), u, w in zip(jobs, us, ws):
            u_scr[d * n_chunks + chunks[ci], :, :] = u
            w_scr[d * n_chunks + chunks[ci], :, :] = w.astype(BF16)
        return carry

    lax.fori_loop(0, n_chunks // PREP_GROUP, prep_group, 0)

    def step(c, carry):
        ccs = (c, n_chunks - 1 - c)
        idxs = [d * n_chunks + ccs[d] for d in range(2)]
        s_prev = [sbd_scr[d] for d in range(2)]
        rs = [jnp.dot(jnp.concatenate([w_scr[idxs[d]], qg_scr[idxs[d]]], axis=0), s_prev[d].astype(BF16),
                      preferred_element_type=F32) for d in range(2)]
        v16 = [(u_scr[idxs[d]] - rs[d][0:CHUNK]).astype(BF16) for d in range(2)]
        outs = [rs[d][CHUNK:2 * CHUNK] + jnp.dot(attn_scr[idxs[d]], _block_diag(v16[d], seg),
                                                 preferred_element_type=F32) for d in range(2)]
        upds = [jnp.dot(kgt_scr[idxs[d]], v16[d], preferred_element_type=F32) for d in range(2)]
        for d in range(2):
            o_ref[d, pl.ds(pl.multiple_of(ccs[d] * CHUNK, CHUNK), CHUNK), :] = outs[d]
            sbd_scr[d] = egl_scr[idxs[d]][0:1, :] * s_prev[d] + upds[d] * segf_ref[...]
        return carry

    lax.fori_loop(0, n_chunks, step, 0)

    for d in range(2):
        for hd in range(HEADS):
            sl = slice(hd * HEAD_DIM, (hd + 1) * HEAD_DIM)
            s_ref[d * HEADS + hd] = sbd_scr[d, sl, sl]


def _deltanet(dn_qkv, g_col, conv_w, par_row, seg, segf, sel, s0):
    B, T, _ = dn_qkv.shape
    nc = T // CHUNK
    per_b = lambda *s: pl.BlockSpec((None,) + s, lambda b: (b,) + (0,) * len(s))
    full = lambda a: pl.BlockSpec(a.shape, lambda b: (0,) * a.ndim)
    st = (2 * HEADS, HEAD_DIM, HEAD_DIM)
    per_chunk = lambda rows, cols, dt: pltpu.VMEM((2 * nc, rows, cols), dt)
    return pl.pallas_call(
        functools.partial(_deltanet_kernel, n_chunks=nc),
        grid=(B,),
        in_specs=[per_b(T, 3 * MIX_SEC), per_b(T, LANES), full(conv_w), full(par_row),
                  full(seg), full(segf), full(sel), per_b(*st)],
        out_specs=[pl.BlockSpec((2, None, T, MIX_SEC), lambda b: (0, b, 0, 0)), per_b(*st)],
        out_shape=[jax.ShapeDtypeStruct((2, B, T, MIX_SEC), F32), jax.ShapeDtypeStruct((B,) + st, F32)],
        scratch_shapes=[per_chunk(CHUNK, MIX_SEC, F32), per_chunk(CHUNK, MIX_SEC, BF16),
                        per_chunk(CHUNK, MIX_SEC, BF16), per_chunk(CHUNK, MIX_SEC, BF16),
                        per_chunk(MIX_SEC, CHUNK, BF16), per_chunk(8, MIX_SEC, F32),
                        pltpu.VMEM((2, MIX_SEC, MIX_SEC), F32)],
        compiler_params=_cparams(("parallel",)),
        name="deltanet",
    )(dn_qkv, g_col, conv_w, par_row, seg, segf, sel, s0)


def _fourier_kernel(x_ref, cg_ref, sg_ref, ct_ref, st_ref, o_ref, p_scr, q_scr, *, scale):
    @pl.when(pl.program_id(1) == 0)
    def _():
        x = x_ref[...].astype(BF16)
        p_scr[...] = jnp.dot(x, cg_ref[...], preferred_element_type=F32).astype(BF16)
        q_scr[...] = jnp.dot(x, sg_ref[...], preferred_element_type=F32).astype(BF16)

    y = (jnp.dot(ct_ref[...], p_scr[...], preferred_element_type=F32)
         - jnp.dot(st_ref[...], q_scr[...], preferred_element_type=F32))
    o_ref[...] = y * scale


def _dft_tables(T):
    def tab(n):
        idx = np.arange(n, dtype=np.int64)
        ang = 2.0 * np.pi * ((idx[:, None] * idx[None, :]) % n).astype(np.float64) / n
        return np.cos(ang), np.sin(ang)
    ct, st = tab(T)
    cg1, sg1 = tab(HEAD_DIM)
    eye = np.eye(HEADS)
    cg, sg = np.kron(eye, cg1), np.kron(eye, sg1)
    return tuple(jnp.asarray(a, dtype=F32).astype(BF16) for a in (cg, sg, ct, st))


def _fourier(x):
    B, T, _ = x.shape
    cg, sg, ct, st = _dft_tables(T)
    tr = min(T, 512)
    return pl.pallas_call(
        functools.partial(_fourier_kernel, scale=1.0 / math.sqrt(T * HEAD_DIM)),
        grid=(B, T // tr),
        in_specs=[pl.BlockSpec((None, T, MIX_SEC), lambda b, i: (b, 0, 0)),
                  pl.BlockSpec((MIX_SEC, MIX_SEC), lambda b, i: (0, 0)),
                  pl.BlockSpec((MIX_SEC, MIX_SEC), lambda b, i: (0, 0)),
                  pl.BlockSpec((tr, T), lambda b, i: (i, 0)),
                  pl.BlockSpec((tr, T), lambda b, i: (i, 0))],
        out_specs=pl.BlockSpec((None, tr, MIX_SEC), lambda b, i: (b, i, 0)),
        out_shape=jax.ShapeDtypeStruct((B, T, MIX_SEC), F32),
        scratch_shapes=[pltpu.VMEM((T, MIX_SEC), BF16), pltpu.VMEM((T, MIX_SEC), BF16)],
        compiler_params=_cparams(("parallel", "arbitrary")),
        name="fourier",
    )(x, cg, sg, ct, st)


def _diffattn_kernel(*refs, lam_init, has_ctx):
    if has_ctx:
        q_ref, k_ref, v_ref, kc_ref, vc_ref, lam_ref, o_ref = refs
    else:
        q_ref, k_ref, v_ref, lam_ref, o_ref = refs
    lp = lam_ref[...]
    lam = (jnp.exp(jnp.sum(lp[0:1, :] * lp[1:2, :], axis=-1, keepdims=True))
           - jnp.exp(jnp.sum(lp[2:3, :] * lp[3:4, :], axis=-1, keepdims=True)) + lam_init)
    q = q_ref[...]
    pieces = [(k_ref[...].astype(BF16), v_ref[...].astype(BF16))]
    if has_ctx:
        pieces.append((kc_ref[...].astype(BF16), vc_ref[...].astype(BF16)))
    lane = lax.broadcasted_iota(jnp.int32, q.shape, 1)

    def scores(hd):
        maps = []
        for mp in range(2):
            lo = (hd * 2 + mp) * DA_QK
            qm = jnp.where((lane >= lo) & (lane < lo + DA_QK), q, jnp.zeros_like(q))
            maps.append([lax.dot_general(qm, kk, (((1,), (1,)), ((), ())), preferred_element_type=F32)
                         for kk, _ in pieces])
        return maps

    out = jnp.zeros(q.shape, F32)
    s_next = scores(0)
    for hd in range(HEADS):
        s_cur = s_next
        if hd + 1 < HEADS:
            s_next = scores(hd + 1)
        probs = []
        for mp in range(2):
            ss = s_cur[mp]
            mx = functools.reduce(jnp.maximum, [jnp.max(s, axis=-1, keepdims=True) for s in ss])
            es = [jnp.exp2(s - mx) for s in ss]
            den = functools.reduce(jnp.add, [jnp.sum(e, axis=-1, keepdims=True) for e in es])
            probs.append((es, 1.0 / den))
        (e0, r0), (e1, r1) = probs
        r1 = lam * r1
        acc = None
        for j, (_, vv) in enumerate(pieces):
            w = (e0[j] * r0 - e1[j] * r1).astype(BF16)
            part = jnp.dot(w, vv, preferred_element_type=F32)
            acc = part if acc is None else acc + part
        out = jnp.where((lane >= hd * HEAD_DIM) & (lane < (hd + 1) * HEAD_DIM), acc, out)
    o_ref[...] = out


def _diffattn(q, k, v, ctx, lam_par, lam_init):
    B, T, _ = q.shape
    tq = min(T, 256)
    has_ctx = ctx is not None
    in_specs = [pl.BlockSpec((None, tq, MIX_SEC), lambda b, i: (b, i, 0)),
                pl.BlockSpec((None, T, MIX_SEC), lambda b, i: (b, 0, 0)),
                pl.BlockSpec((None, T, MIX_SEC), lambda b, i: (b, 0, 0))]
    args = [q, k, v]
    if has_ctx:
        ck, cv, li = ctx
        P = ck.shape[2]
        in_specs += [pl.BlockSpec((None, None, P, MIX_SEC), lambda b, i: (b, li, 0, 0))] * 2
        args += [ck, cv]
    in_specs.append(pl.BlockSpec(lam_par.shape, lambda b, i: (0, 0)))
    args.append(lam_par)
    return pl.pallas_call(
        functools.partial(_diffattn_kernel, lam_init=lam_init, has_ctx=has_ctx),
        grid=(B, T // tq),
        in_specs=in_specs,
        out_specs=pl.BlockSpec((None, tq, MIX_SEC), lambda b, i: (b, i, 0)),
        out_shape=jax.ShapeDtypeStruct((B, T, MIX_SEC), F32),
        compiler_params=_cparams(("parallel", "arbitrary")),
        name="diff_attention",
    )(*args)


def _mixout_kernel(x_ref, mod_ref, gpost_ref, hml_ref, og_ref, odn_ref, dng_ref, ft_ref, oda_ref,
                   gains_ref, seg_ref, w_ref, o_ref, *, da_scale):
    def head_norm(z, gain_row):
        return z * lax.rsqrt(_head_mean_sq(z, seg_ref) + EPS) * gain_row

    h_ml = (hml_ref[0] + hml_ref[1]) * jax.nn.sigmoid(og_ref[...])
    y_ml = head_norm(h_ml, gains_ref[0:1, :])
    y_dn = head_norm(odn_ref[0] + odn_ref[1], gains_ref[1:2, :]) * _silu(dng_ref[...])
    y_da = head_norm(oda_ref[...], gains_ref[2:3, :]) * da_scale
    y = (_mm(y_ml, w_ref[0:MIX_SEC, :]) + _mm(y_dn, w_ref[MIX_SEC:2 * MIX_SEC, :])
         + _mm(ft_ref[...], w_ref[2 * MIX_SEC:3 * MIX_SEC, :]) + _mm(y_da, w_ref[3 * MIX_SEC:4 * MIX_SEC, :]))
    o_ref[...] = x_ref[...] + mod_ref[5:6, :] * (_rms(y) * gpost_ref[...])


def _mixout(x, mod, g_post, h_ml, ml, o_dn, dn_g, y_ft, o_da, gains, seg, w_out, da_scale):
    B, T, _ = x.shape
    tm = min(T, 512)
    tpb = T // tm
    n_tok = B * T
    x2 = lambda a: a.reshape(n_tok, a.shape[-1])
    h_ml = h_ml.reshape(2, n_tok, MIX_SEC)
    o_dn = o_dn.reshape(2, n_tok, MIX_SEC)
    tok = lambda w: pl.BlockSpec((tm, w), lambda i: (i, 0))
    two = pl.BlockSpec((2, tm, MIX_SEC), lambda i: (0, i, 0))
    full = lambda a: pl.BlockSpec(a.shape, lambda i: (0,) * a.ndim)
    out = pl.pallas_call(
        functools.partial(_mixout_kernel, da_scale=da_scale),
        grid=(n_tok // tm,),
        in_specs=[tok(D_MODEL), pl.BlockSpec((None, N_MOD, D_MODEL), lambda i: (i // tpb, 0, 0)), full(g_post),
                  two, tok(MIX_SEC), two, tok(MIX_SEC), tok(MIX_SEC), tok(MIX_SEC),
                  full(gains), full(seg), full(w_out)],
        out_specs=tok(D_MODEL),
        out_shape=jax.ShapeDtypeStruct((n_tok, D_MODEL), F32),
        compiler_params=_cparams(("parallel",)),
        name="mix_out",
    )(x2(x), mod, g_post, h_ml, x2(ml), o_dn, x2(dn_g), x2(y_ft), x2(o_da), gains, seg, w_out)
    return out.reshape(x.shape)


def _rope_tables(T):
    t = np.arange(T)
    row = (t // GRID_W).astype(np.float64)
    col = (t % GRID_W).astype(np.float64)
    inv = ROPE_BASE ** (-np.arange(ROPE_AXIS_PAIRS, dtype=np.float64) / ROPE_AXIS_PAIRS)
    ang = np.concatenate([row[:, None] * inv, col[:, None] * inv], axis=-1)
    cos = np.repeat(np.cos(ang), 2, axis=-1)
    sin = np.repeat(np.sin(ang), 2, axis=-1) * np.tile(np.array([-1.0, 1.0]), DA_QK // 2)
    reps = MIX_SEC // DA_QK
    return jnp.asarray(np.tile(cos, (1, reps)), dtype=F32), jnp.asarray(np.tile(sin, (1, reps)), dtype=F32)


def _layer_params(li, w_in, w_out, ffn_w_in, ffn_w_out, norm_pre, norm_post, ml_i_bias, ml_f_bias, ml_norm,
                  dn_conv, dn_A_log, dn_dt_bias, dn_norm, da_lambda, da_norm):
    w = w_in[li]
    main = jnp.concatenate([w[:, _OFF_ML:_OFF_MLG], w[:, _OFF_DN:_OFF_DNG], w[:, _OFF_FT:IN_WIDTH]], axis=1)
    gate = jnp.concatenate([w[:, _OFF_MLG:_OFF_DN], w[:, _OFF_DNG:_OFF_FT]], axis=1)
    on_lanes = lambda vec, lo: jnp.pad(vec.reshape(-1), (lo, LANES - lo - 2 * HEADS))
    ml_bias = jnp.stack([on_lanes(ml_i_bias[li], 0), on_lanes(ml_f_bias[li], 8)])
    dn_par_row = jnp.stack([on_lanes(-jnp.exp(dn_A_log[li]), 24), on_lanes(dn_dt_bias[li], 24)])
    return dict(
        w_main=main.astype(BF16),
        w_gate=jnp.pad(gate, ((0, 0), (0, LANES - N_GATE))).astype(BF16),
        w_out=w_out[li].astype(BF16),
        ffn_w_in=[ffn_w_in[li, j].astype(BF16) for j in range(2)],
        ffn_w_out=[ffn_w_out[li, j].astype(BF16) for j in range(2)],
        norm_pre=[norm_pre[li, j][None, :] for j in range(3)],
        norm_post=[norm_post[li, j][None, :] for j in range(3)],
        ml_bias_row=ml_bias,
        dn_par_row=dn_par_row,
        dn_conv=dn_conv[li],
        gains=jnp.stack([ml_norm[li].reshape(-1), jnp.tile(dn_norm[li], HEADS), jnp.tile(da_norm[li], HEADS)]),
        da_lambda=da_lambda[li],
    )


def _packed_consts():
    seg = np.kron(np.eye(HEADS), np.ones((HEAD_DIM, HEAD_DIM)))

    def spread(first_lanes):
        sel = np.zeros((4, LANES, MIX_SEC))
        for j, lo in enumerate(first_lanes):
            for hd in range(HEADS):
                sel[j, lo + hd, hd * HEAD_DIM:(hd + 1) * HEAD_DIM] = 1.0
        return jnp.asarray(sel, dtype=BF16)

    return dict(seg=jnp.asarray(seg, dtype=BF16), segf=jnp.asarray(seg, dtype=F32),
                sel_ml=spread((0, HEADS, 8, 8 + HEADS)), sel_dn=spread((24, 24 + HEADS, 16, 16 + HEADS)))


def _mixing(x_tok, mod_tok, shape_bt, lp, li, rope_tabs, states, ctx, consts):
    B, T = shape_bt
    seg, segf = consts['seg'], consts['segf']
    lam_init = 0.8 - 0.6 * math.exp(-0.3 * li)
    ml, ml_o, dn, dn_g, ft, da_q, da_k, da_v, g_col = _inproj(
        x_tok, mod_tok, lp['norm_pre'][1], lp['w_main'], lp['w_gate'], rope_tabs)
    seq = lambda a: a.reshape(B, T, a.shape[-1])
    c0, n0, m0, s0 = states
    h_ml, c_f, n_f, m_f = _mlstm(seq(ml), seq(g_col), lp['ml_bias_row'], seg, segf, consts['sel_ml'], c0, n0, m0)
    o_dn, s_f = _deltanet(seq(dn), seq(g_col), lp['dn_conv'], lp['dn_par_row'], seg, segf, consts['sel_dn'], s0)
    y_ft = _fourier(seq(ft))
    o_da = _diffattn(seq(da_q), seq(da_k), seq(da_v), ctx, lp['da_lambda'], lam_init)
    x_new = _mixout(x_tok, mod_tok, lp['norm_post'][1], h_ml, ml_o, o_dn, dn_g, y_ft, o_da, lp['gains'], seg,
                    lp['w_out'], 1.0 - lam_init)
    return x_new, (seq(da_k), seq(da_v), c_f, n_f, m_f, s_f)


def _trunk_layer(x_tok, mod_tok, shape_bt, lp, li, rope_tabs, states, ctx, consts):
    x_tok = _ffn_block(x_tok, mod_tok, lp['norm_pre'][0], lp['norm_post'][0], lp['ffn_w_in'][0], lp['ffn_w_out'][0], 0)
    x_tok, aux = _mixing(x_tok, mod_tok, shape_bt, lp, li, rope_tabs, states, ctx, consts)
    x_tok = _ffn_block(x_tok, mod_tok, lp['norm_pre'][2], lp['norm_post'][2], lp['ffn_w_in'][1], lp['ffn_w_out'][1], 6)
    return x_tok, aux


def kernel(x_prompt, x_sample, cache_k, cache_v, state_mlstm_C, state_mlstm_n, state_mlstm_m, state_delta_S, c, c_ctx, w_mod, b_mod, norm_pre, norm_post, ffn_w_in, ffn_w_out, w_in, w_out, ml_i_bias, ml_f_bias, ml_norm, dn_conv, dn_A_log, dn_dt_bias, dn_norm, da_lambda, da_norm):
    Bp, Tp, _ = x_prompt.shape
    Bs, Ts, _ = x_sample.shape
    P = cache_k.shape[2]
    n_st = 2 * HEADS

    cond = jnp.concatenate([c, c_ctx[None, :]], axis=0)
    cond = jnp.pad(cond, ((0, (-cond.shape[0]) % 8), (0, 0)))
    mod = _modulation(cond, w_mod, b_mod)
    rope_tabs = _rope_tables(Ts)
    consts = _packed_consts()
    ck = cache_k.reshape(Bs, DEPTH, P, MIX_SEC)
    cv = cache_v.reshape(Bs, DEPTH, P, MIX_SEC)
    zeros_p = (jnp.zeros((Bp, n_st, HEAD_DIM, HEAD_DIM), F32), jnp.zeros((Bp, n_st, 1, HEAD_DIM), F32),
               jnp.zeros((Bp, n_st, 1, 1), F32), jnp.zeros((Bp, n_st, HEAD_DIM, HEAD_DIM), F32))

    xp = x_prompt.reshape(1, Bp * Tp, D_MODEL)
    xs = x_sample
    ks, vs, Cs, ns, ms, Ss = [], [], [], [], [], []
    for li in range(DEPTH):
        lp = _layer_params(li, w_in, w_out, ffn_w_in, ffn_w_out, norm_pre, norm_post, ml_i_bias, ml_f_bias, ml_norm,
                           dn_conv, dn_A_log, dn_dt_bias, dn_norm, da_lambda, da_norm)
        mod_s = jnp.transpose(mod[li, :, :Bs], (1, 0, 2))
        mod_p = mod[li, :, Bs:Bs + 1].reshape(1, N_MOD, D_MODEL)
        xp, (k_c, v_c, C_c, n_c, m_c, S_c) = _trunk_layer(xp, mod_p, (Bp, Tp), lp, li, None, zeros_p, None, consts)
        ks.append(k_c.reshape(Bp, Tp, HEADS, 2, DA_QK))
        vs.append(v_c.reshape(Bp, Tp, HEADS, HEAD_DIM))
        Cs.append(C_c.reshape(Bp, 2, HEADS, HEAD_DIM, HEAD_DIM))
        ns.append(n_c.reshape(Bp, 2, HEADS, HEAD_DIM))
        ms.append(m_c.reshape(Bp, 2, HEADS))
        Ss.append(S_c.reshape(Bp, 2, HEADS, HEAD_DIM, HEAD_DIM))
        states = (state_mlstm_C[:, li].reshape(Bs, n_st, HEAD_DIM, HEAD_DIM),
                  state_mlstm_n[:, li].reshape(Bs, n_st, 1, HEAD_DIM),
                  state_mlstm_m[:, li].reshape(Bs, n_st, 1, 1),
                  state_delta_S[:, li].reshape(Bs, n_st, HEAD_DIM, HEAD_DIM))
        xs, _ = _trunk_layer(xs, mod_s, (Bs, Ts), lp, li, rope_tabs, states, (ck, cv, li), consts)
    return (xp.reshape(Bp, Tp, D_MODEL), xs, jnp.stack(ks, axis=1), jnp.stack(vs, axis=1), jnp.stack(Cs, axis=1),
            jnp.stack(ns, axis=1), jnp.stack(ms, axis=1), jnp.stack(Ss, axis=1))
```

```python
import functools
import math

import numpy as np
import jax
import jax.numpy as jnp
from jax import lax
from jax.experimental import pallas as pl
from jax.experimental.pallas import tpu as pltpu

D_MODEL = 1024
DEPTH = 2
GRID_W = 64
HEADS = 4
HEAD_DIM = 64
MIX_SEC = HEADS * HEAD_DIM
DA_QK = 32
CONV_K = 3
D_FF = ((8 * D_MODEL // 3 + 127) // 128) * 128
N_MOD = 9
CHUNK = 64
N_GATE = 32
ROPE_AXIS_PAIRS = DA_QK // 4
ROPE_BASE = 10000.0
EPS = 1e-6
HALF = 0.5
LOG2_E = 1.4426950408889634
F32 = jnp.float32
BF16 = jnp.bfloat16
LANES = 128
VMEM_LIMIT = 56 * 1024 * 1024

_OFF_ML = 0
_OFF_MLG = 4 * MIX_SEC
_OFF_DN = _OFF_MLG + 16
_OFF_DNG = _OFF_DN + 4 * MIX_SEC
_OFF_FT = _OFF_DNG + 16
_OFF_DA = _OFF_FT + MIX_SEC
IN_WIDTH = _OFF_DA + 3 * MIX_SEC
MAIN_WIDTH = 12 * MIX_SEC


def _cparams(sem):
    return pltpu.CompilerParams(dimension_semantics=sem, vmem_limit_bytes=VMEM_LIMIT)


def _mm(a, b):
    return jnp.dot(a.astype(BF16), b.astype(BF16), preferred_element_type=F32)


def _mm_nt(a, b):
    return lax.dot_general(a.astype(BF16), b.astype(BF16), (((1,), (1,)), ((), ())), preferred_element_type=F32)


def _mm_tn(a, b):
    return lax.dot_general(a.astype(BF16), b.astype(BF16), (((0,), (0,)), ((), ())), preferred_element_type=F32)


def _split(a):
    hi = a.astype(BF16)
    lo = (a - hi.astype(F32)).astype(BF16)
    return hi, lo


def _mm3(a, b):
    ah, al = _split(a)
    bh, bl = _split(b)
    dot = functools.partial(jnp.dot, preferred_element_type=F32)
    return dot(ah, bh) + (dot(ah, bl) + dot(al, bh))


def _mm_exact_rhs(a, b_bf16):
    ah, al = _split(a)
    dot = functools.partial(jnp.dot, preferred_element_type=F32)
    return dot(ah, b_bf16) + dot(al, b_bf16)


def _mm_exact_lhs(a_bf16, b):
    bh, bl = _split(b)
    dot = functools.partial(jnp.dot, preferred_element_type=F32)
    return dot(a_bf16, bh) + dot(a_bf16, bl)


def _rms(x):
    return x * lax.rsqrt(jnp.mean(x * x, axis=-1, keepdims=True) + EPS)


def _silu(x):
    return x * jax.nn.sigmoid(x)


def _log_sigmoid(x):
    return jnp.minimum(x, 0.0) - jnp.log1p(jnp.exp(-jnp.abs(x)))


def _softplus(x):
    return jnp.maximum(x, 0.0) + jnp.log1p(jnp.exp(-jnp.abs(x)))


def _head_mean_sq(x, seg_ref):
    return _mm_exact_rhs(x * x, seg_ref[...]) * (1.0 / HEAD_DIM)


def _tri_masks():
    row = lax.broadcasted_iota(jnp.int32, (CHUNK, CHUNK), 0)
    col = lax.broadcasted_iota(jnp.int32, (CHUNK, CHUNK), 1)
    return row >= col, row <= col, row > col, row < col


def _mod_kernel(c_ref, w_ref, b_ref, o_ref):
    c = c_ref[...]
    o_ref[...] = _mm(_silu(c), w_ref[...]) + b_ref[...]


def _modulation(cond, w_mod, b_mod):
    R = cond.shape[0]
    return pl.pallas_call(
        _mod_kernel,
        grid=(DEPTH, N_MOD),
        in_specs=[pl.BlockSpec((R, D_MODEL), lambda l, j: (0, 0)),
                  pl.BlockSpec((None, D_MODEL, D_MODEL), lambda l, j: (l, 0, j)),
                  pl.BlockSpec((None, None, 1, D_MODEL), lambda l, j: (l, j, 0, 0))],
        out_specs=pl.BlockSpec((None, None, R, D_MODEL), lambda l, j: (l, j, 0, 0)),
        out_shape=jax.ShapeDtypeStruct((DEPTH, N_MOD, R, D_MODEL), F32),
        compiler_params=_cparams(("arbitrary", "arbitrary")),
        name="modulation",
    )(cond, w_mod, b_mod.reshape(DEPTH, N_MOD, 1, D_MODEL))


FFN_SLICE = 256


def _ffn_kernel(x_ref, mod_ref, gpre_ref, gpost_ref, win_ref, wout_ref, o_ref, *, m0):
    x = x_ref[...]
    h = (_rms(x) * gpre_ref[...] * (1.0 + mod_ref[m0 + 1:m0 + 2, :]) + mod_ref[m0:m0 + 1, :]).astype(BF16)
    acc = None
    for lo in range(0, D_FF, FFN_SLICE):
        a = jnp.dot(h, win_ref[:, lo:lo + FFN_SLICE], preferred_element_type=F32)
        b = jnp.dot(h, win_ref[:, D_FF + lo:D_FF + lo + FFN_SLICE], preferred_element_type=F32)
        part = _mm(_silu(a) * b, wout_ref[lo:lo + FFN_SLICE, :])
        acc = part if acc is None else acc + part
    o_ref[...] = x + HALF * mod_ref[m0 + 2:m0 + 3, :] * (_rms(acc) * gpost_ref[...])


def _ffn_block(x, mod, g_pre, g_post, w_in, w_out, m0):
    B, T, _ = x.shape
    tm = min(T, 512)
    tpb = T // tm
    resident = lambda a: pl.BlockSpec(a.shape, lambda i: (0, 0), pipeline_mode=pl.Buffered(1))
    return pl.pallas_call(
        functools.partial(_ffn_kernel, m0=m0),
        grid=(B * tpb,),
        in_specs=[pl.BlockSpec((None, tm, D_MODEL), lambda i: (i // tpb, i % tpb, 0)),
                  pl.BlockSpec((None, N_MOD, D_MODEL), lambda i: (i // tpb, 0, 0)),
                  pl.BlockSpec((1, D_MODEL), lambda i: (0, 0)),
                  pl.BlockSpec((1, D_MODEL), lambda i: (0, 0)),
                  resident(w_in), resident(w_out)],
        out_specs=pl.BlockSpec((None, tm, D_MODEL), lambda i: (i // tpb, i % tpb, 0)),
        out_shape=jax.ShapeDtypeStruct(x.shape, F32),
        compiler_params=_cparams(("parallel",)),
        name="ffn_block",
    )(x, mod, g_pre, g_post, w_in, w_out)


def _rope(x, cos, sin_signed):
    lane = lax.broadcasted_iota(jnp.int32, x.shape, 1)
    nxt = pltpu.roll(x, x.shape[1] - 1, 1)
    prv = pltpu.roll(x, 1, 1)
    swapped = jnp.where(lane % 2 == 0, nxt, prv)
    return x * cos + swapped * sin_signed


def _inproj_kernel(*refs, rope):
    if rope:
        (x_ref, mod_ref, g_ref, w_ref, wg_ref, cos_ref, sin_ref,
         ml_ref, mlo_ref, dn_ref, dng_ref, ft_ref, q_ref, k_ref, v_ref, gc_ref) = refs
    else:
        (x_ref, mod_ref, g_ref, w_ref, wg_ref,
         ml_ref, mlo_ref, dn_ref, dng_ref, ft_ref, q_ref, k_ref, v_ref, gc_ref) = refs
    h = (_rms(x_ref[...]) * g_ref[...] * (1.0 + mod_ref[4:5, :]) + mod_ref[3:4, :]).astype(BF16)

    def proj(lo, width):
        return jnp.dot(h, w_ref[:, lo:lo + width], preferred_element_type=F32)

    ml_ref[...] = proj(0, 3 * MIX_SEC)
    mlo_ref[...] = proj(3 * MIX_SEC, MIX_SEC)
    dn_ref[...] = proj(4 * MIX_SEC, 3 * MIX_SEC)
    dng_ref[...] = proj(7 * MIX_SEC, MIX_SEC)
    ft_ref[...] = proj(8 * MIX_SEC, MIX_SEC).astype(ft_ref.dtype)
    q = proj(9 * MIX_SEC, MIX_SEC)
    k = proj(10 * MIX_SEC, MIX_SEC)
    if rope:
        q = _rope(q, cos_ref[...], sin_ref[...])
        k = _rope(k, cos_ref[...], sin_ref[...])
    q_ref[...] = (q * ((DA_QK ** -0.5) * LOG2_E)).astype(q_ref.dtype)
    k_ref[...] = k.astype(k_ref.dtype)
    v_ref[...] = proj(11 * MIX_SEC, MIX_SEC).astype(v_ref.dtype)
    gc_ref[...] = jnp.dot(h, wg_ref[...], preferred_element_type=F32)


def _inproj(x, mod, g_pre, w_main, w_gate, rope_tabs):
    B, T, _ = x.shape
    tm = min(T, 512)
    tpb = T // tm
    rope = rope_tabs is not None
    tok = lambda w: pl.BlockSpec((None, tm, w), lambda i: (i // tpb, i % tpb, 0))
    full = lambda a: pl.BlockSpec(a.shape, lambda i: (0,) * a.ndim)
    in_specs = [tok(D_MODEL), pl.BlockSpec((None, N_MOD, D_MODEL), lambda i: (i // tpb, 0, 0)),
                full(g_pre), full(w_main), full(w_gate)]
    args = [x, mod, g_pre, w_main, w_gate]
    if rope:
        in_specs += [pl.BlockSpec((tm, MIX_SEC), lambda i: (i % tpb, 0))] * 2
        args += list(rope_tabs)
    widths = (3 * MIX_SEC, MIX_SEC, 3 * MIX_SEC, MIX_SEC, MIX_SEC, MIX_SEC, MIX_SEC, MIX_SEC, LANES)
    kv_dtype = BF16 if rope else F32
    dtypes = (F32, F32, F32, F32, BF16, BF16, kv_dtype, kv_dtype, F32)
    out_specs = [tok(w) for w in widths]
    out_shape = [jax.ShapeDtypeStruct((B, T, w), dt) for w, dt in zip(widths, dtypes)]
    return pl.pallas_call(
        functools.partial(_inproj_kernel, rope=rope),
        grid=(B * tpb,),
        in_specs=in_specs, out_specs=out_specs, out_shape=out_shape,
        compiler_params=_cparams(("parallel",)),
        name="in_proj",
    )(*args)


PREP_GROUP = 4


def _spread(a, b_bf16):
    return _mm_exact_rhs(a, b_bf16)


def _spread_lhs(a_bf16, b):
    return _mm_exact_lhs(a_bf16, b)


def _block_diag(x, seg):
    return jnp.concatenate([x] * HEADS, axis=0) * seg


def _block_diag_pair(pair, seg):
    return _block_diag(pair[0], seg), _block_diag(pair[1], seg)


def _packed_masks():
    row = lax.broadcasted_iota(jnp.int32, (CHUNK, MIX_SEC), 0)
    pos = lax.broadcasted_iota(jnp.int32, (CHUNK, MIX_SEC), 1) % CHUNK
    return (pos <= row, pos >= row), (pos < row, pos > row), jnp.where(pos == row, 1.0, 0.0)


def _cumsum_matrices():
    incl_lo, incl_up, _, _ = _tri_masks()
    return incl_lo.astype(BF16), incl_up.astype(BF16)


def _bc_to_row(x_bc, eye):
    return jnp.sum(x_bc * eye, axis=0, keepdims=True)


def _head_rows(blocks):
    return jnp.concatenate(blocks, axis=-1)


def _mlstm_kernel(x_ref, gc_ref, bias_ref, seg_ref, segf_ref, sel_ref, c0_ref, n0_ref, m0_ref,
                  h_ref, c_ref, n_ref, m_ref,
                  dl_scr, emu_scr, num_scr, sum_scr, ka_scr, nloc_scr, mul_scr, cbd_scr, nrow_scr, mrow_scr,
                  *, n_chunks):
    seg = seg_ref[...]
    incl, _, eye = _packed_masks()
    tri64 = _cumsum_matrices()
    lane_head = lax.broadcasted_iota(jnp.int32, (CHUNK, MIX_SEC), 1) // HEAD_DIM

    zero = jnp.zeros((HEAD_DIM, HEAD_DIM), F32)
    for d in range(2):
        for hd in range(HEADS):
            blocks = [zero] * hd + [c0_ref[d * HEADS + hd]] + [zero] * (HEADS - 1 - hd)
            cbd_scr[d, hd * HEAD_DIM:(hd + 1) * HEAD_DIM, :] = _head_rows(blocks)
        n_row = _head_rows([n0_ref[d * HEADS + hd] for hd in range(HEADS)])
        m_row = _head_rows([jnp.broadcast_to(m0_ref[d * HEADS + hd], (1, HEAD_DIM)) for hd in range(HEADS)])
        nrow_scr[d] = jnp.broadcast_to(n_row, (8, MIX_SEC))
        mrow_scr[d] = jnp.broadcast_to(m_row, (8, MIX_SEC))

    n_groups = n_chunks // PREP_GROUP
    jobs = [(ci, d) for ci in range(PREP_GROUP) for d in range(2)]

    def job_rows(g, ci, d):
        c = PREP_GROUP * g + ci
        return pl.ds(pl.multiple_of((c if d == 0 else n_chunks - 1 - c) * CHUNK, CHUNK), CHUNK)

    def prep_stages(g, base):
        rows = [job_rows(g, ci, d) for ci, d in jobs]
        xs = [x_ref[r, :] for r in rows]
        qs = [x[:, 0:MIX_SEC] for x in xs]
        ks = [x[:, MIX_SEC:2 * MIX_SEC] * (HEAD_DIM ** -0.5) for x in xs]
        vbd = [_block_diag(x[:, 2 * MIX_SEC:3 * MIX_SEC].astype(BF16), seg) for x in xs]
        gcols = [gc_ref[r, :] for r in rows]
        i128 = [gc + bias_ref[0:1, :] for gc in gcols]
        f128 = [_log_sigmoid(gc + bias_ref[1:2, :]) for gc in gcols]
        yield
        cums = [_spread_lhs(tri64[d], f) for f, (ci, d) in zip(f128, jobs)]
        qk = [lax.dot_general(q.astype(BF16), _block_diag(k.astype(BF16), seg), (((1,), (1,)), ((), ())),
                              preferred_element_type=F32) for q, k in zip(qs, ks)]
        yield
        b_bc = [_spread(cum, sel_ref[2 + d]) for cum, (ci, d) in zip(cums, jobs)]
        i_bc = [_spread(i, sel_ref[d]) for i, (ci, d) in zip(i128, jobs)]
        yield
        s_locs = []
        for j, (ci, d) in enumerate(jobs):
            b = b_bc[j]
            r_bc = i_bc[j] - b
            dmat = jnp.where(incl[d], b + _bc_to_row(r_bc, eye), -jnp.inf)
            mu = [jnp.max(dmat[:, hd * HEAD_DIM:(hd + 1) * HEAD_DIM], axis=-1, keepdims=True) for hd in range(HEADS)]
            mu_bc = jnp.where(lane_head == 0, mu[0], jnp.where(lane_head == 1, mu[1],
                                                               jnp.where(lane_head == 2, mu[2], mu[3])))
            s_locs.append(qk[j] * jnp.exp(dmat - mu_bc))
            last = CHUNK - 1 if d == 0 else 0
            mu_last = mu_bc[last:last + 1, :]
            ka = ks[j] * jnp.exp(b[last:last + 1, :] + r_bc - mu_last)
            dl_scr[base + j] = b - mu_bc
            emu_scr[base + j] = jnp.exp(-mu_bc)
            ka_scr[base + j] = ka.astype(BF16)
            nloc_scr[base + j] = jnp.broadcast_to(jnp.sum(ka, axis=0, keepdims=True), (8, MIX_SEC))
            mul_scr[base + j] = jnp.broadcast_to(mu_last, (8, MIX_SEC))
            if j % 2 == 1:
                yield
        nums = [jnp.dot(s.astype(BF16), w, preferred_element_type=F32) for s, w in zip(s_locs, vbd)]
        yield
        sums = [_mm_exact_rhs(s, seg) for s in s_locs]
        yield
        for j in range(len(jobs)):
            num_scr[base + j] = nums[j]
            sum_scr[base + j] = sums[j]

    def step_stages(g, base):
        for ci in range(PREP_GROUP):
            rows = [job_rows(g, ci, d) for d in range(2)]
            slots = [base + 2 * ci + d for d in range(2)]
            qs = [x_ref[rows[d], 0:MIX_SEC] for d in range(2)]
            v16 = [x_ref[rows[d], 2 * MIX_SEC:3 * MIX_SEC].astype(BF16) for d in range(2)]
            c_prev = [cbd_scr[d] for d in range(2)]
            n_prev = [nrow_scr[d][0:1, :] for d in range(2)]
            m_prev = [mrow_scr[d][0:1, :] for d in range(2)]
            qc = [jnp.dot(qs[d].astype(BF16), c_prev[d].astype(BF16), preferred_element_type=F32) for d in range(2)]
            qn = [_mm_exact_rhs(qs[d] * n_prev[d], seg) for d in range(2)]
            upd = [_mm_tn(ka_scr[slots[d]], v16[d]) for d in range(2)]
            yield
            for d in range(2):
                z = dl_scr[slots[d]] + m_prev[d]
                zp = jnp.maximum(z, 0.0)
                w_inter = jnp.exp(z - zp)
                corr = jnp.exp(-zp)
                num = w_inter * qc[d] + corr * num_scr[slots[d]]
                den = w_inter * qn[d] + corr * sum_scr[slots[d]]
                h_ref[d, rows[d], :] = num / jnp.maximum(jnp.abs(den), emu_scr[slots[d]] * corr)
                last = CHUNK - 1 if d == 0 else 0
                a_prev = w_inter[last:last + 1, :]
                gamma = corr[last:last + 1, :]
                cbd_scr[d] = a_prev * c_prev[d] + gamma * (upd[d] * segf_ref[...])
                nrow_scr[d] = jnp.broadcast_to(a_prev * n_prev[d] + gamma * nloc_scr[slots[d]][0:1, :], (8, MIX_SEC))
                mrow_scr[d] = jnp.broadcast_to(mul_scr[slots[d]][0:1, :] + zp[last:last + 1, :], (8, MIX_SEC))
            yield

    _pipelined_groups(n_groups, len(jobs), prep_stages, step_stages)

    for d in range(2):
        for hd in range(HEADS):
            sl = slice(hd * HEAD_DIM, (hd + 1) * HEAD_DIM)
            c_ref[d * HEADS + hd] = cbd_scr[d, sl, sl]
            n_ref[d * HEADS + hd] = nrow_scr[d][0:1, sl]
            m_ref[d * HEADS + hd] = mrow_scr[d][0:1, hd * HEAD_DIM:hd * HEAD_DIM + 1]


def _mlstm(ml_qkv, g_col, bias_row, seg, segf, sel, c0, n0, m0):
    B, T, _ = ml_qkv.shape
    nc = T // CHUNK
    per_b = lambda *s: pl.BlockSpec((None,) + s, lambda b: (b,) + (0,) * len(s))
    full = lambda a: pl.BlockSpec(a.shape, lambda b: (0,) * a.ndim)
    st_shapes = [(2 * HEADS, HEAD_DIM, HEAD_DIM), (2 * HEADS, 1, HEAD_DIM), (2 * HEADS, 1, 1)]
    per_chunk = lambda rows, dt: pltpu.VMEM((4 * PREP_GROUP, rows, MIX_SEC), dt)
    return pl.pallas_call(
        functools.partial(_mlstm_kernel, n_chunks=nc),
        grid=(B,),
        in_specs=[per_b(T, 3 * MIX_SEC), per_b(T, LANES), full(bias_row), full(seg), full(segf), full(sel)]
                 + [per_b(*s) for s in st_shapes],
        out_specs=[pl.BlockSpec((2, None, T, MIX_SEC), lambda b: (0, b, 0, 0))] + [per_b(*s) for s in st_shapes],
        out_shape=[jax.ShapeDtypeStruct((2, B, T, MIX_SEC), F32)]
                  + [jax.ShapeDtypeStruct((B,) + s, F32) for s in st_shapes],
        scratch_shapes=[per_chunk(CHUNK, F32), per_chunk(CHUNK, F32), per_chunk(CHUNK, F32), per_chunk(CHUNK, F32),
                        per_chunk(CHUNK, BF16), per_chunk(8, F32), per_chunk(8, F32),
                        pltpu.VMEM((2, MIX_SEC, MIX_SEC), F32), pltpu.VMEM((2, 8, MIX_SEC), F32),
                        pltpu.VMEM((2, 8, MIX_SEC), F32)],
        compiler_params=_cparams(("parallel",)),
        name="mlstm",
    )(ml_qkv, g_col, bias_row, seg, segf, sel, c0, n0, m0)


def _packed_mm3(l_pair, w_pair):
    (lh, ll), (wh, wl) = l_pair, w_pair
    rows = lh.shape[0]
    dot = functools.partial(jnp.dot, preferred_element_type=F32)
    top = dot(jnp.concatenate([lh, ll], axis=0), wh)
    return top[0:rows] + (top[rows:2 * rows] + dot(lh, wl))


def _pipelined_groups(n_groups, n_jobs, prep_stages, step_stages):
    _interleave(prep_stages(0, 0))

    def pair(i, carry):
        g = 2 * i + 1
        _interleave(prep_stages(g, n_jobs), step_stages(g - 1, 0))
        _interleave(prep_stages(g + 1, 0), step_stages(g, n_jobs))
        return carry

    lax.fori_loop(0, (n_groups - 1) // 2, pair, 0)
    last = n_groups - 1
    if last % 2 == 1:
        _interleave(prep_stages(last, n_jobs), step_stages(last - 1, 0))
    _interleave(step_stages(last, (last % 2) * n_jobs))


def _interleave(*stage_generators):
    live = list(stage_generators)
    while live:
        for gen in list(live):
            try:
                next(gen)
            except StopIteration:
                live.remove(gen)


def _packed_tri_inverse_stages(a_list, eye, seg, out):
    a_pairs = [_split(a) for a in a_list]
    ps = [eye - a for a in a_list]
    xs = [_packed_mm3(ap, _block_diag_pair(ap, seg)) for ap in a_pairs]
    yield
    for i in range(4):
        x_pairs = [_split(x) for x in xs]
        wxs = [_block_diag_pair(xp, seg) for xp in x_pairs]
        p_pairs = [_split(p) for p in ps]
        both = [_packed_mm3((jnp.concatenate([pp[0], xp[0]], axis=0), jnp.concatenate([pp[1], xp[1]], axis=0)), wx)
                for pp, xp, wx in zip(p_pairs, x_pairs, wxs)]
        ps = [p + pq[0:CHUNK] for p, pq in zip(ps, both)]
        xs = [pq[CHUNK:2 * CHUNK] for pq in both]
        yield
    wxs = [_block_diag_pair(_split(x), seg) for x in xs]
    out.extend(p + _packed_mm3(_split(p), wx) for p, wx in zip(ps, wxs))
    yield


def _deltanet_kernel(x_ref, gc_ref, conv_ref, par_row_ref, seg_ref, segf_ref, sel_ref,
                     s0_ref, o_ref, s_ref, u_scr, w_scr, qg_scr, attn_scr, kgt_scr, egl_scr, sbd_scr, *, n_chunks):
    T = n_chunks * CHUNK
    seg = seg_ref[...]
    incl, strict, eye = _packed_masks()
    tri64 = _cumsum_matrices()

    zero = jnp.zeros((HEAD_DIM, HEAD_DIM), F32)
    for d in range(2):
        for hd in range(HEADS):
            blocks = [zero] * hd + [s0_ref[d * HEADS + hd]] + [zero] * (HEADS - 1 - hd)
            sbd_scr[d, hd * HEAD_DIM:(hd + 1) * HEAD_DIM, :] = jnp.concatenate(blocks, axis=-1)

    def conv_silu(c):
        r0 = pl.multiple_of(c * CHUNK, CHUNK)
        xc = x_ref[pl.ds(r0, CHUNK), :]
        before = x_ref[pl.ds(pl.multiple_of(jnp.maximum(r0 - 8, 0), 8), 8), :][7:8, :]
        after = x_ref[pl.ds(pl.multiple_of(jnp.minimum(r0 + CHUNK, T - 8), 8), 8), :][0:1, :]
        before = jnp.where(c > 0, before, 0.0)
        after = jnp.where(c < n_chunks - 1, after, 0.0)
        rows = lax.broadcasted_iota(jnp.int32, xc.shape, 0)
        x_prev = jnp.where(rows == 0, before, pltpu.roll(xc, 1, 0))
        x_next = jnp.where(rows == CHUNK - 1, after, pltpu.roll(xc, CHUNK - 1, 0))
        return _silu(x_prev * conv_ref[0:1, :] + xc * conv_ref[1:2, :] + x_next * conv_ref[2:3, :])

    n_groups = n_chunks // PREP_GROUP
    jobs = [(ci, d) for ci in range(PREP_GROUP) for d in range(2)]

    def job_chunk(g, ci, d):
        c = PREP_GROUP * g + ci
        return c if d == 0 else n_chunks - 1 - c

    def prep_stages(g, base):
        chunks = [job_chunk(g, ci, d) for ci, d in jobs]
        ys = [conv_silu(c) for c in chunks]
        qs = [y[:, 0:MIX_SEC] for y in ys]
        ks = [y[:, MIX_SEC:2 * MIX_SEC] for y in ys]
        vs = [y[:, 2 * MIX_SEC:3 * MIX_SEC] for y in ys]
        gcols = [gc_ref[pl.ds(pl.multiple_of(c * CHUNK, CHUNK), CHUNK), :] for c in chunks]
        beta128 = [jax.nn.sigmoid(gc) for gc in gcols]
        la128 = [par_row_ref[0:1, :] * _softplus(gc + par_row_ref[1:2, :]) for gc in gcols]
        yield
        q_ss = [_head_mean_sq(q, seg_ref) for q in qs]
        k_ss = [_head_mean_sq(k, seg_ref) for k in ks]
        cums = [_spread_lhs(tri64[d], la) for la, (ci, d) in zip(la128, jobs)]
        yield
        qs = [q * lax.rsqrt(ss * HEAD_DIM + EPS) * (HEAD_DIM ** -0.5) for q, ss in zip(qs, q_ss)]
        ks = [k * lax.rsqrt(ss * HEAD_DIM + EPS) for k, ss in zip(ks, k_ss)]
        k16 = [k.astype(BF16) for k in ks]
        kq = [lax.dot_general(jnp.concatenate([kb, q.astype(BF16)], axis=0), _block_diag(kb, seg),
                              (((1,), (1,)), ((), ())), preferred_element_type=F32) for kb, q in zip(k16, qs)]
        g_bc = [_spread(cum, sel_ref[d]) for cum, (ci, d) in zip(cums, jobs)]
        beta_bc = [_spread(b128, sel_ref[2 + d]) for b128, (ci, d) in zip(beta128, jobs)]
        yield
        a_list, rhs_u, rhs_w = [], [], []
        for j, (ci, d) in enumerate(jobs):
            g_t, beta = g_bc[j], beta_bc[j]
            kk, qk = kq[j][0:CHUNK], kq[j][CHUNK:2 * CHUNK]
            decay = jnp.exp(jnp.where(incl[d], g_t - _bc_to_row(g_t, eye), -jnp.inf))
            a_list.append(jnp.where(strict[d], beta * kk * decay, 0.0))
            eg = jnp.exp(g_t)
            last = CHUNK - 1 if d == 0 else 0
            g_last = g_t[last:last + 1, :]
            qg_scr[base + j] = (qs[j] * eg).astype(BF16)
            attn_scr[base + j] = (qk * decay).astype(BF16)
            kgt_scr[base + j] = (ks[j] * jnp.exp(g_last - g_t)).T.astype(BF16)
            egl_scr[base + j] = jnp.broadcast_to(jnp.exp(g_last), (8, MIX_SEC))
            rhs_u.append(_split(vs[j] * beta))
            rhs_w.append(_split(ks[j] * beta * eg))
        yield
        t_inv = []
        yield from _packed_tri_inverse_stages(a_list, eye, seg, t_inv)
        t_pairs = [_split(t) for t in t_inv]
        us = [_packed_mm3(tp, _block_diag_pair(rhs, seg)) for tp, rhs in zip(t_pairs, rhs_u)]
        yield
        ws = [_packed_mm3(tp, _block_diag_pair(rhs, seg)) for tp, rhs in zip(t_pairs, rhs_w)]
        yield
        for j in range(len(jobs)):
            u_scr[base + j] = us[j]
            w_scr[base + j] = ws[j].astype(BF16)

    def step_stages(g, base):
        for ci in range(PREP_GROUP):
            slots = [base + 2 * ci + d for d in range(2)]
            s_prev = [sbd_scr[d] for d in range(2)]
            rs = [jnp.dot(jnp.concatenate([w_scr[slots[d]], qg_scr[slots[d]]], axis=0), s_prev[d].astype(BF16),
                          preferred_element_type=F32) for d in range(2)]
            yield
            v16 = [(u_scr[slots[d]] - rs[d][0:CHUNK]).astype(BF16) for d in range(2)]
            outs = [rs[d][CHUNK:2 * CHUNK] + jnp.dot(attn_scr[slots[d]], _block_diag(v16[d], seg),
                                                     preferred_element_type=F32) for d in range(2)]
            upds = [jnp.dot(kgt_scr[slots[d]], v16[d], preferred_element_type=F32) for d in range(2)]
            yield
            for d in range(2):
                r0 = pl.multiple_of(job_chunk(g, ci, d) * CHUNK, CHUNK)
                o_ref[d, pl.ds(r0, CHUNK), :] = outs[d]
                sbd_scr[d] = egl_scr[slots[d]][0:1, :] * s_prev[d] + upds[d] * segf_ref[...]
            yield

    _pipelined_groups(n_groups, len(jobs), prep_stages, step_stages)

    for d in range(2):
        for hd in range(HEADS):
            sl = slice(hd * HEAD_DIM, (hd + 1) * HEAD_DIM)
            s_ref[d * HEADS + hd] = sbd_scr[d, sl, sl]


def _deltanet(dn_qkv, g_col, conv_w, par_row, seg, segf, sel, s0):
    B, T, _ = dn_qkv.shape
    nc = T // CHUNK
    per_b = lambda *s: pl.BlockSpec((None,) + s, lambda b: (b,) + (0,) * len(s))
    full = lambda a: pl.BlockSpec(a.shape, lambda b: (0,) * a.ndim)
    st = (2 * HEADS, HEAD_DIM, HEAD_DIM)
    per_chunk = lambda rows, cols, dt: pltpu.VMEM((4 * PREP_GROUP, rows, cols), dt)
    return pl.pallas_call(
        functools.partial(_deltanet_kernel, n_chunks=nc),
        grid=(B,),
        in_specs=[per_b(T, 3 * MIX_SEC), per_b(T, LANES), full(conv_w), full(par_row),
                  full(seg), full(segf), full(sel), per_b(*st)],
        out_specs=[pl.BlockSpec((2, None, T, MIX_SEC), lambda b: (0, b, 0, 0)), per_b(*st)],
        out_shape=[jax.ShapeDtypeStruct((2, B, T, MIX_SEC), F32), jax.ShapeDtypeStruct((B,) + st, F32)],
        scratch_shapes=[per_chunk(CHUNK, MIX_SEC, F32), per_chunk(CHUNK, MIX_SEC, BF16),
                        per_chunk(CHUNK, MIX_SEC, BF16), per_chunk(CHUNK, MIX_SEC, BF16),
                        per_chunk(MIX_SEC, CHUNK, BF16), per_chunk(8, MIX_SEC, F32),
                        pltpu.VMEM((2, MIX_SEC, MIX_SEC), F32)],
        compiler_params=_cparams(("parallel",)),
        name="deltanet",
    )(dn_qkv, g_col, conv_w, par_row, seg, segf, sel, s0)


def _fourier_kernel(x_ref, cg_ref, sg_ref, ct_ref, st_ref, o_ref, p_scr, q_scr, *, scale):
    b = pl.program_id(1)

    @pl.when(pl.program_id(0) == 0)
    def _():
        x = x_ref[...].astype(BF16)
        p_scr[b] = jnp.dot(x, cg_ref[...], preferred_element_type=F32).astype(BF16)
        q_scr[b] = jnp.dot(x, sg_ref[...], preferred_element_type=F32).astype(BF16)

    y = (jnp.dot(ct_ref[...], p_scr[b], preferred_element_type=F32)
         - jnp.dot(st_ref[...], q_scr[b], preferred_element_type=F32))
    o_ref[...] = y * scale


def _dft_tables(T):
    def tab(n):
        idx = np.arange(n, dtype=np.int64)
        ang = 2.0 * np.pi * ((idx[:, None] * idx[None, :]) % n).astype(np.float64) / n
        return np.cos(ang), np.sin(ang)
    ct, st = tab(T)
    cg1, sg1 = tab(HEAD_DIM)
    eye = np.eye(HEADS)
    cg, sg = np.kron(eye, cg1), np.kron(eye, sg1)
    return tuple(jnp.asarray(a, dtype=F32).astype(BF16) for a in (cg, sg, ct, st))


def _fourier(x):
    B, T, _ = x.shape
    cg, sg, ct, st = _dft_tables(T)
    tr = min(T, 512)
    return pl.pallas_call(
        functools.partial(_fourier_kernel, scale=1.0 / math.sqrt(T * HEAD_DIM)),
        grid=(T // tr, B),
        in_specs=[pl.BlockSpec((None, T, MIX_SEC), lambda i, b: (jnp.where(i == 0, b, B - 1), 0, 0)),
                  pl.BlockSpec((MIX_SEC, MIX_SEC), lambda i, b: (0, 0)),
                  pl.BlockSpec((MIX_SEC, MIX_SEC), lambda i, b: (0, 0)),
                  pl.BlockSpec((tr, T), lambda i, b: (i, 0)),
                  pl.BlockSpec((tr, T), lambda i, b: (i, 0))],
        out_specs=pl.BlockSpec((None, tr, MIX_SEC), lambda i, b: (b, i, 0)),
        out_shape=jax.ShapeDtypeStruct((B, T, MIX_SEC), F32),
        scratch_shapes=[pltpu.VMEM((B, T, MIX_SEC), BF16), pltpu.VMEM((B, T, MIX_SEC), BF16)],
        compiler_params=_cparams(("arbitrary", "arbitrary")),
        name="fourier",
    )(x, cg, sg, ct, st)


def _diffattn_kernel(*refs, lam_init, has_ctx):
    if has_ctx:
        q_ref, k_ref, v_ref, kc_ref, vc_ref, lam_ref, o_ref = refs
    else:
        q_ref, k_ref, v_ref, lam_ref, o_ref = refs
    lp = lam_ref[...]
    lam = (jnp.exp(jnp.sum(lp[0:1, :] * lp[1:2, :], axis=-1, keepdims=True))
           - jnp.exp(jnp.sum(lp[2:3, :] * lp[3:4, :], axis=-1, keepdims=True)) + lam_init)
    q = q_ref[...]
    pieces = [(k_ref[...].astype(BF16), v_ref[...].astype(BF16))]
    if has_ctx:
        pieces.append((kc_ref[...].astype(BF16), vc_ref[...].astype(BF16)))
    lane = lax.broadcasted_iota(jnp.int32, q.shape, 1)

    def scores(hd):
        maps = []
        for mp in range(2):
            lo = (hd * 2 + mp) * DA_QK
            qm = jnp.where((lane >= lo) & (lane < lo + DA_QK), q, jnp.zeros_like(q))
            maps.append([lax.dot_general(qm, kk, (((1,), (1,)), ((), ())), preferred_element_type=F32)
                         for kk, _ in pieces])
        return maps

    out = jnp.zeros(q.shape, F32)
    s_next = scores(0)
    for hd in range(HEADS):
        s_cur = s_next
        if hd + 1 < HEADS:
            s_next = scores(hd + 1)
        probs = []
        for mp in range(2):
            ss = s_cur[mp]
            mx = functools.reduce(jnp.maximum, [jnp.max(s, axis=-1, keepdims=True) for s in ss])
            es = [jnp.exp2(s - mx) for s in ss]
            den = functools.reduce(jnp.add, [jnp.sum(e, axis=-1, keepdims=True) for e in es])
            probs.append((es, 1.0 / den))
        (e0, r0), (e1, r1) = probs
        r1 = lam * r1
        acc = None
        for j, (_, vv) in enumerate(pieces):
            w = (e0[j] * r0 - e1[j] * r1).astype(BF16)
            part = jnp.dot(w, vv, preferred_element_type=F32)
            acc = part if acc is None else acc + part
        out = jnp.where((lane >= hd * HEAD_DIM) & (lane < (hd + 1) * HEAD_DIM), acc, out)
    o_ref[...] = out


def _diffattn(q, k, v, ctx, lam_par, lam_init):
    B, T, _ = q.shape
    tq = min(T, 512)
    has_ctx = ctx is not None
    in_specs = [pl.BlockSpec((None, tq, MIX_SEC), lambda b, i: (b, i, 0)),
                pl.BlockSpec((None, T, MIX_SEC), lambda b, i: (b, 0, 0)),
                pl.BlockSpec((None, T, MIX_SEC), lambda b, i: (b, 0, 0))]
    args = [q, k, v]
    if has_ctx:
        ck, cv, li = ctx
        P = ck.shape[2]
        in_specs += [pl.BlockSpec((None, None, P, MIX_SEC), lambda b, i: (b, li, 0, 0))] * 2
        args += [ck, cv]
    in_specs.append(pl.BlockSpec(lam_par.shape, lambda b, i: (0, 0)))
    args.append(lam_par)
    return pl.pallas_call(
        functools.partial(_diffattn_kernel, lam_init=lam_init, has_ctx=has_ctx),
        grid=(B, T // tq),
        in_specs=in_specs,
        out_specs=pl.BlockSpec((None, tq, MIX_SEC), lambda b, i: (b, i, 0)),
        out_shape=jax.ShapeDtypeStruct((B, T, MIX_SEC), F32),
        compiler_params=_cparams(("parallel", "arbitrary")),
        name="diff_attention",
    )(*args)


def _mixout_kernel(x_ref, mod_ref, gpost_ref, hml_ref, og_ref, odn_ref, dng_ref, ft_ref, oda_ref,
                   gains_ref, seg_ref, w_ref, o_ref, *, da_scale):
    def head_norm(z, gain_row):
        return z * lax.rsqrt(_head_mean_sq(z, seg_ref) + EPS) * gain_row

    h_ml = (hml_ref[0] + hml_ref[1]) * jax.nn.sigmoid(og_ref[...])
    y_ml = head_norm(h_ml, gains_ref[0:1, :])
    y_dn = head_norm(odn_ref[0] + odn_ref[1], gains_ref[1:2, :]) * _silu(dng_ref[...])
    y_da = head_norm(oda_ref[...], gains_ref[2:3, :]) * da_scale
    y = (_mm(y_ml, w_ref[0:MIX_SEC, :]) + _mm(y_dn, w_ref[MIX_SEC:2 * MIX_SEC, :])
         + _mm(ft_ref[...], w_ref[2 * MIX_SEC:3 * MIX_SEC, :]) + _mm(y_da, w_ref[3 * MIX_SEC:4 * MIX_SEC, :]))
    o_ref[...] = x_ref[...] + mod_ref[5:6, :] * (_rms(y) * gpost_ref[...])


def _mixout(x, mod, g_post, h_ml, ml, o_dn, dn_g, y_ft, o_da, gains, seg, w_out, da_scale):
    B, T, _ = x.shape
    tm = min(T, 512)
    tpb = T // tm
    n_tok = B * T
    x2 = lambda a: a.reshape(n_tok, a.shape[-1])
    h_ml = h_ml.reshape(2, n_tok, MIX_SEC)
    o_dn = o_dn.reshape(2, n_tok, MIX_SEC)
    tok = lambda w: pl.BlockSpec((tm, w), lambda i: (i, 0))
    two = pl.BlockSpec((2, tm, MIX_SEC), lambda i: (0, i, 0))
    full = lambda a: pl.BlockSpec(a.shape, lambda i: (0,) * a.ndim)
    out = pl.pallas_call(
        functools.partial(_mixout_kernel, da_scale=da_scale),
        grid=(n_tok // tm,),
        in_specs=[tok(D_MODEL), pl.BlockSpec((None, N_MOD, D_MODEL), lambda i: (i // tpb, 0, 0)), full(g_post),
                  two, tok(MIX_SEC), two, tok(MIX_SEC), tok(MIX_SEC), tok(MIX_SEC),
                  full(gains), full(seg), full(w_out)],
        out_specs=tok(D_MODEL),
        out_shape=jax.ShapeDtypeStruct((n_tok, D_MODEL), F32),
        compiler_params=_cparams(("parallel",)),
        name="mix_out",
    )(x2(x), mod, g_post, h_ml, x2(ml), o_dn, x2(dn_g), x2(y_ft), x2(o_da), gains, seg, w_out)
    return out.reshape(x.shape)


def _rope_tables(T):
    t = np.arange(T)
    row = (t // GRID_W).astype(np.float64)
    col = (t % GRID_W).astype(np.float64)
    inv = ROPE_BASE ** (-np.arange(ROPE_AXIS_PAIRS, dtype=np.float64) / ROPE_AXIS_PAIRS)
    ang = np.concatenate([row[:, None] * inv, col[:, None] * inv], axis=-1)
    cos = np.repeat(np.cos(ang), 2, axis=-1)
    sin = np.repeat(np.sin(ang), 2, axis=-1) * np.tile(np.array([-1.0, 1.0]), DA_QK // 2)
    reps = MIX_SEC // DA_QK
    return jnp.asarray(np.tile(cos, (1, reps)), dtype=F32), jnp.asarray(np.tile(sin, (1, reps)), dtype=F32)


def _layer_params(li, w_in, w_out, ffn_w_in, ffn_w_out, norm_pre, norm_post, ml_i_bias, ml_f_bias, ml_norm,
                  dn_conv, dn_A_log, dn_dt_bias, dn_norm, da_lambda, da_norm):
    w = w_in[li]
    main = jnp.concatenate([w[:, _OFF_ML:_OFF_MLG], w[:, _OFF_DN:_OFF_DNG], w[:, _OFF_FT:IN_WIDTH]], axis=1)
    gate = jnp.concatenate([w[:, _OFF_MLG:_OFF_DN], w[:, _OFF_DNG:_OFF_FT]], axis=1)
    on_lanes = lambda vec, lo: jnp.pad(vec.reshape(-1), (lo, LANES - lo - 2 * HEADS))
    ml_bias = jnp.stack([on_lanes(ml_i_bias[li], 0), on_lanes(ml_f_bias[li], 8)])
    dn_par_row = jnp.stack([on_lanes(-jnp.exp(dn_A_log[li]), 24), on_lanes(dn_dt_bias[li], 24)])
    return dict(
        w_main=main.astype(BF16),
        w_gate=jnp.pad(gate, ((0, 0), (0, LANES - N_GATE))).astype(BF16),
        w_out=w_out[li].astype(BF16),
        ffn_w_in=[ffn_w_in[li, j].astype(BF16) for j in range(2)],
        ffn_w_out=[ffn_w_out[li, j].astype(BF16) for j in range(2)],
        norm_pre=[norm_pre[li, j][None, :] for j in range(3)],
        norm_post=[norm_post[li, j][None, :] for j in range(3)],
        ml_bias_row=ml_bias,
        dn_par_row=dn_par_row,
        dn_conv=dn_conv[li],
        gains=jnp.stack([ml_norm[li].reshape(-1), jnp.tile(dn_norm[li], HEADS), jnp.tile(da_norm[li], HEADS)]),
        da_lambda=da_lambda[li],
    )


def _packed_consts():
    seg = np.kron(np.eye(HEADS), np.ones((HEAD_DIM, HEAD_DIM)))

    def spread(first_lanes):
        sel = np.zeros((4, LANES, MIX_SEC))
        for j, lo in enumerate(first_lanes):
            for hd in range(HEADS):
                sel[j, lo + hd, hd * HEAD_DIM:(hd + 1) * HEAD_DIM] = 1.0
        return jnp.asarray(sel, dtype=BF16)

    return dict(seg=jnp.asarray(seg, dtype=BF16), segf=jnp.asarray(seg, dtype=F32),
                sel_ml=spread((0, HEADS, 8, 8 + HEADS)), sel_dn=spread((24, 24 + HEADS, 16, 16 + HEADS)))


def _mixing(x_tok, mod_tok, shape_bt, lp, li, rope_tabs, states, ctx, consts):
    B, T = shape_bt
    seg, segf = consts['seg'], consts['segf']
    lam_init = 0.8 - 0.6 * math.exp(-0.3 * li)
    ml, ml_o, dn, dn_g, ft, da_q, da_k, da_v, g_col = _inproj(
        x_tok, mod_tok, lp['norm_pre'][1], lp['w_main'], lp['w_gate'], rope_tabs)
    seq = lambda a: a.reshape(B, T, a.shape[-1])
    c0, n0, m0, s0 = states
    h_ml, c_f, n_f, m_f = _mlstm(seq(ml), seq(g_col), lp['ml_bias_row'], seg, segf, consts['sel_ml'], c0, n0, m0)
    o_dn, s_f = _deltanet(seq(dn), seq(g_col), lp['dn_conv'], lp['dn_par_row'], seg, segf, consts['sel_dn'], s0)
    y_ft = _fourier(seq(ft))
    o_da = _diffattn(seq(da_q), seq(da_k), seq(da_v), ctx, lp['da_lambda'], lam_init)
    x_new = _mixout(x_tok, mod_tok, lp['norm_post'][1], h_ml, ml_o, o_dn, dn_g, y_ft, o_da, lp['gains'], seg,
                    lp['w_out'], 1.0 - lam_init)
    return x_new, (seq(da_k), seq(da_v), c_f, n_f, m_f, s_f)


def _trunk_layer(x_tok, mod_tok, shape_bt, lp, li, rope_tabs, states, ctx, consts):
    x_tok = _ffn_block(x_tok, mod_tok, lp['norm_pre'][0], lp['norm_post'][0], lp['ffn_w_in'][0], lp['ffn_w_out'][0], 0)
    x_tok, aux = _mixing(x_tok, mod_tok, shape_bt, lp, li, rope_tabs, states, ctx, consts)
    x_tok = _ffn_block(x_tok, mod_tok, lp['norm_pre'][2], lp['norm_post'][2], lp['ffn_w_in'][1], lp['ffn_w_out'][1], 6)
    return x_tok, aux


def kernel(x_prompt, x_sample, cache_k, cache_v, state_mlstm_C, state_mlstm_n, state_mlstm_m, state_delta_S, c, c_ctx, w_mod, b_mod, norm_pre, norm_post, ffn_w_in, ffn_w_out, w_in, w_out, ml_i_bias, ml_f_bias, ml_norm, dn_conv, dn_A_log, dn_dt_bias, dn_norm, da_lambda, da_norm):
    Bp, Tp, _ = x_prompt.shape
    Bs, Ts, _ = x_sample.shape
    P = cache_k.shape[2]
    n_st = 2 * HEADS

    cond = jnp.concatenate([c, c_ctx[None, :]], axis=0)
    cond = jnp.pad(cond, ((0, (-cond.shape[0]) % 8), (0, 0)))
    mod = _modulation(cond, w_mod, b_mod)
    rope_tabs = _rope_tables(Ts)
    consts = _packed_consts()
    ck = cache_k.reshape(Bs, DEPTH, P, MIX_SEC)
    cv = cache_v.reshape(Bs, DEPTH, P, MIX_SEC)
    zeros_p = (jnp.zeros((Bp, n_st, HEAD_DIM, HEAD_DIM), F32), jnp.zeros((Bp, n_st, 1, HEAD_DIM), F32),
               jnp.zeros((Bp, n_st, 1, 1), F32), jnp.zeros((Bp, n_st, HEAD_DIM, HEAD_DIM), F32))

    xp = x_prompt.reshape(1, Bp * Tp, D_MODEL)
    xs = x_sample
    ks, vs, Cs, ns, ms, Ss = [], [], [], [], [], []
    for li in range(DEPTH):
        lp = _layer_params(li, w_in, w_out, ffn_w_in, ffn_w_out, norm_pre, norm_post, ml_i_bias, ml_f_bias, ml_norm,
                           dn_conv, dn_A_log, dn_dt_bias, dn_norm, da_lambda, da_norm)
        mod_s = jnp.transpose(mod[li, :, :Bs], (1, 0, 2))
        mod_p = mod[li, :, Bs:Bs + 1].reshape(1, N_MOD, D_MODEL)
        xp, (k_c, v_c, C_c, n_c, m_c, S_c) = _trunk_layer(xp, mod_p, (Bp, Tp), lp, li, None, zeros_p, None, consts)
        ks.append(k_c.reshape(Bp, Tp, HEADS, 2, DA_QK))
        vs.append(v_c.reshape(Bp, Tp, HEADS, HEAD_DIM))
        Cs.append(C_c.reshape(Bp, 2, HEADS, HEAD_DIM, HEAD_DIM))
        ns.append(n_c.reshape(Bp, 2, HEADS, HEAD_DIM))
        ms.append(m_c.reshape(Bp, 2, HEADS))
        Ss.append(S_c.reshape(Bp, 2, HEADS, HEAD_DIM, HEAD_DIM))
        states = (state_mlstm_C[:, li].reshape(Bs, n_st, HEAD_DIM, HEAD_DIM),
                  state_mlstm_n[:, li].reshape(Bs, n_st, 1, HEAD_DIM),
                  state_mlstm_m[:, li].reshape(Bs, n_st, 1, 1),
                  state_delta_S[:, li].reshape(Bs, n_st, HEAD_DIM, HEAD_DIM))
        xs, _ = _trunk_layer(xs, mod_s, (Bs, Ts), lp, li, rope_tabs, states, (ck, cv, li), consts)
    return (xp.reshape(Bp, Tp, D_MODEL), xs, jnp.stack(ks, axis=1), jnp.stack(vs, axis=1), jnp.stack(Cs, axis=1),
            jnp.stack(ns, axis=1), jnp.stack(ms, axis=1), jnp.stack(Ss, axis=1))
```

```python
import functools
import math

import numpy as np
import jax
import jax.numpy as jnp
from jax import lax
from jax.experimental import pallas as pl
from jax.experimental.pallas import tpu as pltpu

D_MODEL = 1024
DEPTH = 2
GRID_W = 64
HEADS = 4
HEAD_DIM = 64
MIX_SEC = HEADS * HEAD_DIM
DA_QK = 32
D_FF = ((8 * D_MODEL // 3 + 127) // 128) * 128
N_MOD = 9
CHUNK = 64
N_GATE = 32
GATE_ML_I, GATE_ML_F, GATE_DN_B, GATE_DN_A = 0, 8, 16, 24
ROPE_AXIS_PAIRS = DA_QK // 4
ROPE_BASE = 10000.0
EPS = 1e-6
HALF = 0.5
LOG2_E = 1.4426950408889634
F32 = jnp.float32
BF16 = jnp.bfloat16
LANES = 128
VMEM_LIMIT = 56 * 1024 * 1024

_OFF_ML = 0
_OFF_MLG = 4 * MIX_SEC
_OFF_DN = _OFF_MLG + 16
_OFF_DNG = _OFF_DN + 4 * MIX_SEC
_OFF_FT = _OFF_DNG + 16
_OFF_DA = _OFF_FT + MIX_SEC
IN_WIDTH = _OFF_DA + 3 * MIX_SEC


def _cparams(sem):
    return pltpu.CompilerParams(dimension_semantics=sem, vmem_limit_bytes=VMEM_LIMIT)


def _mm(a, b):
    return jnp.dot(a.astype(BF16), b.astype(BF16), preferred_element_type=F32)


def _mm_tn(a, b):
    return lax.dot_general(a.astype(BF16), b.astype(BF16), (((0,), (0,)), ((), ())), preferred_element_type=F32)


def _split(a):
    hi = a.astype(BF16)
    lo = (a - hi.astype(F32)).astype(BF16)
    return hi, lo


def _mm_exact_rhs(a, b_bf16):
    ah, al = _split(a)
    dot = functools.partial(jnp.dot, preferred_element_type=F32)
    return dot(ah, b_bf16) + dot(al, b_bf16)


def _mm_exact_lhs(a_bf16, b):
    bh, bl = _split(b)
    dot = functools.partial(jnp.dot, preferred_element_type=F32)
    return dot(a_bf16, bh) + dot(a_bf16, bl)


def _rms(x):
    return x * lax.rsqrt(jnp.mean(x * x, axis=-1, keepdims=True) + EPS)


def _silu(x):
    return x * jax.nn.sigmoid(x)


def _log_sigmoid(x):
    return jnp.minimum(x, 0.0) - jnp.log1p(jnp.exp(-jnp.abs(x)))


def _softplus(x):
    return jnp.maximum(x, 0.0) + jnp.log1p(jnp.exp(-jnp.abs(x)))


def _head_mean_sq(x, seg_ref):
    return _mm_exact_rhs(x * x, seg_ref[...]) * (1.0 / HEAD_DIM)


def _tri_masks():
    row = lax.broadcasted_iota(jnp.int32, (CHUNK, CHUNK), 0)
    col = lax.broadcasted_iota(jnp.int32, (CHUNK, CHUNK), 1)
    return row >= col, row <= col, row > col, row < col


def _mod_kernel(c_ref, w_ref, b_ref, o_ref):
    c = c_ref[...]
    o_ref[...] = _mm(_silu(c), w_ref[...]) + b_ref[...]


def _modulation(cond, w_mod, b_mod):
    R = cond.shape[0]
    return pl.pallas_call(
        _mod_kernel,
        grid=(DEPTH, N_MOD),
        in_specs=[pl.BlockSpec((R, D_MODEL), lambda l, j: (0, 0)),
                  pl.BlockSpec((None, D_MODEL, D_MODEL), lambda l, j: (l, 0, j)),
                  pl.BlockSpec((None, None, 1, D_MODEL), lambda l, j: (l, j, 0, 0))],
        out_specs=pl.BlockSpec((None, None, R, D_MODEL), lambda l, j: (l, j, 0, 0)),
        out_shape=jax.ShapeDtypeStruct((DEPTH, N_MOD, R, D_MODEL), F32),
        compiler_params=_cparams(("arbitrary", "arbitrary")),
        name="modulation",
    )(cond, w_mod, b_mod.reshape(DEPTH, N_MOD, 1, D_MODEL))


FFN_SLICE = 256


def _ffn_kernel(x_ref, mod_ref, gpre_ref, gpost_ref, win_ref, wout_ref, o_ref, *, m0):
    x = x_ref[...]
    h = (_rms(x) * gpre_ref[...] * (1.0 + mod_ref[m0 + 1:m0 + 2, :]) + mod_ref[m0:m0 + 1, :]).astype(BF16)
    acc = None
    for lo in range(0, D_FF, FFN_SLICE):
        a = jnp.dot(h, win_ref[:, lo:lo + FFN_SLICE], preferred_element_type=F32)
        b = jnp.dot(h, win_ref[:, D_FF + lo:D_FF + lo + FFN_SLICE], preferred_element_type=F32)
        part = _mm(_silu(a) * b, wout_ref[lo:lo + FFN_SLICE, :])
        acc = part if acc is None else acc + part
    o_ref[...] = x + HALF * mod_ref[m0 + 2:m0 + 3, :] * (_rms(acc) * gpost_ref[...])


def _ffn_block(x, mod, g_pre, g_post, w_in, w_out, m0):
    B, T, _ = x.shape
    tm = min(T, 1024)
    tpb = T // tm
    resident = lambda a: pl.BlockSpec(a.shape, lambda i: (0, 0), pipeline_mode=pl.Buffered(1))
    return pl.pallas_call(
        functools.partial(_ffn_kernel, m0=m0),
        grid=(B * tpb,),
        in_specs=[pl.BlockSpec((None, tm, D_MODEL), lambda i: (i // tpb, i % tpb, 0)),
                  pl.BlockSpec((None, N_MOD, D_MODEL), lambda i: (i // tpb, 0, 0)),
                  pl.BlockSpec((1, D_MODEL), lambda i: (0, 0)),
                  pl.BlockSpec((1, D_MODEL), lambda i: (0, 0)),
                  resident(w_in), resident(w_out)],
        out_specs=pl.BlockSpec((None, tm, D_MODEL), lambda i: (i // tpb, i % tpb, 0)),
        out_shape=jax.ShapeDtypeStruct(x.shape, F32),
        compiler_params=_cparams(("parallel",)),
        name="ffn_block",
    )(x, mod, g_pre, g_post, w_in, w_out)


def _rope(x, cos, sin_signed):
    lane = lax.broadcasted_iota(jnp.int32, x.shape, 1)
    nxt = pltpu.roll(x, x.shape[1] - 1, 1)
    prv = pltpu.roll(x, 1, 1)
    swapped = jnp.where(lane % 2 == 0, nxt, prv)
    return x * cos + swapped * sin_signed


def _inproj_kernel(*refs, rope):
    if rope:
        (x_ref, mod_ref, g_ref, w_ref, wg_ref, cos_ref, sin_ref,
         ml_ref, mlo_ref, dn_ref, dng_ref, ft_ref, q_ref, k_ref, v_ref, gc_ref) = refs
    else:
        (x_ref, mod_ref, g_ref, w_ref, wg_ref,
         ml_ref, mlo_ref, dn_ref, dng_ref, ft_ref, q_ref, k_ref, v_ref, gc_ref) = refs
    h = (_rms(x_ref[...]) * g_ref[...] * (1.0 + mod_ref[4:5, :]) + mod_ref[3:4, :]).astype(BF16)

    def proj(lo, width):
        return jnp.dot(h, w_ref[:, lo:lo + width], preferred_element_type=F32)

    ml_ref[...] = proj(0, 3 * MIX_SEC)
    mlo_ref[...] = proj(3 * MIX_SEC, MIX_SEC)
    dn_ref[...] = proj(4 * MIX_SEC, 3 * MIX_SEC)
    dng_ref[...] = proj(7 * MIX_SEC, MIX_SEC)
    ft_ref[...] = proj(8 * MIX_SEC, MIX_SEC).astype(ft_ref.dtype)
    q = proj(9 * MIX_SEC, MIX_SEC)
    k = proj(10 * MIX_SEC, MIX_SEC)
    if rope:
        q = _rope(q, cos_ref[...], sin_ref[...])
        k = _rope(k, cos_ref[...], sin_ref[...])
    q_ref[...] = (q * ((DA_QK ** -0.5) * LOG2_E)).astype(q_ref.dtype)
    k_ref[...] = k.astype(k_ref.dtype)
    v_ref[...] = proj(11 * MIX_SEC, MIX_SEC).astype(v_ref.dtype)
    gc_ref[...] = jnp.dot(h, wg_ref[...], preferred_element_type=F32)


def _inproj(x, mod, g_pre, w_main, w_gate, rope_tabs):
    B, T, _ = x.shape
    tm = min(T, 512)
    tpb = T // tm
    rope = rope_tabs is not None
    tok = lambda w: pl.BlockSpec((None, tm, w), lambda i: (i // tpb, i % tpb, 0))
    full = lambda a: pl.BlockSpec(a.shape, lambda i: (0,) * a.ndim)
    in_specs = [tok(D_MODEL), pl.BlockSpec((None, N_MOD, D_MODEL), lambda i: (i // tpb, 0, 0)),
                full(g_pre), full(w_main), full(w_gate)]
    args = [x, mod, g_pre, w_main, w_gate]
    if rope:
        in_specs += [pl.BlockSpec((tm, MIX_SEC), lambda i: (i % tpb, 0))] * 2
        args += list(rope_tabs)
    widths = (3 * MIX_SEC, MIX_SEC, 3 * MIX_SEC, MIX_SEC, MIX_SEC, MIX_SEC, MIX_SEC, MIX_SEC, LANES)
    kv_dtype = BF16 if rope else F32
    dtypes = (F32, F32, F32, F32, BF16, BF16, kv_dtype, kv_dtype, F32)
    out_specs = [tok(w) for w in widths]
    out_shape = [jax.ShapeDtypeStruct((B, T, w), dt) for w, dt in zip(widths, dtypes)]
    return pl.pallas_call(
        functools.partial(_inproj_kernel, rope=rope),
        grid=(B * tpb,),
        in_specs=in_specs, out_specs=out_specs, out_shape=out_shape,
        compiler_params=_cparams(("parallel",)),
        name="in_proj",
    )(*args)


PREP_GROUP = 4


def _spread(a, b_bf16):
    return _mm_exact_rhs(a, b_bf16)


def _spread_lhs(a_bf16, b):
    return _mm_exact_lhs(a_bf16, b)


def _block_diag(x, seg):
    return jnp.concatenate([x] * HEADS, axis=0) * seg


def _block_diag_pair(pair, seg):
    return _block_diag(pair[0], seg), _block_diag(pair[1], seg)


def _packed_masks():
    row = lax.broadcasted_iota(jnp.int32, (CHUNK, MIX_SEC), 0)
    pos = lax.broadcasted_iota(jnp.int32, (CHUNK, MIX_SEC), 1) % CHUNK
    return (pos <= row, pos >= row), (pos < row, pos > row), jnp.where(pos == row, 1.0, 0.0)


def _cumsum_matrices():
    incl_lo, incl_up, _, _ = _tri_masks()
    return incl_lo.astype(BF16), incl_up.astype(BF16)


def _bc_to_row(x_bc, eye):
    return jnp.sum(x_bc * eye, axis=0, keepdims=True)


def _head_rows(blocks):
    return jnp.concatenate(blocks, axis=-1)


def _mlstm_kernel(x_ref, gc_ref, bias_ref, seg_ref, segf_ref, sel_ref, c0_ref, n0_ref, m0_ref,
                  h_ref, c_ref, n_ref, m_ref,
                  dl_scr, emu_scr, num_scr, sum_scr, ka_scr, nloc_scr, mul_scr, cbd_scr, nrow_scr, mrow_scr,
                  *, n_chunks):
    seg = seg_ref[...]
    incl, _, eye = _packed_masks()
    tri64 = _cumsum_matrices()
    lane_head = lax.broadcasted_iota(jnp.int32, (CHUNK, MIX_SEC), 1) // HEAD_DIM

    zero = jnp.zeros((HEAD_DIM, HEAD_DIM), F32)
    for d in range(2):
        for hd in range(HEADS):
            blocks = [zero] * hd + [c0_ref[d * HEADS + hd]] + [zero] * (HEADS - 1 - hd)
            cbd_scr[d, hd * HEAD_DIM:(hd + 1) * HEAD_DIM, :] = _head_rows(blocks)
        n_row = _head_rows([n0_ref[d * HEADS + hd] for hd in range(HEADS)])
        m_row = _head_rows([jnp.broadcast_to(m0_ref[d * HEADS + hd], (1, HEAD_DIM)) for hd in range(HEADS)])
        nrow_scr[d] = jnp.broadcast_to(n_row, (8, MIX_SEC))
        mrow_scr[d] = jnp.broadcast_to(m_row, (8, MIX_SEC))

    n_groups = n_chunks // PREP_GROUP
    jobs = [(ci, d) for ci in range(PREP_GROUP) for d in range(2)]

    def job_rows(g, ci, d):
        c = PREP_GROUP * g + ci
        return pl.ds(pl.multiple_of((c if d == 0 else n_chunks - 1 - c) * CHUNK, CHUNK), CHUNK)

    def prep_stages(g, base):
        rows = [job_rows(g, ci, d) for ci, d in jobs]
        xs = [x_ref[r, :] for r in rows]
        qs = [x[:, 0:MIX_SEC] for x in xs]
        ks = [x[:, MIX_SEC:2 * MIX_SEC] * (HEAD_DIM ** -0.5) for x in xs]
        vbd = [_block_diag(x[:, 2 * MIX_SEC:3 * MIX_SEC].astype(BF16), seg) for x in xs]
        gcols = [gc_ref[r, :] for r in rows]
        i128 = [gc + bias_ref[0:1, :] for gc in gcols]
        f128 = [_log_sigmoid(gc + bias_ref[1:2, :]) for gc in gcols]
        yield
        cums = [_spread_lhs(tri64[d], f) for f, (ci, d) in zip(f128, jobs)]
        qk = [lax.dot_general(q.astype(BF16), _block_diag(k.astype(BF16), seg), (((1,), (1,)), ((), ())),
                              preferred_element_type=F32) for q, k in zip(qs, ks)]
        yield
        b_bc = [_spread(cum, sel_ref[2 + d]) for cum, (ci, d) in zip(cums, jobs)]
        i_bc = [_spread(i, sel_ref[d]) for i, (ci, d) in zip(i128, jobs)]
        yield
        s_locs = []
        for j, (ci, d) in enumerate(jobs):
            b = b_bc[j]
            r_bc = i_bc[j] - b
            dmat = jnp.where(incl[d], b + _bc_to_row(r_bc, eye), -jnp.inf)
            mu = [jnp.max(dmat[:, hd * HEAD_DIM:(hd + 1) * HEAD_DIM], axis=-1, keepdims=True) for hd in range(HEADS)]
            mu_bc = jnp.where(lane_head == 0, mu[0], jnp.where(lane_head == 1, mu[1],
                                                               jnp.where(lane_head == 2, mu[2], mu[3])))
            s_locs.append(qk[j] * jnp.exp(dmat - mu_bc))
            last = CHUNK - 1 if d == 0 else 0
            mu_last = mu_bc[last:last + 1, :]
            ka = ks[j] * jnp.exp(b[last:last + 1, :] + r_bc - mu_last)
            dl_scr[base + j] = b - mu_bc
            emu_scr[base + j] = jnp.exp(-mu_bc)
            ka_scr[base + j] = ka.astype(BF16)
            nloc_scr[base + j] = jnp.broadcast_to(jnp.sum(ka, axis=0, keepdims=True), (8, MIX_SEC))
            mul_scr[base + j] = jnp.broadcast_to(mu_last, (8, MIX_SEC))
            if j % 2 == 1:
                yield
        nums = [jnp.dot(s.astype(BF16), w, preferred_element_type=F32) for s, w in zip(s_locs, vbd)]
        yield
        sums = [_mm_exact_rhs(s, seg) for s in s_locs]
        yield
        for j in range(len(jobs)):
            num_scr[base + j] = nums[j]
            sum_scr[base + j] = sums[j]

    def step_stages(g, base):
        for ci in range(PREP_GROUP):
            rows = [job_rows(g, ci, d) for d in range(2)]
            slots = [base + 2 * ci + d for d in range(2)]
            qs = [x_ref[rows[d], 0:MIX_SEC] for d in range(2)]
            v16 = [x_ref[rows[d], 2 * MIX_SEC:3 * MIX_SEC].astype(BF16) for d in range(2)]
            c_prev = [cbd_scr[d] for d in range(2)]
            n_prev = [nrow_scr[d][0:1, :] for d in range(2)]
            m_prev = [mrow_scr[d][0:1, :] for d in range(2)]
            qc = [jnp.dot(qs[d].astype(BF16), c_prev[d].astype(BF16), preferred_element_type=F32) for d in range(2)]
            qn = [_mm_exact_rhs(qs[d] * n_prev[d], seg) for d in range(2)]
            upd = [_mm_tn(ka_scr[slots[d]], v16[d]) for d in range(2)]
            yield
            for d in range(2):
                z = dl_scr[slots[d]] + m_prev[d]
                zp = jnp.maximum(z, 0.0)
                w_inter = jnp.exp(z - zp)
                corr = jnp.exp(-zp)
                num = w_inter * qc[d] + corr * num_scr[slots[d]]
                den = w_inter * qn[d] + corr * sum_scr[slots[d]]
                h_ref[d, rows[d], :] = num / jnp.maximum(jnp.abs(den), emu_scr[slots[d]] * corr)
                last = CHUNK - 1 if d == 0 else 0
                a_prev = w_inter[last:last + 1, :]
                gamma = corr[last:last + 1, :]
                cbd_scr[d] = a_prev * c_prev[d] + gamma * (upd[d] * segf_ref[...])
                nrow_scr[d] = jnp.broadcast_to(a_prev * n_prev[d] + gamma * nloc_scr[slots[d]][0:1, :], (8, MIX_SEC))
                mrow_scr[d] = jnp.broadcast_to(mul_scr[slots[d]][0:1, :] + zp[last:last + 1, :], (8, MIX_SEC))
            yield

    _pipelined_groups(n_groups, len(jobs), prep_stages, step_stages)

    for d in range(2):
        for hd in range(HEADS):
            sl = slice(hd * HEAD_DIM, (hd + 1) * HEAD_DIM)
            c_ref[d * HEADS + hd] = cbd_scr[d, sl, sl]
            n_ref[d * HEADS + hd] = nrow_scr[d][0:1, sl]
            m_ref[d * HEADS + hd] = mrow_scr[d][0:1, hd * HEAD_DIM:hd * HEAD_DIM + 1]


def _mlstm(ml_qkv, g_col, bias_row, seg, segf, sel, c0, n0, m0):
    B, T, _ = ml_qkv.shape
    nc = T // CHUNK
    per_b = lambda *s: pl.BlockSpec((None,) + s, lambda b: (b,) + (0,) * len(s))
    full = lambda a: pl.BlockSpec(a.shape, lambda b: (0,) * a.ndim)
    st_shapes = [(2 * HEADS, HEAD_DIM, HEAD_DIM), (2 * HEADS, 1, HEAD_DIM), (2 * HEADS, 1, 1)]
    per_chunk = lambda rows, dt: pltpu.VMEM((4 * PREP_GROUP, rows, MIX_SEC), dt)
    return pl.pallas_call(
        functools.partial(_mlstm_kernel, n_chunks=nc),
        grid=(B,),
        in_specs=[per_b(T, 3 * MIX_SEC), per_b(T, LANES), full(bias_row), full(seg), full(segf), full(sel)]
                 + [per_b(*s) for s in st_shapes],
        out_specs=[pl.BlockSpec((2, None, T, MIX_SEC), lambda b: (0, b, 0, 0))] + [per_b(*s) for s in st_shapes],
        out_shape=[jax.ShapeDtypeStruct((2, B, T, MIX_SEC), F32)]
                  + [jax.ShapeDtypeStruct((B,) + s, F32) for s in st_shapes],
        scratch_shapes=[per_chunk(CHUNK, F32), per_chunk(CHUNK, F32), per_chunk(CHUNK, F32), per_chunk(CHUNK, F32),
                        per_chunk(CHUNK, BF16), per_chunk(8, F32), per_chunk(8, F32),
                        pltpu.VMEM((2, MIX_SEC, MIX_SEC), F32), pltpu.VMEM((2, 8, MIX_SEC), F32),
                        pltpu.VMEM((2, 8, MIX_SEC), F32)],
        compiler_params=_cparams(("parallel",)),
        name="mlstm",
    )(ml_qkv, g_col, bias_row, seg, segf, sel, c0, n0, m0)


def _packed_mm3(l_pair, w_pair):
    (lh, ll), (wh, wl) = l_pair, w_pair
    rows = lh.shape[0]
    dot = functools.partial(jnp.dot, preferred_element_type=F32)
    top = dot(jnp.concatenate([lh, ll], axis=0), wh)
    return top[0:rows] + (top[rows:2 * rows] + dot(lh, wl))


def _pipelined_groups(n_groups, n_jobs, prep_stages, step_stages):
    _interleave(prep_stages(0, 0))

    def pair(i, carry):
        g = 2 * i + 1
        _interleave(prep_stages(g, n_jobs), step_stages(g - 1, 0))
        _interleave(prep_stages(g + 1, 0), step_stages(g, n_jobs))
        return carry

    lax.fori_loop(0, (n_groups - 1) // 2, pair, 0)
    last = n_groups - 1
    if last % 2 == 1:
        _interleave(prep_stages(last, n_jobs), step_stages(last - 1, 0))
    _interleave(step_stages(last, (last % 2) * n_jobs))


def _interleave(*stage_generators):
    live = list(stage_generators)
    while live:
        for gen in list(live):
            try:
                next(gen)
            except StopIteration:
                live.remove(gen)


def _packed_tri_inverse_stages(a_list, eye, seg, out):
    a_pairs = [_split(a) for a in a_list]
    ps = [eye - a for a in a_list]
    xs = [_packed_mm3(ap, _block_diag_pair(ap, seg)) for ap in a_pairs]
    yield
    for i in range(4):
        x_pairs = [_split(x) for x in xs]
        wxs = [_block_diag_pair(xp, seg) for xp in x_pairs]
        p_pairs = [_split(p) for p in ps]
        both = [_packed_mm3((jnp.concatenate([pp[0], xp[0]], axis=0), jnp.concatenate([pp[1], xp[1]], axis=0)), wx)
                for pp, xp, wx in zip(p_pairs, x_pairs, wxs)]
        ps = [p + pq[0:CHUNK] for p, pq in zip(ps, both)]
        xs = [pq[CHUNK:2 * CHUNK] for pq in both]
        yield
    wxs = [_block_diag_pair(_split(x), seg) for x in xs]
    out.extend(p + _packed_mm3(_split(p), wx) for p, wx in zip(ps, wxs))
    yield


def _deltanet_kernel(x_ref, gc_ref, conv_ref, par_row_ref, seg_ref, segf_ref, sel_ref,
                     s0_ref, o_ref, s_ref, u_scr, w_scr, qg_scr, attn_scr, kgt_scr, egl_scr, sbd_scr, *, n_chunks):
    T = n_chunks * CHUNK
    seg = seg_ref[...]
    incl, strict, eye = _packed_masks()
    tri64 = _cumsum_matrices()

    zero = jnp.zeros((HEAD_DIM, HEAD_DIM), F32)
    for d in range(2):
        for hd in range(HEADS):
            blocks = [zero] * hd + [s0_ref[d * HEADS + hd]] + [zero] * (HEADS - 1 - hd)
            sbd_scr[d, hd * HEAD_DIM:(hd + 1) * HEAD_DIM, :] = jnp.concatenate(blocks, axis=-1)

    def conv_silu(c):
        r0 = pl.multiple_of(c * CHUNK, CHUNK)
        xc = x_ref[pl.ds(r0, CHUNK), :]
        before = x_ref[pl.ds(pl.multiple_of(jnp.maximum(r0 - 8, 0), 8), 8), :][7:8, :]
        after = x_ref[pl.ds(pl.multiple_of(jnp.minimum(r0 + CHUNK, T - 8), 8), 8), :][0:1, :]
        before = jnp.where(c > 0, before, 0.0)
        after = jnp.where(c < n_chunks - 1, after, 0.0)
        rows = lax.broadcasted_iota(jnp.int32, xc.shape, 0)
        x_prev = jnp.where(rows == 0, before, pltpu.roll(xc, 1, 0))
        x_next = jnp.where(rows == CHUNK - 1, after, pltpu.roll(xc, CHUNK - 1, 0))
        return _silu(x_prev * conv_ref[0:1, :] + xc * conv_ref[1:2, :] + x_next * conv_ref[2:3, :])

    n_groups = n_chunks // PREP_GROUP
    jobs = [(ci, d) for ci in range(PREP_GROUP) for d in range(2)]

    def job_chunk(g, ci, d):
        c = PREP_GROUP * g + ci
        return c if d == 0 else n_chunks - 1 - c

    def prep_stages(g, base):
        chunks = [job_chunk(g, ci, d) for ci, d in jobs]
        ys = [conv_silu(c) for c in chunks]
        qs = [y[:, 0:MIX_SEC] for y in ys]
        ks = [y[:, MIX_SEC:2 * MIX_SEC] for y in ys]
        vs = [y[:, 2 * MIX_SEC:3 * MIX_SEC] for y in ys]
        gcols = [gc_ref[pl.ds(pl.multiple_of(c * CHUNK, CHUNK), CHUNK), :] for c in chunks]
        beta128 = [jax.nn.sigmoid(gc) for gc in gcols]
        la128 = [par_row_ref[0:1, :] * _softplus(gc + par_row_ref[1:2, :]) for gc in gcols]
        yield
        q_ss = [_head_mean_sq(q, seg_ref) for q in qs]
        k_ss = [_head_mean_sq(k, seg_ref) for k in ks]
        cums = [_spread_lhs(tri64[d], la) for la, (ci, d) in zip(la128, jobs)]
        yield
        qs = [q * lax.rsqrt(ss * HEAD_DIM + EPS) * (HEAD_DIM ** -0.5) for q, ss in zip(qs, q_ss)]
        ks = [k * lax.rsqrt(ss * HEAD_DIM + EPS) for k, ss in zip(ks, k_ss)]
        k16 = [k.astype(BF16) for k in ks]
        kq = [lax.dot_general(jnp.concatenate([kb, q.astype(BF16)], axis=0), _block_diag(kb, seg),
                              (((1,), (1,)), ((), ())), preferred_element_type=F32) for kb, q in zip(k16, qs)]
        g_bc = [_spread(cum, sel_ref[d]) for cum, (ci, d) in zip(cums, jobs)]
        beta_bc = [_spread(b128, sel_ref[2 + d]) for b128, (ci, d) in zip(beta128, jobs)]
        yield
        a_list, rhs_u, rhs_w = [], [], []
        for j, (ci, d) in enumerate(jobs):
            g_t, beta = g_bc[j], beta_bc[j]
            kk, qk = kq[j][0:CHUNK], kq[j][CHUNK:2 * CHUNK]
            decay = jnp.exp(jnp.where(incl[d], g_t - _bc_to_row(g_t, eye), -jnp.inf))
            a_list.append(jnp.where(strict[d], beta * kk * decay, 0.0))
            eg = jnp.exp(g_t)
            last = CHUNK - 1 if d == 0 else 0
            g_last = g_t[last:last + 1, :]
            qg_scr[base + j] = (qs[j] * eg).astype(BF16)
            attn_scr[base + j] = (qk * decay).astype(BF16)
            kgt_scr[base + j] = (ks[j] * jnp.exp(g_last - g_t)).T.astype(BF16)
            egl_scr[base + j] = jnp.broadcast_to(jnp.exp(g_last), (8, MIX_SEC))
            rhs_u.append(_split(vs[j] * beta))
            rhs_w.append(_split(ks[j] * beta * eg))
        yield
        t_inv = []
        yield from _packed_tri_inverse_stages(a_list, eye, seg, t_inv)
        t_pairs = [_split(t) for t in t_inv]
        us = [_packed_mm3(tp, _block_diag_pair(rhs, seg)) for tp, rhs in zip(t_pairs, rhs_u)]
        yield
        ws = [_packed_mm3(tp, _block_diag_pair(rhs, seg)) for tp, rhs in zip(t_pairs, rhs_w)]
        yield
        for j in range(len(jobs)):
            u_scr[base + j] = us[j]
            w_scr[base + j] = ws[j].astype(BF16)

    def step_stages(g, base):
        for ci in range(PREP_GROUP):
            slots = [base + 2 * ci + d for d in range(2)]
            s_prev = [sbd_scr[d] for d in range(2)]
            rs = [jnp.dot(jnp.concatenate([w_scr[slots[d]], qg_scr[slots[d]]], axis=0), s_prev[d].astype(BF16),
                          preferred_element_type=F32) for d in range(2)]
            yield
            v16 = [(u_scr[slots[d]] - rs[d][0:CHUNK]).astype(BF16) for d in range(2)]
            outs = [rs[d][CHUNK:2 * CHUNK] + jnp.dot(attn_scr[slots[d]], _block_diag(v16[d], seg),
                                                     preferred_element_type=F32) for d in range(2)]
            upds = [jnp.dot(kgt_scr[slots[d]], v16[d], preferred_element_type=F32) for d in range(2)]
            yield
            for d in range(2):
                r0 = pl.multiple_of(job_chunk(g, ci, d) * CHUNK, CHUNK)
                o_ref[d, pl.ds(r0, CHUNK), :] = outs[d]
                sbd_scr[d] = egl_scr[slots[d]][0:1, :] * s_prev[d] + upds[d] * segf_ref[...]
            yield

    _pipelined_groups(n_groups, len(jobs), prep_stages, step_stages)

    for d in range(2):
        for hd in range(HEADS):
            sl = slice(hd * HEAD_DIM, (hd + 1) * HEAD_DIM)
            s_ref[d * HEADS + hd] = sbd_scr[d, sl, sl]


def _deltanet(dn_qkv, g_col, conv_w, par_row, seg, segf, sel, s0):
    B, T, _ = dn_qkv.shape
    nc = T // CHUNK
    per_b = lambda *s: pl.BlockSpec((None,) + s, lambda b: (b,) + (0,) * len(s))
    full = lambda a: pl.BlockSpec(a.shape, lambda b: (0,) * a.ndim)
    st = (2 * HEADS, HEAD_DIM, HEAD_DIM)
    per_chunk = lambda rows, cols, dt: pltpu.VMEM((4 * PREP_GROUP, rows, cols), dt)
    return pl.pallas_call(
        functools.partial(_deltanet_kernel, n_chunks=nc),
        grid=(B,),
        in_specs=[per_b(T, 3 * MIX_SEC), per_b(T, LANES), full(conv_w), full(par_row),
                  full(seg), full(segf), full(sel), per_b(*st)],
        out_specs=[pl.BlockSpec((2, None, T, MIX_SEC), lambda b: (0, b, 0, 0)), per_b(*st)],
        out_shape=[jax.ShapeDtypeStruct((2, B, T, MIX_SEC), F32), jax.ShapeDtypeStruct((B,) + st, F32)],
        scratch_shapes=[per_chunk(CHUNK, MIX_SEC, F32), per_chunk(CHUNK, MIX_SEC, BF16),
                        per_chunk(CHUNK, MIX_SEC, BF16), per_chunk(CHUNK, MIX_SEC, BF16),
                        per_chunk(MIX_SEC, CHUNK, BF16), per_chunk(8, MIX_SEC, F32),
                        pltpu.VMEM((2, MIX_SEC, MIX_SEC), F32)],
        compiler_params=_cparams(("parallel",)),
        name="deltanet",
    )(dn_qkv, g_col, conv_w, par_row, seg, segf, sel, s0)


def _fourier_kernel(x_ref, cg_ref, sg_ref, ct_ref, st_ref, o_ref, p_scr, q_scr, *, scale):
    b = pl.program_id(1)

    @pl.when(pl.program_id(0) == 0)
    def _():
        x = x_ref[...].astype(BF16)
        p_scr[b] = jnp.dot(x, cg_ref[...], preferred_element_type=F32).astype(BF16)
        q_scr[b] = jnp.dot(x, sg_ref[...], preferred_element_type=F32).astype(BF16)

    y = (jnp.dot(ct_ref[...], p_scr[b], preferred_element_type=F32)
         - jnp.dot(st_ref[...], q_scr[b], preferred_element_type=F32))
    o_ref[...] = (y * scale).astype(o_ref.dtype)


def _dft_tables(T):
    def tab(n):
        idx = np.arange(n, dtype=np.int64)
        ang = 2.0 * np.pi * ((idx[:, None] * idx[None, :]) % n).astype(np.float64) / n
        return np.cos(ang), np.sin(ang)
    ct, st = tab(T)
    cg1, sg1 = tab(HEAD_DIM)
    eye = np.eye(HEADS)
    cg, sg = np.kron(eye, cg1), np.kron(eye, sg1)
    return tuple(jnp.asarray(a, dtype=F32).astype(BF16) for a in (cg, sg, ct, st))


def _fourier(x):
    B, T, _ = x.shape
    cg, sg, ct, st = _dft_tables(T)
    tr = min(T, 512)
    return pl.pallas_call(
        functools.partial(_fourier_kernel, scale=1.0 / math.sqrt(T * HEAD_DIM)),
        grid=(T // tr, B),
        in_specs=[pl.BlockSpec((None, T, MIX_SEC), lambda i, b: (jnp.where(i == 0, b, B - 1), 0, 0)),
                  pl.BlockSpec((MIX_SEC, MIX_SEC), lambda i, b: (0, 0)),
                  pl.BlockSpec((MIX_SEC, MIX_SEC), lambda i, b: (0, 0)),
                  pl.BlockSpec((tr, T), lambda i, b: (i, 0)),
                  pl.BlockSpec((tr, T), lambda i, b: (i, 0))],
        out_specs=pl.BlockSpec((None, tr, MIX_SEC), lambda i, b: (b, i, 0)),
        out_shape=jax.ShapeDtypeStruct((B, T, MIX_SEC), BF16),
        scratch_shapes=[pltpu.VMEM((B, T, MIX_SEC), BF16), pltpu.VMEM((B, T, MIX_SEC), BF16)],
        compiler_params=_cparams(("arbitrary", "arbitrary")),
        name="fourier",
    )(x, cg, sg, ct, st)


def _diffattn_kernel(*refs, lam_init, has_ctx):
    if has_ctx:
        q_ref, k_ref, v_ref, kc_ref, vc_ref, lam_ref, o_ref = refs
    else:
        q_ref, k_ref, v_ref, lam_ref, o_ref = refs
    lp = lam_ref[...]
    lam = (jnp.exp(jnp.sum(lp[0:1, :] * lp[1:2, :], axis=-1, keepdims=True))
           - jnp.exp(jnp.sum(lp[2:3, :] * lp[3:4, :], axis=-1, keepdims=True)) + lam_init)
    q = q_ref[...]
    pieces = [(k_ref[...].astype(BF16), v_ref[...].astype(BF16))]
    if has_ctx:
        pieces.append((kc_ref[...].astype(BF16), vc_ref[...].astype(BF16)))
    lane = lax.broadcasted_iota(jnp.int32, q.shape, 1)

    def scores(hd):
        maps = []
        for mp in range(2):
            lo = (hd * 2 + mp) * DA_QK
            qm = jnp.where((lane >= lo) & (lane < lo + DA_QK), q, jnp.zeros_like(q))
            maps.append([lax.dot_general(qm, kk, (((1,), (1,)), ((), ())), preferred_element_type=F32)
                         for kk, _ in pieces])
        return maps

    out = jnp.zeros(q.shape, F32)
    s_next = scores(0)
    for hd in range(HEADS):
        s_cur = s_next
        if hd + 1 < HEADS:
            s_next = scores(hd + 1)
        probs = []
        for mp in range(2):
            ss = s_cur[mp]
            mx = functools.reduce(jnp.maximum, [jnp.max(s, axis=-1, keepdims=True) for s in ss])
            es = [jnp.exp2(s - mx) for s in ss]
            den = functools.reduce(jnp.add, [jnp.sum(e, axis=-1, keepdims=True) for e in es])
            probs.append((es, 1.0 / den))
        (e0, r0), (e1, r1) = probs
        r1 = lam * r1
        acc = None
        for j, (_, vv) in enumerate(pieces):
            w = (e0[j] * r0 - e1[j] * r1).astype(BF16)
            part = jnp.dot(w, vv, preferred_element_type=F32)
            acc = part if acc is None else acc + part
        out = jnp.where((lane >= hd * HEAD_DIM) & (lane < (hd + 1) * HEAD_DIM), acc, out)
    o_ref[...] = out


def _diffattn(q, k, v, ctx, lam_par, lam_init):
    B, T, _ = q.shape
    tq = min(T, 512)
    has_ctx = ctx is not None
    in_specs = [pl.BlockSpec((None, tq, MIX_SEC), lambda b, i: (b, i, 0)),
                pl.BlockSpec((None, T, MIX_SEC), lambda b, i: (b, 0, 0)),
                pl.BlockSpec((None, T, MIX_SEC), lambda b, i: (b, 0, 0))]
    args = [q, k, v]
    if has_ctx:
        ck, cv, li = ctx
        P = ck.shape[2]
        in_specs += [pl.BlockSpec((None, None, P, MIX_SEC), lambda b, i: (b, li, 0, 0))] * 2
        args += [ck, cv]
    in_specs.append(pl.BlockSpec(lam_par.shape, lambda b, i: (0, 0)))
    args.append(lam_par)
    return pl.pallas_call(
        functools.partial(_diffattn_kernel, lam_init=lam_init, has_ctx=has_ctx),
        grid=(B, T // tq),
        in_specs=in_specs,
        out_specs=pl.BlockSpec((None, tq, MIX_SEC), lambda b, i: (b, i, 0)),
        out_shape=jax.ShapeDtypeStruct((B, T, MIX_SEC), F32),
        compiler_params=_cparams(("parallel", "arbitrary")),
        name="diff_attention",
    )(*args)


def _mixout_kernel(x_ref, mod_ref, gpost_ref, hml_ref, og_ref, odn_ref, dng_ref, ft_ref, oda_ref,
                   gains_ref, seg_ref, w_ref, o_ref, *, da_scale):
    def head_norm(z, gain_row):
        return z * lax.rsqrt(_head_mean_sq(z, seg_ref) + EPS) * gain_row

    h_ml = (hml_ref[0] + hml_ref[1]) * jax.nn.sigmoid(og_ref[...])
    y_ml = head_norm(h_ml, gains_ref[0:1, :])
    y_dn = head_norm(odn_ref[0] + odn_ref[1], gains_ref[1:2, :]) * _silu(dng_ref[...])
    y_da = head_norm(oda_ref[...], gains_ref[2:3, :]) * da_scale
    y = (_mm(y_ml, w_ref[0:MIX_SEC, :]) + _mm(y_dn, w_ref[MIX_SEC:2 * MIX_SEC, :])
         + _mm(ft_ref[...], w_ref[2 * MIX_SEC:3 * MIX_SEC, :]) + _mm(y_da, w_ref[3 * MIX_SEC:4 * MIX_SEC, :]))
    o_ref[...] = x_ref[...] + mod_ref[5:6, :] * (_rms(y) * gpost_ref[...])


def _mixout(x, mod, g_post, h_ml, ml, o_dn, dn_g, y_ft, o_da, gains, seg, w_out, da_scale):
    B, T, _ = x.shape
    tm = min(T, 512)
    tpb = T // tm
    n_tok = B * T
    x2 = lambda a: a.reshape(n_tok, a.shape[-1])
    h_ml = h_ml.reshape(2, n_tok, MIX_SEC)
    o_dn = o_dn.reshape(2, n_tok, MIX_SEC)
    tok = lambda w: pl.BlockSpec((tm, w), lambda i: (i, 0))
    two = pl.BlockSpec((2, tm, MIX_SEC), lambda i: (0, i, 0))
    full = lambda a: pl.BlockSpec(a.shape, lambda i: (0,) * a.ndim)
    out = pl.pallas_call(
        functools.partial(_mixout_kernel, da_scale=da_scale),
        grid=(n_tok // tm,),
        in_specs=[tok(D_MODEL), pl.BlockSpec((None, N_MOD, D_MODEL), lambda i: (i // tpb, 0, 0)), full(g_post),
                  two, tok(MIX_SEC), two, tok(MIX_SEC), tok(MIX_SEC), tok(MIX_SEC),
                  full(gains), full(seg), full(w_out)],
        out_specs=tok(D_MODEL),
        out_shape=jax.ShapeDtypeStruct((n_tok, D_MODEL), F32),
        compiler_params=_cparams(("parallel",)),
        name="mix_out",
    )(x2(x), mod, g_post, h_ml, x2(ml), o_dn, x2(dn_g), x2(y_ft), x2(o_da), gains, seg, w_out)
    return out.reshape(x.shape)


def _rope_tables(T):
    t = np.arange(T)
    row = (t // GRID_W).astype(np.float64)
    col = (t % GRID_W).astype(np.float64)
    inv = ROPE_BASE ** (-np.arange(ROPE_AXIS_PAIRS, dtype=np.float64) / ROPE_AXIS_PAIRS)
    ang = np.concatenate([row[:, None] * inv, col[:, None] * inv], axis=-1)
    cos = np.repeat(np.cos(ang), 2, axis=-1)
    sin = np.repeat(np.sin(ang), 2, axis=-1) * np.tile(np.array([-1.0, 1.0]), DA_QK // 2)
    reps = MIX_SEC // DA_QK
    return jnp.asarray(np.tile(cos, (1, reps)), dtype=F32), jnp.asarray(np.tile(sin, (1, reps)), dtype=F32)


def _layer_params(li, w_in, w_out, ffn_w_in, ffn_w_out, norm_pre, norm_post, ml_i_bias, ml_f_bias, ml_norm,
                  dn_conv, dn_A_log, dn_dt_bias, dn_norm, da_lambda, da_norm):
    w = w_in[li]
    main = jnp.concatenate([w[:, _OFF_ML:_OFF_MLG], w[:, _OFF_DN:_OFF_DNG], w[:, _OFF_FT:IN_WIDTH]], axis=1)
    gate = jnp.concatenate([w[:, _OFF_MLG:_OFF_DN], w[:, _OFF_DNG:_OFF_FT]], axis=1)
    on_lanes = lambda vec, lo: jnp.pad(vec.reshape(-1), (lo, LANES - lo - 2 * HEADS))
    ml_bias = jnp.stack([on_lanes(ml_i_bias[li], GATE_ML_I), on_lanes(ml_f_bias[li], GATE_ML_F)])
    dn_par_row = jnp.stack([on_lanes(-jnp.exp(dn_A_log[li]), GATE_DN_A), on_lanes(dn_dt_bias[li], GATE_DN_A)])
    return dict(
        w_main=main.astype(BF16),
        w_gate=jnp.pad(gate, ((0, 0), (0, LANES - N_GATE))).astype(BF16),
        w_out=w_out[li].astype(BF16),
        ffn_w_in=[ffn_w_in[li, j].astype(BF16) for j in range(2)],
        ffn_w_out=[ffn_w_out[li, j].astype(BF16) for j in range(2)],
        norm_pre=[norm_pre[li, j][None, :] for j in range(3)],
        norm_post=[norm_post[li, j][None, :] for j in range(3)],
        ml_bias_row=ml_bias,
        dn_par_row=dn_par_row,
        dn_conv=dn_conv[li],
        gains=jnp.stack([ml_norm[li].reshape(-1), jnp.tile(dn_norm[li], HEADS), jnp.tile(da_norm[li], HEADS)]),
        da_lambda=da_lambda[li],
    )


def _packed_consts():
    seg = np.kron(np.eye(HEADS), np.ones((HEAD_DIM, HEAD_DIM)))

    def spread(first_lanes):
        sel = np.zeros((4, LANES, MIX_SEC))
        for j, lo in enumerate(first_lanes):
            for hd in range(HEADS):
                sel[j, lo + hd, hd * HEAD_DIM:(hd + 1) * HEAD_DIM] = 1.0
        return jnp.asarray(sel, dtype=BF16)

    return dict(seg=jnp.asarray(seg, dtype=BF16), segf=jnp.asarray(seg, dtype=F32),
                sel_ml=spread((GATE_ML_I, GATE_ML_I + HEADS, GATE_ML_F, GATE_ML_F + HEADS)),
                sel_dn=spread((GATE_DN_A, GATE_DN_A + HEADS, GATE_DN_B, GATE_DN_B + HEADS)))


def _mixing(x_tok, mod_tok, shape_bt, lp, li, rope_tabs, states, ctx, consts):
    B, T = shape_bt
    seg, segf = consts['seg'], consts['segf']
    lam_init = 0.8 - 0.6 * math.exp(-0.3 * li)
    ml, ml_o, dn, dn_g, ft, da_q, da_k, da_v, g_col = _inproj(
        x_tok, mod_tok, lp['norm_pre'][1], lp['w_main'], lp['w_gate'], rope_tabs)
    seq = lambda a: a.reshape(B, T, a.shape[-1])
    c0, n0, m0, s0 = states
    h_ml, c_f, n_f, m_f = _mlstm(seq(ml), seq(g_col), lp['ml_bias_row'], seg, segf, consts['sel_ml'], c0, n0, m0)
    o_dn, s_f = _deltanet(seq(dn), seq(g_col), lp['dn_conv'], lp['dn_par_row'], seg, segf, consts['sel_dn'], s0)
    y_ft = _fourier(seq(ft))
    o_da = _diffattn(seq(da_q), seq(da_k), seq(da_v), ctx, lp['da_lambda'], lam_init)
    x_new = _mixout(x_tok, mod_tok, lp['norm_post'][1], h_ml, ml_o, o_dn, dn_g, y_ft, o_da, lp['gains'], seg,
                    lp['w_out'], 1.0 - lam_init)
    return x_new, (seq(da_k), seq(da_v), c_f, n_f, m_f, s_f)


def _trunk_layer(x_tok, mod_tok, shape_bt, lp, li, rope_tabs, states, ctx, consts):
    x_tok = _ffn_block(x_tok, mod_tok, lp['norm_pre'][0], lp['norm_post'][0], lp['ffn_w_in'][0], lp['ffn_w_out'][0], 0)
    x_tok, aux = _mixing(x_tok, mod_tok, shape_bt, lp, li, rope_tabs, states, ctx, consts)
    x_tok = _ffn_block(x_tok, mod_tok, lp['norm_pre'][2], lp['norm_post'][2], lp['ffn_w_in'][1], lp['ffn_w_out'][1], 6)
    return x_tok, aux


def kernel(x_prompt, x_sample, cache_k, cache_v, state_mlstm_C, state_mlstm_n, state_mlstm_m, state_delta_S, c, c_ctx, w_mod, b_mod, norm_pre, norm_post, ffn_w_in, ffn_w_out, w_in, w_out, ml_i_bias, ml_f_bias, ml_norm, dn_conv, dn_A_log, dn_dt_bias, dn_norm, da_lambda, da_norm):
    Bp, Tp, _ = x_prompt.shape
    Bs, Ts, _ = x_sample.shape
    P = cache_k.shape[2]
    n_st = 2 * HEADS

    cond = jnp.concatenate([c, c_ctx[None, :]], axis=0)
    cond = jnp.pad(cond, ((0, (-cond.shape[0]) % 8), (0, 0)))
    mod = _modulation(cond, w_mod, b_mod)
    rope_tabs = _rope_tables(Ts)
    consts = _packed_consts()
    ck = cache_k.reshape(Bs, DEPTH, P, MIX_SEC)
    cv = cache_v.reshape(Bs, DEPTH, P, MIX_SEC)
    zeros_p = (jnp.zeros((Bp, n_st, HEAD_DIM, HEAD_DIM), F32), jnp.zeros((Bp, n_st, 1, HEAD_DIM), F32),
               jnp.zeros((Bp, n_st, 1, 1), F32), jnp.zeros((Bp, n_st, HEAD_DIM, HEAD_DIM), F32))

    xp = x_prompt.reshape(1, Bp * Tp, D_MODEL)
    xs = x_sample
    ks, vs, Cs, ns, ms, Ss = [], [], [], [], [], []
    for li in range(DEPTH):
        lp = _layer_params(li, w_in, w_out, ffn_w_in, ffn_w_out, norm_pre, norm_post, ml_i_bias, ml_f_bias, ml_norm,
                           dn_conv, dn_A_log, dn_dt_bias, dn_norm, da_lambda, da_norm)
        mod_s = jnp.transpose(mod[li, :, :Bs], (1, 0, 2))
        mod_p = mod[li, :, Bs:Bs + 1].reshape(1, N_MOD, D_MODEL)
        xp, (k_c, v_c, C_c, n_c, m_c, S_c) = _trunk_layer(xp, mod_p, (Bp, Tp), lp, li, None, zeros_p, None, consts)
        ks.append(k_c.reshape(Bp, Tp, HEADS, 2, DA_QK))
        vs.append(v_c.reshape(Bp, Tp, HEADS, HEAD_DIM))
        Cs.append(C_c.reshape(Bp, 2, HEADS, HEAD_DIM, HEAD_DIM))
        ns.append(n_c.reshape(Bp, 2, HEADS, HEAD_DIM))
        ms.append(m_c.reshape(Bp, 2, HEADS))
        Ss.append(S_c.reshape(Bp, 2, HEADS, HEAD_DIM, HEAD_DIM))
        states = (state_mlstm_C[:, li].reshape(Bs, n_st, HEAD_DIM, HEAD_DIM),
                  state_mlstm_n[:, li].reshape(Bs, n_st, 1, HEAD_DIM),
                  state_mlstm_m[:, li].reshape(Bs, n_st, 1, 1),
                  state_delta_S[:, li].reshape(Bs, n_st, HEAD_DIM, HEAD_DIM))
        xs, _ = _trunk_layer(xs, mod_s, (Bs, Ts), lp, li, rope_tabs, states, (ck, cv, li), consts)
    return (xp.reshape(Bp, Tp, D_MODEL), xs, jnp.stack(ks, axis=1), jnp.stack(vs, axis=1), jnp.stack(Cs, axis=1),
            jnp.stack(ns, axis=1), jnp.stack(ms, axis=1), jnp.stack(Ss, axis=1))
```

```python
import functools
import math

import numpy as np
import jax
import jax.numpy as jnp
from jax import lax
from jax.experimental import pallas as pl
from jax.experimental.pallas import tpu as pltpu

D_MODEL = 1024
DEPTH = 2
GRID_W = 64
HEADS = 4
HEAD_DIM = 64
MIX_SEC = HEADS * HEAD_DIM
DA_QK = 32
D_FF = ((8 * D_MODEL // 3 + 127) // 128) * 128
N_MOD = 9
CHUNK = 64
N_GATE = 32
GATE_ML_I, GATE_ML_F, GATE_DN_B, GATE_DN_A = 0, 8, 16, 24
ROPE_AXIS_PAIRS = DA_QK // 4
ROPE_BASE = 10000.0
EPS = 1e-6
HALF = 0.5
LOG2_E = 1.4426950408889634
F32 = jnp.float32
BF16 = jnp.bfloat16
LANES = 128
VMEM_LIMIT = 56 * 1024 * 1024

_OFF_ML = 0
_OFF_MLG = 4 * MIX_SEC
_OFF_DN = _OFF_MLG + 16
_OFF_DNG = _OFF_DN + 4 * MIX_SEC
_OFF_FT = _OFF_DNG + 16
_OFF_DA = _OFF_FT + MIX_SEC
IN_WIDTH = _OFF_DA + 3 * MIX_SEC


def _cparams(sem):
    return pltpu.CompilerParams(dimension_semantics=sem, vmem_limit_bytes=VMEM_LIMIT)


def _mm(a, b):
    return jnp.dot(a.astype(BF16), b.astype(BF16), preferred_element_type=F32)


def _mm_tn(a, b):
    return lax.dot_general(a.astype(BF16), b.astype(BF16), (((0,), (0,)), ((), ())), preferred_element_type=F32)


def _split(a):
    hi = a.astype(BF16)
    lo = (a - hi.astype(F32)).astype(BF16)
    return hi, lo


def _mm_exact_rhs(a, b_bf16):
    ah, al = _split(a)
    dot = functools.partial(jnp.dot, preferred_element_type=F32)
    return dot(ah, b_bf16) + dot(al, b_bf16)


def _mm_exact_lhs(a_bf16, b):
    bh, bl = _split(b)
    dot = functools.partial(jnp.dot, preferred_element_type=F32)
    return dot(a_bf16, bh) + dot(a_bf16, bl)


def _rms(x):
    return x * lax.rsqrt(jnp.mean(x * x, axis=-1, keepdims=True) + EPS)


def _silu(x):
    return x * jax.nn.sigmoid(x)


def _log_sigmoid(x):
    return jnp.minimum(x, 0.0) - jnp.log1p(jnp.exp(-jnp.abs(x)))


def _softplus(x):
    return jnp.maximum(x, 0.0) + jnp.log1p(jnp.exp(-jnp.abs(x)))


def _head_mean_sq(x, seg_ref):
    return _mm_exact_rhs(x * x, seg_ref[...]) * (1.0 / HEAD_DIM)


def _tri_masks():
    row = lax.broadcasted_iota(jnp.int32, (CHUNK, CHUNK), 0)
    col = lax.broadcasted_iota(jnp.int32, (CHUNK, CHUNK), 1)
    return row >= col, row <= col, row > col, row < col


def _mod_kernel(c_ref, w_ref, b_ref, o_ref):
    c = c_ref[...]
    o_ref[...] = _mm(_silu(c), w_ref[...]) + b_ref[...]


def _modulation(cond, w_mod, b_mod):
    R = cond.shape[0]
    return pl.pallas_call(
        _mod_kernel,
        grid=(DEPTH, N_MOD),
        in_specs=[pl.BlockSpec((R, D_MODEL), lambda l, j: (0, 0)),
                  pl.BlockSpec((None, D_MODEL, D_MODEL), lambda l, j: (l, 0, j)),
                  pl.BlockSpec((None, None, 1, D_MODEL), lambda l, j: (l, j, 0, 0))],
        out_specs=pl.BlockSpec((None, None, R, D_MODEL), lambda l, j: (l, j, 0, 0)),
        out_shape=jax.ShapeDtypeStruct((DEPTH, N_MOD, R, D_MODEL), F32),
        compiler_params=_cparams(("arbitrary", "arbitrary")),
        name="modulation",
    )(cond, w_mod, b_mod.reshape(DEPTH, N_MOD, 1, D_MODEL))


FFN_SLICE = 256


def _ffn_kernel(x_ref, mod_ref, gpre_ref, gpost_ref, win_ref, wout_ref, o_ref, *, m0):
    x = x_ref[...]
    h = (_rms(x) * gpre_ref[...] * (1.0 + mod_ref[m0 + 1:m0 + 2, :]) + mod_ref[m0:m0 + 1, :]).astype(BF16)
    acc = None
    for lo in range(0, D_FF, FFN_SLICE):
        a = jnp.dot(h, win_ref[:, lo:lo + FFN_SLICE], preferred_element_type=F32)
        b = jnp.dot(h, win_ref[:, D_FF + lo:D_FF + lo + FFN_SLICE], preferred_element_type=F32)
        part = _mm(_silu(a) * b, wout_ref[lo:lo + FFN_SLICE, :])
        acc = part if acc is None else acc + part
    o_ref[...] = x + HALF * mod_ref[m0 + 2:m0 + 3, :] * (_rms(acc) * gpost_ref[...])


def _ffn_block(x, mod, g_pre, g_post, w_in, w_out, m0):
    B, T, _ = x.shape
    tm = min(T, 1024)
    tpb = T // tm
    resident = lambda a: pl.BlockSpec(a.shape, lambda i: (0, 0), pipeline_mode=pl.Buffered(1))
    return pl.pallas_call(
        functools.partial(_ffn_kernel, m0=m0),
        grid=(B * tpb,),
        in_specs=[pl.BlockSpec((None, tm, D_MODEL), lambda i: (i // tpb, i % tpb, 0)),
                  pl.BlockSpec((None, N_MOD, D_MODEL), lambda i: (i // tpb, 0, 0)),
                  pl.BlockSpec((1, D_MODEL), lambda i: (0, 0)),
                  pl.BlockSpec((1, D_MODEL), lambda i: (0, 0)),
                  resident(w_in), resident(w_out)],
        out_specs=pl.BlockSpec((None, tm, D_MODEL), lambda i: (i // tpb, i % tpb, 0)),
        out_shape=jax.ShapeDtypeStruct(x.shape, F32),
        compiler_params=_cparams(("parallel",)),
        name="ffn_block",
    )(x, mod, g_pre, g_post, w_in, w_out)


def _rope(x, cos, sin_signed):
    lane = lax.broadcasted_iota(jnp.int32, x.shape, 1)
    nxt = pltpu.roll(x, x.shape[1] - 1, 1)
    prv = pltpu.roll(x, 1, 1)
    swapped = jnp.where(lane % 2 == 0, nxt, prv)
    return x * cos + swapped * sin_signed


def _inproj_kernel(*refs, rope):
    if rope:
        (x_ref, mod_ref, g_ref, w_ref, wg_ref, cos_ref, sin_ref,
         ml_ref, mlo_ref, dn_ref, dng_ref, ft_ref, q_ref, k_ref, v_ref, gc_ref) = refs
    else:
        (x_ref, mod_ref, g_ref, w_ref, wg_ref,
         ml_ref, mlo_ref, dn_ref, dng_ref, ft_ref, q_ref, k_ref, v_ref, gc_ref) = refs
    h = (_rms(x_ref[...]) * g_ref[...] * (1.0 + mod_ref[4:5, :]) + mod_ref[3:4, :]).astype(BF16)

    def proj(lo, width):
        return jnp.dot(h, w_ref[:, lo:lo + width], preferred_element_type=F32)

    ml_ref[...] = proj(0, 3 * MIX_SEC)
    mlo_ref[...] = proj(3 * MIX_SEC, MIX_SEC).astype(mlo_ref.dtype)
    dn_ref[...] = proj(4 * MIX_SEC, 3 * MIX_SEC)
    dng_ref[...] = proj(7 * MIX_SEC, MIX_SEC).astype(dng_ref.dtype)
    ft_ref[...] = proj(8 * MIX_SEC, MIX_SEC).astype(ft_ref.dtype)
    q = proj(9 * MIX_SEC, MIX_SEC)
    k = proj(10 * MIX_SEC, MIX_SEC)
    if rope:
        q = _rope(q, cos_ref[...], sin_ref[...])
        k = _rope(k, cos_ref[...], sin_ref[...])
    q_ref[...] = (q * ((DA_QK ** -0.5) * LOG2_E)).astype(q_ref.dtype)
    k_ref[...] = k.astype(k_ref.dtype)
    v_ref[...] = proj(11 * MIX_SEC, MIX_SEC).astype(v_ref.dtype)
    gc_ref[...] = jnp.dot(h, wg_ref[...], preferred_element_type=F32)


def _inproj(x, mod, g_pre, w_main, w_gate, rope_tabs):
    B, T, _ = x.shape
    tm = min(T, 512)
    tpb = T // tm
    rope = rope_tabs is not None
    tok = lambda w: pl.BlockSpec((None, tm, w), lambda i: (i // tpb, i % tpb, 0))
    full = lambda a: pl.BlockSpec(a.shape, lambda i: (0,) * a.ndim)
    in_specs = [tok(D_MODEL), pl.BlockSpec((None, N_MOD, D_MODEL), lambda i: (i // tpb, 0, 0)),
                full(g_pre), full(w_main), full(w_gate)]
    args = [x, mod, g_pre, w_main, w_gate]
    if rope:
        in_specs += [pl.BlockSpec((tm, MIX_SEC), lambda i: (i % tpb, 0))] * 2
        args += list(rope_tabs)
    widths = (3 * MIX_SEC, MIX_SEC, 3 * MIX_SEC, MIX_SEC, MIX_SEC, MIX_SEC, MIX_SEC, MIX_SEC, LANES)
    kv_dtype = BF16 if rope else F32
    dtypes = (F32, BF16, F32, BF16, BF16, BF16, kv_dtype, kv_dtype, F32)
    out_specs = [tok(w) for w in widths]
    out_shape = [jax.ShapeDtypeStruct((B, T, w), dt) for w, dt in zip(widths, dtypes)]
    return pl.pallas_call(
        functools.partial(_inproj_kernel, rope=rope),
        grid=(B * tpb,),
        in_specs=in_specs, out_specs=out_specs, out_shape=out_shape,
        compiler_params=_cparams(("parallel",)),
        name="in_proj",
    )(*args)


PREP_GROUP = 4


def _spread(a, b_bf16):
    return _mm_exact_rhs(a, b_bf16)


def _spread_lhs(a_bf16, b):
    return _mm_exact_lhs(a_bf16, b)


def _block_diag(x, seg):
    return jnp.concatenate([x] * HEADS, axis=0) * seg


def _block_diag_pair(pair, seg):
    return _block_diag(pair[0], seg), _block_diag(pair[1], seg)


def _packed_masks():
    row = lax.broadcasted_iota(jnp.int32, (CHUNK, MIX_SEC), 0)
    pos = lax.broadcasted_iota(jnp.int32, (CHUNK, MIX_SEC), 1) % CHUNK
    return (pos <= row, pos >= row), (pos < row, pos > row), jnp.where(pos == row, 1.0, 0.0)


def _cumsum_matrices():
    incl_lo, incl_up, _, _ = _tri_masks()
    return incl_lo.astype(BF16), incl_up.astype(BF16)


def _bc_to_row(x_bc, eye):
    return jnp.sum(x_bc * eye, axis=0, keepdims=True)


def _head_rows(blocks):
    return jnp.concatenate(blocks, axis=-1)


def _mlstm_kernel(x_ref, gc_ref, bias_ref, seg_ref, segf_ref, sel_ref, c0_ref, n0_ref, m0_ref,
                  h_ref, c_ref, n_ref, m_ref,
                  dl_scr, emu_scr, num_scr, sum_scr, ka_scr, nloc_scr, mul_scr, cbd_scr, nrow_scr, mrow_scr,
                  *, n_chunks):
    seg = seg_ref[...]
    incl, _, eye = _packed_masks()
    tri64 = _cumsum_matrices()
    lane_head = lax.broadcasted_iota(jnp.int32, (CHUNK, MIX_SEC), 1) // HEAD_DIM

    zero = jnp.zeros((HEAD_DIM, HEAD_DIM), F32)
    for d in range(2):
        for hd in range(HEADS):
            blocks = [zero] * hd + [c0_ref[d * HEADS + hd]] + [zero] * (HEADS - 1 - hd)
            cbd_scr[d, hd * HEAD_DIM:(hd + 1) * HEAD_DIM, :] = _head_rows(blocks)
        n_row = _head_rows([n0_ref[d * HEADS + hd] for hd in range(HEADS)])
        m_row = _head_rows([jnp.broadcast_to(m0_ref[d * HEADS + hd], (1, HEAD_DIM)) for hd in range(HEADS)])
        nrow_scr[d] = jnp.broadcast_to(n_row, (8, MIX_SEC))
        mrow_scr[d] = jnp.broadcast_to(m_row, (8, MIX_SEC))

    n_groups = n_chunks // PREP_GROUP
    jobs = [(ci, d) for ci in range(PREP_GROUP) for d in range(2)]

    def job_rows(g, ci, d):
        c = PREP_GROUP * g + ci
        return pl.ds(pl.multiple_of((c if d == 0 else n_chunks - 1 - c) * CHUNK, CHUNK), CHUNK)

    def prep_stages(g, base):
        rows = [job_rows(g, ci, d) for ci, d in jobs]
        xs = [x_ref[r, :] for r in rows]
        qs = [x[:, 0:MIX_SEC] for x in xs]
        ks = [x[:, MIX_SEC:2 * MIX_SEC] * (HEAD_DIM ** -0.5) for x in xs]
        vbd = [_block_diag(x[:, 2 * MIX_SEC:3 * MIX_SEC].astype(BF16), seg) for x in xs]
        gcols = [gc_ref[r, :] for r in rows]
        i128 = [gc + bias_ref[0:1, :] for gc in gcols]
        f128 = [_log_sigmoid(gc + bias_ref[1:2, :]) for gc in gcols]
        yield
        cums = [_spread_lhs(tri64[d], f) for f, (ci, d) in zip(f128, jobs)]
        qk = [lax.dot_general(q.astype(BF16), _block_diag(k.astype(BF16), seg), (((1,), (1,)), ((), ())),
                              preferred_element_type=F32) for q, k in zip(qs, ks)]
        yield
        b_bc = [_spread(cum, sel_ref[2 + d]) for cum, (ci, d) in zip(cums, jobs)]
        i_bc = [_spread(i, sel_ref[d]) for i, (ci, d) in zip(i128, jobs)]
        yield
        s_locs = []
        for j, (ci, d) in enumerate(jobs):
            b = b_bc[j]
            r_bc = i_bc[j] - b
            dmat = jnp.where(incl[d], b + _bc_to_row(r_bc, eye), -jnp.inf)
            mu = [jnp.max(dmat[:, hd * HEAD_DIM:(hd + 1) * HEAD_DIM], axis=-1, keepdims=True) for hd in range(HEADS)]
            mu_bc = jnp.where(lane_head == 0, mu[0], jnp.where(lane_head == 1, mu[1],
                                                               jnp.where(lane_head == 2, mu[2], mu[3])))
            s_locs.append(qk[j] * jnp.exp(dmat - mu_bc))
            last = CHUNK - 1 if d == 0 else 0
            mu_last = mu_bc[last:last + 1, :]
            ka = ks[j] * jnp.exp(b[last:last + 1, :] + r_bc - mu_last)
            dl_scr[base + j] = b - mu_bc
            emu_scr[base + j] = jnp.exp(-mu_bc)
            ka_scr[base + j] = ka.astype(BF16)
            nloc_scr[base + j] = jnp.broadcast_to(jnp.sum(ka, axis=0, keepdims=True), (8, MIX_SEC))
            mul_scr[base + j] = jnp.broadcast_to(mu_last, (8, MIX_SEC))
            if j % 2 == 1:
                yield
        nums = [jnp.dot(s.astype(BF16), w, preferred_element_type=F32) for s, w in zip(s_locs, vbd)]
        yield
        sums = [_mm_exact_rhs(s, seg) for s in s_locs]
        yield
        for j in range(len(jobs)):
            num_scr[base + j] = nums[j]
            sum_scr[base + j] = sums[j]

    def step_stages(g, base):
        for ci in range(PREP_GROUP):
            rows = [job_rows(g, ci, d) for d in range(2)]
            slots = [base + 2 * ci + d for d in range(2)]
            qs = [x_ref[rows[d], 0:MIX_SEC] for d in range(2)]
            v16 = [x_ref[rows[d], 2 * MIX_SEC:3 * MIX_SEC].astype(BF16) for d in range(2)]
            c_prev = [cbd_scr[d] for d in range(2)]
            n_prev = [nrow_scr[d][0:1, :] for d in range(2)]
            m_prev = [mrow_scr[d][0:1, :] for d in range(2)]
            qc = [jnp.dot(qs[d].astype(BF16), c_prev[d].astype(BF16), preferred_element_type=F32) for d in range(2)]
            qn = [_mm_exact_rhs(qs[d] * n_prev[d], seg) for d in range(2)]
            upd = [_mm_tn(ka_scr[slots[d]], v16[d]) for d in range(2)]
            yield
            for d in range(2):
                z = dl_scr[slots[d]] + m_prev[d]
                zp = jnp.maximum(z, 0.0)
                w_inter = jnp.exp(z - zp)
                corr = jnp.exp(-zp)
                num = w_inter * qc[d] + corr * num_scr[slots[d]]
                den = w_inter * qn[d] + corr * sum_scr[slots[d]]
                h_ref[d, rows[d], :] = (num / jnp.maximum(jnp.abs(den), emu_scr[slots[d]] * corr)).astype(h_ref.dtype)
                last = CHUNK - 1 if d == 0 else 0
                a_prev = w_inter[last:last + 1, :]
                gamma = corr[last:last + 1, :]
                cbd_scr[d] = a_prev * c_prev[d] + gamma * (upd[d] * segf_ref[...])
                nrow_scr[d] = jnp.broadcast_to(a_prev * n_prev[d] + gamma * nloc_scr[slots[d]][0:1, :], (8, MIX_SEC))
                mrow_scr[d] = jnp.broadcast_to(mul_scr[slots[d]][0:1, :] + zp[last:last + 1, :], (8, MIX_SEC))
            yield

    _pipelined_groups(n_groups, len(jobs), prep_stages, step_stages)

    for d in range(2):
        for hd in range(HEADS):
            sl = slice(hd * HEAD_DIM, (hd + 1) * HEAD_DIM)
            c_ref[d * HEADS + hd] = cbd_scr[d, sl, sl]
            n_ref[d * HEADS + hd] = nrow_scr[d][0:1, sl]
            m_ref[d * HEADS + hd] = mrow_scr[d][0:1, hd * HEAD_DIM:hd * HEAD_DIM + 1]


def _mlstm(ml_qkv, g_col, bias_row, seg, segf, sel, c0, n0, m0):
    B, T, _ = ml_qkv.shape
    nc = T // CHUNK
    per_b = lambda *s: pl.BlockSpec((None,) + s, lambda b: (b,) + (0,) * len(s))
    full = lambda a: pl.BlockSpec(a.shape, lambda b: (0,) * a.ndim)
    st_shapes = [(2 * HEADS, HEAD_DIM, HEAD_DIM), (2 * HEADS, 1, HEAD_DIM), (2 * HEADS, 1, 1)]
    per_chunk = lambda rows, dt: pltpu.VMEM((4 * PREP_GROUP, rows, MIX_SEC), dt)
    return pl.pallas_call(
        functools.partial(_mlstm_kernel, n_chunks=nc),
        grid=(B,),
        in_specs=[per_b(T, 3 * MIX_SEC), per_b(T, LANES), full(bias_row), full(seg), full(segf), full(sel)]
                 + [per_b(*s) for s in st_shapes],
        out_specs=[pl.BlockSpec((2, None, T, MIX_SEC), lambda b: (0, b, 0, 0))] + [per_b(*s) for s in st_shapes],
        out_shape=[jax.ShapeDtypeStruct((2, B, T, MIX_SEC), BF16)]
                  + [jax.ShapeDtypeStruct((B,) + s, F32) for s in st_shapes],
        scratch_shapes=[per_chunk(CHUNK, F32), per_chunk(CHUNK, F32), per_chunk(CHUNK, F32), per_chunk(CHUNK, F32),
                        per_chunk(CHUNK, BF16), per_chunk(8, F32), per_chunk(8, F32),
                        pltpu.VMEM((2, MIX_SEC, MIX_SEC), F32), pltpu.VMEM((2, 8, MIX_SEC), F32),
                        pltpu.VMEM((2, 8, MIX_SEC), F32)],
        compiler_params=_cparams(("parallel",)),
        name="mlstm",
    )(ml_qkv, g_col, bias_row, seg, segf, sel, c0, n0, m0)


def _packed_mm3(l_pair, w_pair):
    (lh, ll), (wh, wl) = l_pair, w_pair
    rows = lh.shape[0]
    dot = functools.partial(jnp.dot, preferred_element_type=F32)
    top = dot(jnp.concatenate([lh, ll], axis=0), wh)
    return top[0:rows] + (top[rows:2 * rows] + dot(lh, wl))


def _pipelined_groups(n_groups, n_jobs, prep_stages, step_stages):
    _interleave(prep_stages(0, 0))

    def pair(i, carry):
        g = 2 * i + 1
        _interleave(prep_stages(g, n_jobs), step_stages(g - 1, 0))
        _interleave(prep_stages(g + 1, 0), step_stages(g, n_jobs))
        return carry

    lax.fori_loop(0, (n_groups - 1) // 2, pair, 0)
    last = n_groups - 1
    if last % 2 == 1:
        _interleave(prep_stages(last, n_jobs), step_stages(last - 1, 0))
    _interleave(step_stages(last, (last % 2) * n_jobs))


def _interleave(*stage_generators):
    live = list(stage_generators)
    while live:
        for gen in list(live):
            try:
                next(gen)
            except StopIteration:
                live.remove(gen)


def _packed_tri_inverse_stages(a_list, eye, seg, out):
    a_pairs = [_split(a) for a in a_list]
    ps = [eye - a for a in a_list]
    xs = [_packed_mm3(ap, _block_diag_pair(ap, seg)) for ap in a_pairs]
    yield
    for i in range(4):
        x_pairs = [_split(x) for x in xs]
        wxs = [_block_diag_pair(xp, seg) for xp in x_pairs]
        p_pairs = [_split(p) for p in ps]
        both = [_packed_mm3((jnp.concatenate([pp[0], xp[0]], axis=0), jnp.concatenate([pp[1], xp[1]], axis=0)), wx)
                for pp, xp, wx in zip(p_pairs, x_pairs, wxs)]
        ps = [p + pq[0:CHUNK] for p, pq in zip(ps, both)]
        xs = [pq[CHUNK:2 * CHUNK] for pq in both]
        yield
    wxs = [_block_diag_pair(_split(x), seg) for x in xs]
    out.extend(p + _packed_mm3(_split(p), wx) for p, wx in zip(ps, wxs))
    yield


def _deltanet_kernel(x_ref, gc_ref, conv_ref, par_row_ref, seg_ref, segf_ref, sel_ref,
                     s0_ref, o_ref, s_ref, u_scr, w_scr, qg_scr, attn_scr, kgt_scr, egl_scr, sbd_scr, *, n_chunks):
    T = n_chunks * CHUNK
    seg = seg_ref[...]
    incl, strict, eye = _packed_masks()
    tri64 = _cumsum_matrices()

    zero = jnp.zeros((HEAD_DIM, HEAD_DIM), F32)
    for d in range(2):
        for hd in range(HEADS):
            blocks = [zero] * hd + [s0_ref[d * HEADS + hd]] + [zero] * (HEADS - 1 - hd)
            sbd_scr[d, hd * HEAD_DIM:(hd + 1) * HEAD_DIM, :] = jnp.concatenate(blocks, axis=-1)

    def conv_silu(c):
        r0 = pl.multiple_of(c * CHUNK, CHUNK)
        xc = x_ref[pl.ds(r0, CHUNK), :]
        before = x_ref[pl.ds(pl.multiple_of(jnp.maximum(r0 - 8, 0), 8), 8), :][7:8, :]
        after = x_ref[pl.ds(pl.multiple_of(jnp.minimum(r0 + CHUNK, T - 8), 8), 8), :][0:1, :]
        before = jnp.where(c > 0, before, 0.0)
        after = jnp.where(c < n_chunks - 1, after, 0.0)
        rows = lax.broadcasted_iota(jnp.int32, xc.shape, 0)
        x_prev = jnp.where(rows == 0, before, pltpu.roll(xc, 1, 0))
        x_next = jnp.where(rows == CHUNK - 1, after, pltpu.roll(xc, CHUNK - 1, 0))
        return _silu(x_prev * conv_ref[0:1, :] + xc * conv_ref[1:2, :] + x_next * conv_ref[2:3, :])

    n_groups = n_chunks // PREP_GROUP
    jobs = [(ci, d) for ci in range(PREP_GROUP) for d in range(2)]

    def job_chunk(g, ci, d):
        c = PREP_GROUP * g + ci
        return c if d == 0 else n_chunks - 1 - c

    def prep_stages(g, base):
        chunks = [job_chunk(g, ci, d) for ci, d in jobs]
        ys = [conv_silu(c) for c in chunks]
        qs = [y[:, 0:MIX_SEC] for y in ys]
        ks = [y[:, MIX_SEC:2 * MIX_SEC] for y in ys]
        vs = [y[:, 2 * MIX_SEC:3 * MIX_SEC] for y in ys]
        gcols = [gc_ref[pl.ds(pl.multiple_of(c * CHUNK, CHUNK), CHUNK), :] for c in chunks]
        beta128 = [jax.nn.sigmoid(gc) for gc in gcols]
        la128 = [par_row_ref[0:1, :] * _softplus(gc + par_row_ref[1:2, :]) for gc in gcols]
        yield
        q_ss = [_head_mean_sq(q, seg_ref) for q in qs]
        k_ss = [_head_mean_sq(k, seg_ref) for k in ks]
        cums = [_spread_lhs(tri64[d], la) for la, (ci, d) in zip(la128, jobs)]
        yield
        qs = [q * lax.rsqrt(ss * HEAD_DIM + EPS) * (HEAD_DIM ** -0.5) for q, ss in zip(qs, q_ss)]
        ks = [k * lax.rsqrt(ss * HEAD_DIM + EPS) for k, ss in zip(ks, k_ss)]
        k16 = [k.astype(BF16) for k in ks]
        kq = [lax.dot_general(jnp.concatenate([kb, q.astype(BF16)], axis=0), _block_diag(kb, seg),
                              (((1,), (1,)), ((), ())), preferred_element_type=F32) for kb, q in zip(k16, qs)]
        g_bc = [_spread(cum, sel_ref[d]) for cum, (ci, d) in zip(cums, jobs)]
        beta_bc = [_spread(b128, sel_ref[2 + d]) for b128, (ci, d) in zip(beta128, jobs)]
        yield
        a_list, rhs_u, rhs_w = [], [], []
        for j, (ci, d) in enumerate(jobs):
            g_t, beta = g_bc[j], beta_bc[j]
            kk, qk = kq[j][0:CHUNK], kq[j][CHUNK:2 * CHUNK]
            decay = jnp.exp(jnp.where(incl[d], g_t - _bc_to_row(g_t, eye), -jnp.inf))
            a_list.append(jnp.where(strict[d], beta * kk * decay, 0.0))
            eg = jnp.exp(g_t)
            last = CHUNK - 1 if d == 0 else 0
            g_last = g_t[last:last + 1, :]
            qg_scr[base + j] = (qs[j] * eg).astype(BF16)
            attn_scr[base + j] = (qk * decay).astype(BF16)
            kgt_scr[base + j] = (ks[j] * jnp.exp(g_last - g_t)).T.astype(BF16)
            egl_scr[base + j] = jnp.broadcast_to(jnp.exp(g_last), (8, MIX_SEC))
            rhs_u.append(_split(vs[j] * beta))
            rhs_w.append(_split(ks[j] * beta * eg))
        yield
        t_inv = []
        yield from _packed_tri_inverse_stages(a_list, eye, seg, t_inv)
        t_pairs = [_split(t) for t in t_inv]
        us = [_packed_mm3(tp, _block_diag_pair(rhs, seg)) for tp, rhs in zip(t_pairs, rhs_u)]
        yield
        ws = [_packed_mm3(tp, _block_diag_pair(rhs, seg)) for tp, rhs in zip(t_pairs, rhs_w)]
        yield
        for j in range(len(jobs)):
            u_scr[base + j] = us[j]
            w_scr[base + j] = ws[j].astype(BF16)

    def step_stages(g, base):
        for ci in range(PREP_GROUP):
            slots = [base + 2 * ci + d for d in range(2)]
            s_prev = [sbd_scr[d] for d in range(2)]
            rs = [jnp.dot(jnp.concatenate([w_scr[slots[d]], qg_scr[slots[d]]], axis=0), s_prev[d].astype(BF16),
                          preferred_element_type=F32) for d in range(2)]
            yield
            v16 = [(u_scr[slots[d]] - rs[d][0:CHUNK]).astype(BF16) for d in range(2)]
            outs = [rs[d][CHUNK:2 * CHUNK] + jnp.dot(attn_scr[slots[d]], _block_diag(v16[d], seg),
                                                     preferred_element_type=F32) for d in range(2)]
            upds = [jnp.dot(kgt_scr[slots[d]], v16[d], preferred_element_type=F32) for d in range(2)]
            yield
            for d in range(2):
                r0 = pl.multiple_of(job_chunk(g, ci, d) * CHUNK, CHUNK)
                o_ref[d, pl.ds(r0, CHUNK), :] = outs[d].astype(o_ref.dtype)
                sbd_scr[d] = egl_scr[slots[d]][0:1, :] * s_prev[d] + upds[d] * segf_ref[...]
            yield

    _pipelined_groups(n_groups, len(jobs), prep_stages, step_stages)

    for d in range(2):
        for hd in range(HEADS):
            sl = slice(hd * HEAD_DIM, (hd + 1) * HEAD_DIM)
            s_ref[d * HEADS + hd] = sbd_scr[d, sl, sl]


def _deltanet(dn_qkv, g_col, conv_w, par_row, seg, segf, sel, s0):
    B, T, _ = dn_qkv.shape
    nc = T // CHUNK
    per_b = lambda *s: pl.BlockSpec((None,) + s, lambda b: (b,) + (0,) * len(s))
    full = lambda a: pl.BlockSpec(a.shape, lambda b: (0,) * a.ndim)
    st = (2 * HEADS, HEAD_DIM, HEAD_DIM)
    per_chunk = lambda rows, cols, dt: pltpu.VMEM((4 * PREP_GROUP, rows, cols), dt)
    return pl.pallas_call(
        functools.partial(_deltanet_kernel, n_chunks=nc),
        grid=(B,),
        in_specs=[per_b(T, 3 * MIX_SEC), per_b(T, LANES), full(conv_w), full(par_row),
                  full(seg), full(segf), full(sel), per_b(*st)],
        out_specs=[pl.BlockSpec((2, None, T, MIX_SEC), lambda b: (0, b, 0, 0)), per_b(*st)],
        out_shape=[jax.ShapeDtypeStruct((2, B, T, MIX_SEC), BF16), jax.ShapeDtypeStruct((B,) + st, F32)],
        scratch_shapes=[per_chunk(CHUNK, MIX_SEC, F32), per_chunk(CHUNK, MIX_SEC, BF16),
                        per_chunk(CHUNK, MIX_SEC, BF16), per_chunk(CHUNK, MIX_SEC, BF16),
                        per_chunk(MIX_SEC, CHUNK, BF16), per_chunk(8, MIX_SEC, F32),
                        pltpu.VMEM((2, MIX_SEC, MIX_SEC), F32)],
        compiler_params=_cparams(("parallel",)),
        name="deltanet",
    )(dn_qkv, g_col, conv_w, par_row, seg, segf, sel, s0)


def _fourier_kernel(x_ref, cg_ref, sg_ref, ct_ref, st_ref, o_ref, p_scr, q_scr, *, scale):
    b = pl.program_id(1)

    @pl.when(pl.program_id(0) == 0)
    def _():
        x = x_ref[...].astype(BF16)
        p_scr[b] = jnp.dot(x, cg_ref[...], preferred_element_type=F32).astype(BF16)
        q_scr[b] = jnp.dot(x, sg_ref[...], preferred_element_type=F32).astype(BF16)

    y = (jnp.dot(ct_ref[...], p_scr[b], preferred_element_type=F32)
         - jnp.dot(st_ref[...], q_scr[b], preferred_element_type=F32))
    o_ref[...] = (y * scale).astype(o_ref.dtype)


def _dft_tables(T):
    def tab(n):
        idx = np.arange(n, dtype=np.int64)
        ang = 2.0 * np.pi * ((idx[:, None] * idx[None, :]) % n).astype(np.float64) / n
        return np.cos(ang), np.sin(ang)
    ct, st = tab(T)
    cg1, sg1 = tab(HEAD_DIM)
    eye = np.eye(HEADS)
    cg, sg = np.kron(eye, cg1), np.kron(eye, sg1)
    return tuple(jnp.asarray(a, dtype=F32).astype(BF16) for a in (cg, sg, ct, st))


def _fourier(x):
    B, T, _ = x.shape
    cg, sg, ct, st = _dft_tables(T)
    tr = min(T, 512)
    return pl.pallas_call(
        functools.partial(_fourier_kernel, scale=1.0 / math.sqrt(T * HEAD_DIM)),
        grid=(T // tr, B),
        in_specs=[pl.BlockSpec((None, T, MIX_SEC), lambda i, b: (jnp.where(i == 0, b, B - 1), 0, 0)),
                  pl.BlockSpec((MIX_SEC, MIX_SEC), lambda i, b: (0, 0)),
                  pl.BlockSpec((MIX_SEC, MIX_SEC), lambda i, b: (0, 0)),
                  pl.BlockSpec((tr, T), lambda i, b: (i, 0)),
                  pl.BlockSpec((tr, T), lambda i, b: (i, 0))],
        out_specs=pl.BlockSpec((None, tr, MIX_SEC), lambda i, b: (b, i, 0)),
        out_shape=jax.ShapeDtypeStruct((B, T, MIX_SEC), BF16),
        scratch_shapes=[pltpu.VMEM((B, T, MIX_SEC), BF16), pltpu.VMEM((B, T, MIX_SEC), BF16)],
        compiler_params=_cparams(("arbitrary", "arbitrary")),
        name="fourier",
    )(x, cg, sg, ct, st)


def _diffattn_kernel(*refs, lam_init, has_ctx):
    if has_ctx:
        q_ref, k_ref, v_ref, kc_ref, vc_ref, lam_ref, o_ref = refs
    else:
        q_ref, k_ref, v_ref, lam_ref, o_ref = refs
    lp = lam_ref[...]
    lam = (jnp.exp(jnp.sum(lp[0:1, :] * lp[1:2, :], axis=-1, keepdims=True))
           - jnp.exp(jnp.sum(lp[2:3, :] * lp[3:4, :], axis=-1, keepdims=True)) + lam_init)
    q = q_ref[...]
    pieces = [(k_ref[...].astype(BF16), v_ref[...].astype(BF16))]
    if has_ctx:
        pieces.append((kc_ref[...].astype(BF16), vc_ref[...].astype(BF16)))
    lane = lax.broadcasted_iota(jnp.int32, q.shape, 1)

    def scores(hd):
        maps = []
        for mp in range(2):
            lo = (hd * 2 + mp) * DA_QK
            qm = jnp.where((lane >= lo) & (lane < lo + DA_QK), q, jnp.zeros_like(q))
            maps.append([lax.dot_general(qm, kk, (((1,), (1,)), ((), ())), preferred_element_type=F32)
                         for kk, _ in pieces])
        return maps

    out = jnp.zeros(q.shape, F32)
    s_next = scores(0)
    for hd in range(HEADS):
        s_cur = s_next
        if hd + 1 < HEADS:
            s_next = scores(hd + 1)
        probs = []
        for mp in range(2):
            ss = s_cur[mp]
            mx = functools.reduce(jnp.maximum, [jnp.max(s, axis=-1, keepdims=True) for s in ss])
            es = [jnp.exp2(s - mx) for s in ss]
            den = functools.reduce(jnp.add, [jnp.sum(e, axis=-1, keepdims=True) for e in es])
            probs.append((es, 1.0 / den))
        (e0, r0), (e1, r1) = probs
        r1 = lam * r1
        acc = None
        for j, (_, vv) in enumerate(pieces):
            w = (e0[j] * r0 - e1[j] * r1).astype(BF16)
            part = jnp.dot(w, vv, preferred_element_type=F32)
            acc = part if acc is None else acc + part
        out = jnp.where((lane >= hd * HEAD_DIM) & (lane < (hd + 1) * HEAD_DIM), acc, out)
    o_ref[...] = out.astype(o_ref.dtype)


def _diffattn(q, k, v, ctx, lam_par, lam_init):
    B, T, _ = q.shape
    tq = min(T, 512)
    has_ctx = ctx is not None
    in_specs = [pl.BlockSpec((None, tq, MIX_SEC), lambda b, i: (b, i, 0)),
                pl.BlockSpec((None, T, MIX_SEC), lambda b, i: (b, 0, 0)),
                pl.BlockSpec((None, T, MIX_SEC), lambda b, i: (b, 0, 0))]
    args = [q, k, v]
    if has_ctx:
        ck, cv, li = ctx
        P = ck.shape[2]
        in_specs += [pl.BlockSpec((None, None, P, MIX_SEC), lambda b, i: (b, li, 0, 0))] * 2
        args += [ck, cv]
    in_specs.append(pl.BlockSpec(lam_par.shape, lambda b, i: (0, 0)))
    args.append(lam_par)
    return pl.pallas_call(
        functools.partial(_diffattn_kernel, lam_init=lam_init, has_ctx=has_ctx),
        grid=(B, T // tq),
        in_specs=in_specs,
        out_specs=pl.BlockSpec((None, tq, MIX_SEC), lambda b, i: (b, i, 0)),
        out_shape=jax.ShapeDtypeStruct((B, T, MIX_SEC), BF16),
        compiler_params=_cparams(("parallel", "arbitrary")),
        name="diff_attention",
    )(*args)


def _mixout_kernel(x_ref, mod_ref, gpost_ref, hml_ref, og_ref, odn_ref, dng_ref, ft_ref, oda_ref,
                   gains_ref, seg_ref, w_ref, o_ref, *, da_scale):
    def head_norm(z, gain_row):
        return z * lax.rsqrt(_head_mean_sq(z, seg_ref) + EPS) * gain_row

    f32 = lambda ref_slice: ref_slice.astype(F32)
    h_ml = (f32(hml_ref[0]) + f32(hml_ref[1])) * jax.nn.sigmoid(f32(og_ref[...]))
    y_ml = head_norm(h_ml, gains_ref[0:1, :])
    y_dn = head_norm(f32(odn_ref[0]) + f32(odn_ref[1]), gains_ref[1:2, :]) * _silu(f32(dng_ref[...]))
    y_da = head_norm(f32(oda_ref[...]), gains_ref[2:3, :]) * da_scale
    y = (_mm(y_ml, w_ref[0:MIX_SEC, :]) + _mm(y_dn, w_ref[MIX_SEC:2 * MIX_SEC, :])
         + _mm(ft_ref[...], w_ref[2 * MIX_SEC:3 * MIX_SEC, :]) + _mm(y_da, w_ref[3 * MIX_SEC:4 * MIX_SEC, :]))
    o_ref[...] = x_ref[...] + mod_ref[5:6, :] * (_rms(y) * gpost_ref[...])


def _mixout(x, mod, g_post, h_ml, ml, o_dn, dn_g, y_ft, o_da, gains, seg, w_out, da_scale):
    B, T, _ = x.shape
    tm = min(T, 512)
    tpb = T // tm
    n_tok = B * T
    x2 = lambda a: a.reshape(n_tok, a.shape[-1])
    h_ml = h_ml.reshape(2, n_tok, MIX_SEC)
    o_dn = o_dn.reshape(2, n_tok, MIX_SEC)
    tok = lambda w: pl.BlockSpec((tm, w), lambda i: (i, 0))
    two = pl.BlockSpec((2, tm, MIX_SEC), lambda i: (0, i, 0))
    full = lambda a: pl.BlockSpec(a.shape, lambda i: (0,) * a.ndim)
    out = pl.pallas_call(
        functools.partial(_mixout_kernel, da_scale=da_scale),
        grid=(n_tok // tm,),
        in_specs=[tok(D_MODEL), pl.BlockSpec((None, N_MOD, D_MODEL), lambda i: (i // tpb, 0, 0)), full(g_post),
                  two, tok(MIX_SEC), two, tok(MIX_SEC), tok(MIX_SEC), tok(MIX_SEC),
                  full(gains), full(seg), full(w_out)],
        out_specs=tok(D_MODEL),
        out_shape=jax.ShapeDtypeStruct((n_tok, D_MODEL), F32),
        compiler_params=_cparams(("parallel",)),
        name="mix_out",
    )(x2(x), mod, g_post, h_ml, x2(ml), o_dn, x2(dn_g), x2(y_ft), x2(o_da), gains, seg, w_out)
    return out.reshape(x.shape)


def _rope_tables(T):
    t = np.arange(T)
    row = (t // GRID_W).astype(np.float64)
    col = (t % GRID_W).astype(np.float64)
    inv = ROPE_BASE ** (-np.arange(ROPE_AXIS_PAIRS, dtype=np.float64) / ROPE_AXIS_PAIRS)
    ang = np.concatenate([row[:, None] * inv, col[:, None] * inv], axis=-1)
    cos = np.repeat(np.cos(ang), 2, axis=-1)
    sin = np.repeat(np.sin(ang), 2, axis=-1) * np.tile(np.array([-1.0, 1.0]), DA_QK // 2)
    reps = MIX_SEC // DA_QK
    return jnp.asarray(np.tile(cos, (1, reps)), dtype=F32), jnp.asarray(np.tile(sin, (1, reps)), dtype=F32)


def _layer_params(li, w_in, w_out, ffn_w_in, ffn_w_out, norm_pre, norm_post, ml_i_bias, ml_f_bias, ml_norm,
                  dn_conv, dn_A_log, dn_dt_bias, dn_norm, da_lambda, da_norm):
    w = w_in[li]
    main = jnp.concatenate([w[:, _OFF_ML:_OFF_MLG], w[:, _OFF_DN:_OFF_DNG], w[:, _OFF_FT:IN_WIDTH]], axis=1)
    gate = jnp.concatenate([w[:, _OFF_MLG:_OFF_DN], w[:, _OFF_DNG:_OFF_FT]], axis=1)
    on_lanes = lambda vec, lo: jnp.pad(vec.reshape(-1), (lo, LANES - lo - 2 * HEADS))
    ml_bias = jnp.stack([on_lanes(ml_i_bias[li], GATE_ML_I), on_lanes(ml_f_bias[li], GATE_ML_F)])
    dn_par_row = jnp.stack([on_lanes(-jnp.exp(dn_A_log[li]), GATE_DN_A), on_lanes(dn_dt_bias[li], GATE_DN_A)])
    return dict(
        w_main=main.astype(BF16),
        w_gate=jnp.pad(gate, ((0, 0), (0, LANES - N_GATE))).astype(BF16),
        w_out=w_out[li].astype(BF16),
        ffn_w_in=[ffn_w_in[li, j].astype(BF16) for j in range(2)],
        ffn_w_out=[ffn_w_out[li, j].astype(BF16) for j in range(2)],
        norm_pre=[norm_pre[li, j][None, :] for j in range(3)],
        norm_post=[norm_post[li, j][None, :] for j in range(3)],
        ml_bias_row=ml_bias,
        dn_par_row=dn_par_row,
        dn_conv=dn_conv[li],
        gains=jnp.stack([ml_norm[li].reshape(-1), jnp.tile(dn_norm[li], HEADS), jnp.tile(da_norm[li], HEADS)]),
        da_lambda=da_lambda[li],
    )


def _packed_consts():
    seg = np.kron(np.eye(HEADS), np.ones((HEAD_DIM, HEAD_DIM)))

    def spread(first_lanes):
        sel = np.zeros((4, LANES, MIX_SEC))
        for j, lo in enumerate(first_lanes):
            for hd in range(HEADS):
                sel[j, lo + hd, hd * HEAD_DIM:(hd + 1) * HEAD_DIM] = 1.0
        return jnp.asarray(sel, dtype=BF16)

    return dict(seg=jnp.asarray(seg, dtype=BF16), segf=jnp.asarray(seg, dtype=F32),
                sel_ml=spread((GATE_ML_I, GATE_ML_I + HEADS, GATE_ML_F, GATE_ML_F + HEADS)),
                sel_dn=spread((GATE_DN_A, GATE_DN_A + HEADS, GATE_DN_B, GATE_DN_B + HEADS)))


def _mixing(x_tok, mod_tok, shape_bt, lp, li, rope_tabs, states, ctx, consts):
    B, T = shape_bt
    seg, segf = consts['seg'], consts['segf']
    lam_init = 0.8 - 0.6 * math.exp(-0.3 * li)
    ml, ml_o, dn, dn_g, ft, da_q, da_k, da_v, g_col = _inproj(
        x_tok, mod_tok, lp['norm_pre'][1], lp['w_main'], lp['w_gate'], rope_tabs)
    seq = lambda a: a.reshape(B, T, a.shape[-1])
    c0, n0, m0, s0 = states
    h_ml, c_f, n_f, m_f = _mlstm(seq(ml), seq(g_col), lp['ml_bias_row'], seg, segf, consts['sel_ml'], c0, n0, m0)
    o_dn, s_f = _deltanet(seq(dn), seq(g_col), lp['dn_conv'], lp['dn_par_row'], seg, segf, consts['sel_dn'], s0)
    y_ft = _fourier(seq(ft))
    o_da = _diffattn(seq(da_q), seq(da_k), seq(da_v), ctx, lp['da_lambda'], lam_init)
    x_new = _mixout(x_tok, mod_tok, lp['norm_post'][1], h_ml, ml_o, o_dn, dn_g, y_ft, o_da, lp['gains'], seg,
                    lp['w_out'], 1.0 - lam_init)
    return x_new, (seq(da_k), seq(da_v), c_f, n_f, m_f, s_f)


def _trunk_layer(x_tok, mod_tok, shape_bt, lp, li, rope_tabs, states, ctx, consts):
    x_tok = _ffn_block(x_tok, mod_tok, lp['norm_pre'][0], lp['norm_post'][0], lp['ffn_w_in'][0], lp['ffn_w_out'][0], 0)
    x_tok, aux = _mixing(x_tok, mod_tok, shape_bt, lp, li, rope_tabs, states, ctx, consts)
    x_tok = _ffn_block(x_tok, mod_tok, lp['norm_pre'][2], lp['norm_post'][2], lp['ffn_w_in'][1], lp['ffn_w_out'][1], 6)
    return x_tok, aux


def kernel(x_prompt, x_sample, cache_k, cache_v, state_mlstm_C, state_mlstm_n, state_mlstm_m, state_delta_S, c, c_ctx, w_mod, b_mod, norm_pre, norm_post, ffn_w_in, ffn_w_out, w_in, w_out, ml_i_bias, ml_f_bias, ml_norm, dn_conv, dn_A_log, dn_dt_bias, dn_norm, da_lambda, da_norm):
    Bp, Tp, _ = x_prompt.shape
    Bs, Ts, _ = x_sample.shape
    P = cache_k.shape[2]
    n_st = 2 * HEADS

    cond = jnp.concatenate([c, c_ctx[None, :]], axis=0)
    cond = jnp.pad(cond, ((0, (-cond.shape[0]) % 8), (0, 0)))
    mod = _modulation(cond, w_mod, b_mod)
    rope_tabs = _rope_tables(Ts)
    consts = _packed_consts()
    ck = cache_k.reshape(Bs, DEPTH, P, MIX_SEC)
    cv = cache_v.reshape(Bs, DEPTH, P, MIX_SEC)
    zeros_p = (jnp.zeros((Bp, n_st, HEAD_DIM, HEAD_DIM), F32), jnp.zeros((Bp, n_st, 1, HEAD_DIM), F32),
               jnp.zeros((Bp, n_st, 1, 1), F32), jnp.zeros((Bp, n_st, HEAD_DIM, HEAD_DIM), F32))

    xp = x_prompt.reshape(1, Bp * Tp, D_MODEL)
    xs = x_sample
    ks, vs, Cs, ns, ms, Ss = [], [], [], [], [], []
    for li in range(DEPTH):
        lp = _layer_params(li, w_in, w_out, ffn_w_in, ffn_w_out, norm_pre, norm_post, ml_i_bias, ml_f_bias, ml_norm,
                           dn_conv, dn_A_log, dn_dt_bias, dn_norm, da_lambda, da_norm)
        mod_s = jnp.transpose(mod[li, :, :Bs], (1, 0, 2))
        mod_p = mod[li, :, Bs:Bs + 1].reshape(1, N_MOD, D_MODEL)
        xp, (k_c, v_c, C_c, n_c, m_c, S_c) = _trunk_layer(xp, mod_p, (Bp, Tp), lp, li, None, zeros_p, None, consts)
        ks.append(k_c.reshape(Bp, Tp, HEADS, 2, DA_QK))
        vs.append(v_c.reshape(Bp, Tp, HEADS, HEAD_DIM))
        Cs.append(C_c.reshape(Bp, 2, HEADS, HEAD_DIM, HEAD_DIM))
        ns.append(n_c.reshape(Bp, 2, HEADS, HEAD_DIM))
        ms.append(m_c.reshape(Bp, 2, HEADS))
        Ss.append(S_c.reshape(Bp, 2, HEADS, HEAD_DIM, HEAD_DIM))
        states = (state_mlstm_C[:, li].reshape(Bs, n_st, HEAD_DIM, HEAD_DIM),
                  state_mlstm_n[:, li].reshape(Bs, n_st, 1, HEAD_DIM),
                  state_mlstm_m[:, li].reshape(Bs, n_st, 1, 1),
                  state_delta_S[:, li].reshape(Bs, n_st, HEAD_DIM, HEAD_DIM))
        xs, _ = _trunk_layer(xs, mod_s, (Bs, Ts), lp, li, rope_tabs, states, (ck, cv, li), consts)
    return (xp.reshape(Bp, Tp, D_MODEL), xs, jnp.stack(ks, axis=1), jnp.stack(vs, axis=1), jnp.stack(Cs, axis=1),
            jnp.stack(ns, axis=1), jnp.stack(ms, axis=1), jnp.stack(Ss, axis=1))
```
